```python
import jax, jax.numpy as jnp
from jax import lax
import numpy as np

D_MODEL = 2048
BATCH = 32
SEQ = 256
DEPTH = 4
DEC_BATCH = 8
DEC_SEQ = 1024
PAST_LEN = 512

GRID_W = 64
N_MIXERS = 3
N_MLSTM_LAYERS = (DEPTH + 2) // 3
N_FOURIER_LAYERS = (DEPTH + 1) // 3
N_LRU_LAYERS = DEPTH // 3
MLSTM_HEADS = 4
MLSTM_DK = D_MODEL // 8
MLSTM_DV = D_MODEL // 4
MLSTM_CHUNK = 64
FNET_GROUPS = 8
FNET_GC = D_MODEL // FNET_GROUPS
LRU_WIDTH = D_MODEL
LRU_BLOCKS = 8
LRU_BS = LRU_WIDTH // LRU_BLOCKS
CONV_W = 4
LRU_C = 8.0
D_FF = 4 * D_MODEL
EPS = 1e-6

kernel_name = 'hybrid_mlstm_fnet_rglru_diffusion_step'


def rms_norm(x, g):
    xf = x.astype(jnp.float32)
    y = xf * lax.rsqrt(jnp.mean(xf * xf, axis=-1, keepdims=True) + EPS)
    return (y * g.astype(jnp.float32)).astype(x.dtype)


def modulation(cond, w, b):
    m = jax.nn.silu(cond) @ w + b
    return jnp.split(m[..., None, :], 6, axis=-1)


def _flip(a):
    return jnp.flip(a, axis=1)


def mlstm_chunkwise(q, k, v, log_i, log_f, C0, n0, m0):
    B, T, H, _ = q.shape
    L = MLSTM_CHUNK
    nc = T // L

    def to_chunks(a):
        return jnp.moveaxis(a.reshape((B, nc, L) + a.shape[2:]), 1, 0)

    causal = jnp.tril(jnp.ones((L, L), dtype=bool))

    def step(carry, inp):
        C, n, m = carry
        qc, kc, vc, li, lf = inp
        b = jnp.cumsum(lf, axis=1)
        Dm = b[:, :, None, :] - b[:, None, :, :] + li[:, None, :, :]
        Dm = jnp.where(causal[None, :, :, None], Dm, -jnp.inf)
        inter = b + m[:, None, :]
        m_t = jnp.maximum(inter, jnp.max(Dm, axis=2))
        w_inter = jnp.exp(inter - m_t)
        s = jnp.einsum('bthd,bshd->btsh', qc, kc) * jnp.exp(Dm - m_t[:, :, None, :])
        num = jnp.einsum('btsh,bshv->bthv', s, vc) + w_inter[..., None] * jnp.einsum('bthd,bhdv->bthv', qc, C)
        den = jnp.sum(s, axis=2) + w_inter * jnp.einsum('bthd,bhd->bth', qc, n)
        h = num / jnp.maximum(jnp.abs(den), jnp.exp(-m_t))[..., None]
        bL = b[:, -1, :]
        g = bL[:, None, :] - b + li
        m_new = jnp.maximum(bL + m, jnp.max(g, axis=1))
        wC = jnp.exp(bL + m - m_new)
        ws = jnp.exp(g - m_new[:, None, :])
        C_new = wC[..., None, None] * C + jnp.einsum('bsh,bshd,bshv->bhdv', ws, kc, vc)
        n_new = wC[..., None] * n + jnp.einsum('bsh,bshd->bhd', ws, kc)
        return (C_new, n_new, m_new), h

    (C, n, m), hs = lax.scan(step, (C0, n0, m0), tuple(map(to_chunks, (q, k, v, log_i, log_f))))
    h = jnp.moveaxis(hs, 0, 1).reshape(B, T, H, v.shape[-1])
    return h, C, n, m


def mlstm_mixer(h, w_in, b_gate, norm_g, w_out, C0, n0, m0):
    B, T, _ = h.shape
    f32 = jnp.float32
    HK = MLSTM_HEADS * MLSTM_DK
    HV = MLSTM_HEADS * MLSTM_DV
    q, k, v, o, g = jnp.split(h @ w_in, [HK, 2 * HK, 2 * HK + HV, 2 * HK + 2 * HV], axis=-1)
    q = q.astype(f32).reshape(B, T, MLSTM_HEADS, MLSTM_DK)
    k = k.astype(f32).reshape(B, T, MLSTM_HEADS, MLSTM_DK) * (MLSTM_DK ** -0.5)
    v = v.astype(f32).reshape(B, T, MLSTM_HEADS, MLSTM_DV)
    g = g.astype(f32).reshape(B, T, 2, 2, MLSTM_HEADS) + b_gate.astype(f32)
    hs, Cs, ns, ms = [], [], [], []
    for d in range(2):
        li = g[:, :, d, 0]
        lf = jax.nn.log_sigmoid(g[:, :, d, 1])
        args = (q, k, v, li, lf)
        if d == 1:
            args = tuple(map(_flip, args))
        hd, C, n, m = mlstm_chunkwise(*args, C0[:, d].astype(f32), n0[:, d].astype(f32), m0[:, d].astype(f32))
        hs.append(_flip(hd) if d == 1 else hd)
        Cs.append(C); ns.append(n); ms.append(m)
    hsum = hs[0] + hs[1]
    hsum = hsum * lax.rsqrt(jnp.mean(hsum * hsum, axis=-1, keepdims=True) + EPS)
    y = hsum.reshape(B, T, HV) * norm_g.astype(f32) * jax.nn.sigmoid(o.astype(f32))
    out = y.astype(h.dtype) @ w_out
    return out, jnp.stack(Cs, 1), jnp.stack(ns, 1), jnp.stack(ms, 1)


def fourier_mixer(h, w_out, b_out):
    B, T, D = h.shape
    hg = h.astype(jnp.float32).reshape(B, T, FNET_GROUPS, FNET_GC)
    mixed = jnp.fft.fft2(hg, axes=(1, 3), norm='ortho').real
    return mixed.reshape(B, T, D).astype(h.dtype) @ w_out + b_out


def depthwise_conv_centred(x, w, b):
    pad_l = (CONV_W - 1) // 2
    y = lax.conv_general_dilated(x, w[:, None, :].astype(x.dtype), (1,), [(pad_l, CONV_W - 1 - pad_l)],
                                 dimension_numbers=('NWC', 'WIO', 'NWC'), feature_group_count=x.shape[-1])
    return y + b


def _lin_combine(e1, e2):
    a1, b1 = e1
    a2, b2 = e2
    return a1 * a2, a2 * b1 + b2


def rglru_mixer(h, w_in, conv_w, conv_b, gate_w, gate_b, lam, w_out, h0):
    B, T, _ = h.shape
    f32 = jnp.float32
    gate_br, xb = jnp.split(h @ w_in, 2, axis=-1)
    xb = depthwise_conv_centred(xb, conv_w, conv_b).astype(f32)
    pre = jnp.einsum('btnk,dgnkj->btdgnj', xb.reshape(B, T, LRU_BLOCKS, LRU_BS), gate_w.astype(f32))
    pre = pre.reshape(B, T, 2, 2, LRU_WIDTH) + gate_b.astype(f32)
    r = jax.nn.sigmoid(pre[:, :, :, 0])
    ig = jax.nn.sigmoid(pre[:, :, :, 1])
    log_a = LRU_C * r * jax.nn.log_sigmoid(lam.astype(f32))
    a = jnp.exp(log_a)
    u = jnp.sqrt(-jnp.expm1(2.0 * log_a)) * (ig * xb[:, :, None, :])
    a = jnp.stack([a[:, :, 0], _flip(a[:, :, 1])], axis=2)
    u = jnp.stack([u[:, :, 0], _flip(u[:, :, 1])], axis=2)
    u = u.at[:, 0].add(a[:, 0] * h0.astype(f32))
    _, hs = lax.associative_scan(_lin_combine, (a, u), axis=1)
    y = hs[:, :, 0] + _flip(hs[:, :, 1])
    out = (y * jax.nn.gelu(gate_br.astype(f32))).astype(h.dtype) @ w_out
    return out, hs[:, -1]


def sq_relu_mlp(h, w_up, w_down):
    return jnp.square(jax.nn.relu(h @ w_up)) @ w_down


def trunk(x, cond, C0, n0, m0, h0, mod_w, mod_b, norm_g, ffn_w_up, ffn_w_down,
          mlstm_w_in, mlstm_b_gate, mlstm_norm_g, mlstm_w_out, fnet_w_out, fnet_b_out,
          lru_w_in, lru_conv_w, lru_conv_b, lru_gate_w, lru_gate_b, lru_lambda, lru_w_out):
    Cs, ns, ms, hs = [], [], [], []
    for i in range(DEPTH):
        sh1, sc1, g1, sh2, sc2, g2 = modulation(cond, mod_w[i], mod_b[i])
        hin = rms_norm(x, norm_g[i, 0]) * (1 + sc1) + sh1
        kind, j = i % N_MIXERS, i // N_MIXERS
        if kind == 0:
            out, C, n, m = mlstm_mixer(hin, mlstm_w_in[j], mlstm_b_gate[j], mlstm_norm_g[j], mlstm_w_out[j],
                                       C0[:, j], n0[:, j], m0[:, j])
            Cs.append(C); ns.append(n); ms.append(m)
        elif kind == 1:
            out = fourier_mixer(hin, fnet_w_out[j], fnet_b_out[j])
        else:
            out, hl = rglru_mixer(hin, lru_w_in[j], lru_conv_w[j], lru_conv_b[j], lru_gate_w[j], lru_gate_b[j],
                                  lru_lambda[j], lru_w_out[j], h0[:, j])
            hs.append(hl)
        x = x + g1 * rms_norm(out.astype(x.dtype), norm_g[i, 1])
        hin = rms_norm(x, norm_g[i, 2]) * (1 + sc2) + sh2
        x = x + g2 * rms_norm(sq_relu_mlp(hin, ffn_w_up[i], ffn_w_down[i]).astype(x.dtype), norm_g[i, 3])
    return x, jnp.stack(Cs, 1), jnp.stack(ns, 1), jnp.stack(ms, 1), jnp.stack(hs, 1)


def setup_inputs(seed: int = 0) -> dict:
    key = jax.random.key(seed)
    ks = jax.random.split(key, 32)
    f32 = jnp.float32

    def nrm(k, shape, s):
        return jax.random.normal(k, shape, f32) * s

    D = D_MODEL
    H = MLSTM_HEADS
    HK = H * MLSTM_DK
    HV = H * MLSTM_DV
    NA, NB, NC = N_MLSTM_LAYERS, N_FOURIER_LAYERS, N_LRU_LAYERS
    gate_base = jnp.stack([jnp.zeros((H,), f32), jnp.linspace(3.0, 6.0, H, dtype=f32)])
    a_target = jax.random.uniform(ks[24], (NC, 2, LRU_WIDTH), f32, 0.9, 0.999)
    a0 = a_target ** (1.0 / LRU_C)
    return {
        'x_prompt': nrm(ks[0], (BATCH, SEQ, D), 1.0),
        'x_sample': nrm(ks[1], (DEC_BATCH, DEC_SEQ, D), 1.0),
        'c': nrm(ks[2], (DEC_BATCH, D), 1.0),
        'state_mlstm_C': nrm(ks[3], (DEC_BATCH, NA, 2, H, MLSTM_DK, MLSTM_DV), 1.0),
        'state_mlstm_n': nrm(ks[4], (DEC_BATCH, NA, 2, H, MLSTM_DK), 1.0),
        'state_mlstm_m': nrm(ks[5], (DEC_BATCH, NA, 2, H), 1.0),
        'state_lru_h': nrm(ks[6], (DEC_BATCH, NC, 2, LRU_WIDTH), 1.0),
        'c_ctx': nrm(ks[7], (D,), 1.0),
        'mod_w': nrm(ks[8], (DEPTH, D, 6 * D), 0.5 * D ** -0.5),
        'mod_b': nrm(ks[9], (DEPTH, 6 * D), 0.05),
        'norm_g': 1.0 + nrm(ks[10], (DEPTH, 4, D), 0.1),
        'ffn_w_up': nrm(ks[11], (DEPTH, D, D_FF), D ** -0.5),
        'ffn_w_down': nrm(ks[12], (DEPTH, D_FF, D), D_FF ** -0.5),
        'mlstm_w_in': nrm(ks[13], (NA, D, 2 * HK + 2 * HV + 4 * H), D ** -0.5),
        'mlstm_b_gate': nrm(ks[14], (NA, 2, 2, H), 0.1) + gate_base[None, None],
        'mlstm_norm_g': 1.0 + nrm(ks[15], (NA, HV), 0.1),
        'mlstm_w_out': nrm(ks[16], (NA, HV, D), HV ** -0.5),
        'fnet_w_out': nrm(ks[17], (NB, D, D), D ** -0.5),
        'fnet_b_out': nrm(ks[18], (NB, D), 0.02),
        'lru_w_in': nrm(ks[19], (NC, D, 2 * LRU_WIDTH), D ** -0.5),
        'lru_conv_w': nrm(ks[20], (NC, CONV_W, LRU_WIDTH), CONV_W ** -0.5),
        'lru_conv_b': nrm(ks[21], (NC, LRU_WIDTH), 0.02),
        'lru_gate_w': nrm(ks[22], (NC, 2, 2, LRU_BLOCKS, LRU_BS, LRU_BS), LRU_BS ** -0.5),
        'lru_gate_b': nrm(ks[23], (NC, 2, 2, LRU_WIDTH), 0.02),
        'lru_lambda': jnp.log(a0) - jnp.log1p(-a0),
        'lru_w_out': nrm(ks[25], (NC, LRU_WIDTH, D), LRU_WIDTH ** -0.5),
    }


def reference(x_prompt, x_sample, c, state_mlstm_C, state_mlstm_n, state_mlstm_m, state_lru_h, c_ctx,
              mod_w, mod_b, norm_g, ffn_w_up, ffn_w_down, mlstm_w_in, mlstm_b_gate, mlstm_norm_g, mlstm_w_out,
              fnet_w_out, fnet_b_out, lru_w_in, lru_conv_w, lru_conv_b, lru_gate_w, lru_gate_b, lru_lambda,
              lru_w_out):
    weights = (mod_w, mod_b, norm_g, ffn_w_up, ffn_w_down, mlstm_w_in, mlstm_b_gate, mlstm_norm_g, mlstm_w_out,
               fnet_w_out, fnet_b_out, lru_w_in, lru_conv_w, lru_conv_b, lru_gate_w, lru_gate_b, lru_lambda,
               lru_w_out)
    Bp = x_prompt.shape[0]
    f32 = jnp.float32
    zC = jnp.zeros((Bp, N_MLSTM_LAYERS, 2, MLSTM_HEADS, MLSTM_DK, MLSTM_DV), f32)
    zn = jnp.zeros((Bp, N_MLSTM_LAYERS, 2, MLSTM_HEADS, MLSTM_DK), f32)
    zm = jnp.zeros((Bp, N_MLSTM_LAYERS, 2, MLSTM_HEADS), f32)
    zh = jnp.zeros((Bp, N_LRU_LAYERS, 2, LRU_WIDTH), f32)
    y_prompt, new_C, new_n, new_m, new_h = trunk(x_prompt, c_ctx, zC, zn, zm, zh, *weights)
    y_sample, _, _, _, _ = trunk(x_sample, c, state_mlstm_C, state_mlstm_n, state_mlstm_m, state_lru_h, *weights)
    return (y_prompt, y_sample, new_C, new_n, new_m, new_h)
```

```python
import functools
import math

import numpy as np
import jax
import jax.numpy as jnp
from jax import lax
from jax.experimental import pallas as pl
from jax.experimental.pallas import tpu as pltpu

F32 = jnp.float32
BF16 = jnp.bfloat16
EPS = 1e-6
FNET_GROUPS = 8
LRU_C = 8.0
LANES = 128
MLSTM_L = 128
VMEM_LIMIT = 48 * 1024 * 1024
N_COND = 16


def _cparams(*sem):
    return pltpu.CompilerParams(dimension_semantics=sem, vmem_limit_bytes=VMEM_LIMIT)


def _sigmoid(x):
    return 1.0 / (1.0 + jnp.exp(-x))


def _log_sigmoid(x):
    return jnp.minimum(x, 0.0) - jnp.log1p(jnp.exp(-jnp.abs(x)))


def _gelu_tanh(x):
    return 0.5 * x * (1.0 + jnp.tanh(math.sqrt(2.0 / math.pi) * (x + 0.044715 * (x * x * x))))


def _rms(x):
    return x * lax.rsqrt(jnp.mean(x * x, axis=-1, keepdims=True) + EPS)


def _norm_mod(x, g, shift, scale):
    return (_rms(x) * g) * (1.0 + scale) + shift


def _cond_of_block(i, tm, rows_ctx, rows_per_sample):
    r = i * tm
    return jnp.where(r < rows_ctx, 0, 1 + (r - rows_ctx) // rows_per_sample)


def _mod_kernel(c_ref, w_ref, b_ref, o_ref):
    c = c_ref[...]
    a = (c * _sigmoid(c)).astype(BF16)
    o_ref[...] = jnp.dot(a, w_ref[...].astype(BF16), preferred_element_type=F32) + b_ref[...]


def _modulation(cond, mod_w, mod_b):
    depth, d, n = mod_w.shape
    tn = min(n, 1024)
    return pl.pallas_call(
        _mod_kernel,
        grid=(depth, n // tn),
        in_specs=[pl.BlockSpec((N_COND, d), lambda l, j: (0, 0)),
                  pl.BlockSpec((None, d, tn), lambda l, j: (l, 0, j)),
                  pl.BlockSpec((None, 1, tn), lambda l, j: (l, 0, j))],
        out_specs=pl.BlockSpec((None, N_COND, tn), lambda l, j: (l, 0, j)),
        out_shape=jax.ShapeDtypeStruct((depth, N_COND, n), F32),
        compiler_params=_cparams("parallel", "parallel"),
        name="modulation",
    )(cond, mod_w, mod_b.reshape(depth, 1, n))


def _pm_kernel(x_ref, g_ref, mod_ref, w_ref, o_ref, h_ref, *, sh, sc):
    @pl.when(pl.program_id(1) == 0)
    def _():
        h_ref[...] = _norm_mod(x_ref[...], g_ref[...], mod_ref[sh:sh + 1, :], mod_ref[sc:sc + 1, :]).astype(BF16)

    o_ref[...] = jnp.dot(h_ref[...], w_ref[...], preferred_element_type=F32).astype(o_ref.dtype)


def _norm_mod_matmul(x, g, mods, w, *, sh, sc, out_dtype, tm, tn, rows_ctx, rows_per_sample):
    m, d = x.shape
    n = w.shape[1]
    cond = functools.partial(_cond_of_block, tm=tm, rows_ctx=rows_ctx, rows_per_sample=rows_per_sample)
    return pl.pallas_call(
        functools.partial(_pm_kernel, sh=sh, sc=sc),
        grid=(m // tm, n // tn),
        in_specs=[pl.BlockSpec((tm, d), lambda i, j: (i, 0)),
                  pl.BlockSpec((1, d), lambda i, j: (0, 0)),
                  pl.BlockSpec((None, 6, d), lambda i, j: (cond(i), 0, 0)),
                  pl.BlockSpec((d, tn), lambda i, j: (0, j))],
        out_specs=pl.BlockSpec((tm, tn), lambda i, j: (i, j)),
        out_shape=jax.ShapeDtypeStruct((m, n), out_dtype),
        scratch_shapes=[pltpu.VMEM((tm, d), BF16)],
        compiler_params=_cparams("parallel", "arbitrary"),
        name="norm_mod_matmul",
    )(x, g.reshape(1, d), mods, w)


def _nm_kernel(x_ref, g_ref, mod_ref, o_ref, *, sh, sc):
    o_ref[...] = _norm_mod(x_ref[...], g_ref[...], mod_ref[sh:sh + 1, :], mod_ref[sc:sc + 1, :]).astype(o_ref.dtype)


def _norm_mod_only(x, g, mods, *, sh, sc, tm, rows_ctx, rows_per_sample):
    m, d = x.shape
    cond = functools.partial(_cond_of_block, tm=tm, rows_ctx=rows_ctx, rows_per_sample=rows_per_sample)
    return pl.pallas_call(
        functools.partial(_nm_kernel, sh=sh, sc=sc),
        grid=(m // tm,),
        in_specs=[pl.BlockSpec((tm, d), lambda i: (i, 0)),
                  pl.BlockSpec((1, d), lambda i: (0, 0)),
                  pl.BlockSpec((None, 6, d), lambda i: (cond(i), 0, 0))],
        out_specs=pl.BlockSpec((tm, d), lambda i: (i, 0)),
        out_shape=jax.ShapeDtypeStruct((m, d), BF16),
        compiler_params=_cparams("parallel"),
        name="norm_mod",
    )(x, g.reshape(1, d), mods)


def _me_kernel(y_ref, w_ref, b_ref, x_ref, g_ref, mod_ref, o_ref, *, gate):
    out = jnp.dot(y_ref[...], w_ref[...], preferred_element_type=F32) + b_ref[...]
    o_ref[...] = x_ref[...] + mod_ref[gate:gate + 1, :] * (_rms(out) * g_ref[...])


def _matmul_residual(y, w, b, x, g, mods, *, gate, tm, rows_ctx, rows_per_sample):
    m, k = y.shape
    d = w.shape[1]
    cond = functools.partial(_cond_of_block, tm=tm, rows_ctx=rows_ctx, rows_per_sample=rows_per_sample)
    return pl.pallas_call(
        functools.partial(_me_kernel, gate=gate),
        grid=(m // tm,),
        in_specs=[pl.BlockSpec((tm, k), lambda i: (i, 0)),
                  pl.BlockSpec((k, d), lambda i: (0, 0)),
                  pl.BlockSpec((1, d), lambda i: (0, 0)),
                  pl.BlockSpec((tm, d), lambda i: (i, 0)),
                  pl.BlockSpec((1, d), lambda i: (0, 0)),
                  pl.BlockSpec((None, 6, d), lambda i: (cond(i), 0, 0))],
        out_specs=pl.BlockSpec((tm, d), lambda i: (i, 0)),
        out_shape=jax.ShapeDtypeStruct((m, d), F32),
        compiler_params=_cparams("parallel"),
        name="matmul_residual",
    )(y, w, b.reshape(1, d), x, g.reshape(1, d), mods)


def _ffn_kernel(x_ref, g1_ref, g2_ref, mod_ref, wu_ref, wd_ref, o_ref, h_ref, acc_ref):
    j = pl.program_id(1)

    @pl.when(j == 0)
    def _():
        h_ref[...] = _norm_mod(x_ref[...], g1_ref[...], mod_ref[3:4, :], mod_ref[4:5, :]).astype(BF16)
        acc_ref[...] = jnp.zeros_like(acc_ref)

    u = jnp.maximum(jnp.dot(h_ref[...], wu_ref[...], preferred_element_type=F32), 0.0)
    acc_ref[...] += jnp.dot((u * u).astype(BF16), wd_ref[...], preferred_element_type=F32)

    @pl.when(j == pl.num_programs(1) - 1)
    def _():
        o_ref[...] = x_ref[...] + mod_ref[5:6, :] * (_rms(acc_ref[...]) * g2_ref[...])


def _ffn(x, g_pre, g_post, mods, w_up, w_down, *, tm, tf, rows_ctx, rows_per_sample):
    m, d = x.shape
    f = w_up.shape[1]
    cond = functools.partial(_cond_of_block, tm=tm, rows_ctx=rows_ctx, rows_per_sample=rows_per_sample)
    return pl.pallas_call(
        _ffn_kernel,
        grid=(m // tm, f // tf),
        in_specs=[pl.BlockSpec((tm, d), lambda i, j: (i, 0)),
                  pl.BlockSpec((1, d), lambda i, j: (0, 0)),
                  pl.BlockSpec((1, d), lambda i, j: (0, 0)),
                  pl.BlockSpec((None, 6, d), lambda i, j: (cond(i), 0, 0)),
                  pl.BlockSpec((d, tf), lambda i, j: (0, j)),
                  pl.BlockSpec((tf, d), lambda i, j: (j, 0))],
        out_specs=pl.BlockSpec((tm, d), lambda i, j: (i, 0)),
        out_shape=jax.ShapeDtypeStruct((m, d), F32),
        scratch_shapes=[pltpu.VMEM((tm, d), BF16), pltpu.VMEM((tm, d), F32)],
        compiler_params=_cparams("parallel", "arbitrary"),
        name="ffn",
    )(x, g_pre.reshape(1, d), g_post.reshape(1, d), mods, w_up, w_down)


def _mlstm_kernel(*refs, seq, chunk, zero_init, want_state):
    it = iter(refs)
    q_ref, k_ref, v_ref, o_ref, gt_ref, gb_ref, ng_ref = (next(it) for _ in range(7))
    if not zero_init:
        c0_ref, n0_ref, m0_ref = next(it), next(it), next(it)
    y_ref = next(it)
    if want_state:
        co_ref, no_ref, mo_ref = next(it), next(it), next(it)
    hf_ref, hb_ref, c_ref, n_ref = (next(it) for _ in range(4))
    L = chunk
    nc = seq // L

    if zero_init:
        c_ref[...] = jnp.zeros_like(c_ref)
        n_ref[...] = jnp.zeros_like(n_ref)
        m_init = (jnp.zeros((1, 1), F32), jnp.zeros((1, 1), F32))
    else:
        c_ref[...] = c0_ref[...]
        n_ref[...] = n0_ref[...]
        m_init = (m0_ref[0:1, 0:1], m0_ref[1:2, 0:1])

    row = lax.broadcasted_iota(jnp.int32, (L, L), 0)
    col = lax.broadcasted_iota(jnp.int32, (L, L), 1)
    masks = (col <= row, col >= row)
    lane = lax.broadcasted_iota(jnp.int32, (L, LANES), 1)
    is_forget = (lane % 2) == 1

    def direction(d, c, m_prev):
        r0 = pl.multiple_of(c * L, L)
        rows = pl.ds(r0, L)
        q = q_ref[rows, :]
        k = k_ref[rows, :]
        v = v_ref[rows, :]
        gts = gt_ref[rows, :] + gb_ref[...]
        x = jnp.where(is_forget, _log_sigmoid(gts), gts)
        y = jnp.dot(masks[d].astype(F32), x, precision=lax.Precision.HIGHEST, preferred_element_type=F32)
        xt = x.T
        yt = y.T
        li_c = x[:, 2 * d:2 * d + 1]
        b_c = y[:, 2 * d + 1:2 * d + 2]
        a_c = li_c - b_c
        a_r = xt[2 * d:2 * d + 1, :] - yt[2 * d + 1:2 * d + 2, :]
        dm = jnp.where(masks[d], b_c + a_r, -jnp.inf)
        inter = b_c + m_prev
        m_t = jnp.maximum(inter, jnp.max(dm, axis=1, keepdims=True))
        w_inter = jnp.exp(inter - m_t)
        s = lax.dot_general(q, k, (((1,), (1,)), ((), ())), preferred_element_type=F32) * jnp.exp(dm - m_t)
        c_old = c_ref[d]
        n_old = n_ref[d]
        num = (jnp.dot(s.astype(BF16), v, preferred_element_type=F32)
               + w_inter * jnp.dot(q, c_old.astype(BF16), preferred_element_type=F32))
        qn = jnp.sum(q.astype(F32) * n_old, axis=1, keepdims=True)
        den = jnp.sum(s, axis=1, keepdims=True) + w_inter * qn
        h = num / jnp.maximum(jnp.abs(den), jnp.exp(-m_t))
        b_last = b_c[L - 1:L, :] if d == 0 else b_c[0:1, :]
        g_c = b_last + a_c
        m_new = jnp.maximum(b_last + m_prev, jnp.max(g_c, axis=0, keepdims=True))
        w_c = jnp.exp(b_last + m_prev - m_new)
        kw = k.astype(F32) * jnp.exp(g_c - m_new)
        c_ref[d] = w_c * c_old + lax.dot_general(kw.astype(BF16), v, (((0,), (0,)), ((), ())),
                                                 preferred_element_type=F32)
        n_ref[d] = w_c * n_old + jnp.sum(kw, axis=0, keepdims=True)
        return rows, h, m_new

    def body(i, carry):
        m_f, m_b = carry
        rows, h, m_f = direction(0, i, m_f)
        hf_ref[rows, :] = h
        rows, h, m_b = direction(1, nc - 1 - i, m_b)
        hb_ref[rows, :] = h
        return m_f, m_b

    m_f, m_b = lax.fori_loop(0, nc, body, m_init)

    def finish(c, _):
        rows = pl.ds(pl.multiple_of(c * L, L), L)
        hs = hf_ref[rows, :] + hb_ref[rows, :]
        y_ref[rows, :] = (_rms(hs) * ng_ref[...] * _sigmoid(o_ref[rows, :])).astype(y_ref.dtype)
        return 0

    lax.fori_loop(0, nc, finish, 0)

    if want_state:
        co_ref[...] = c_ref[...]
        no_ref[...] = n_ref[...]
        mo_ref[0:1, :] = jnp.broadcast_to(m_f, (1, LANES))
        mo_ref[1:2, :] = jnp.broadcast_to(m_b, (1, LANES))


def _mlstm_core(qkv, og, gate_b, norm_g, state, *, row0, batch, seq, heads, dk, dv, want_state):
    L = MLSTM_L
    assert seq % L == 0 and row0 % seq == 0
    b0 = row0 // seq
    hk = heads * dk
    zero_init = state is None
    in_specs = [pl.BlockSpec((seq, dk), lambda b, h: (b0 + b, h)),
                pl.BlockSpec((seq, dk), lambda b, h: (b0 + b, heads + h)),
                pl.BlockSpec((seq, dv), lambda b, h: (b0 + b, 2 * hk // dv + h)),
                pl.BlockSpec((seq, dv), lambda b, h: (b0 + b, h)),
                pl.BlockSpec((seq, LANES), lambda b, h: (b0 + b, heads * dv // LANES + h)),
                pl.BlockSpec((1, LANES), lambda b, h: (0, h)),
                pl.BlockSpec((1, dv), lambda b, h: (0, h))]
    args = [qkv, qkv, qkv, og, og, gate_b, norm_g.reshape(1, heads * dv)]
    if not zero_init:
        c0, n0, m0 = state
        in_specs += [pl.BlockSpec((None, 2, None, dk, dv), lambda b, h: (b, 0, h, 0, 0)),
                     pl.BlockSpec((None, 2, None, 1, dk), lambda b, h: (b, 0, h, 0, 0)),
                     pl.BlockSpec((None, None, 2, LANES), lambda b, h: (b, h, 0, 0))]
        args += [c0, n0, m0]
    out_specs = [pl.BlockSpec((seq, dv), lambda b, h: (b, h))]
    out_shape = [jax.ShapeDtypeStruct((batch * seq, heads * dv), BF16)]
    if want_state:
        out_specs += [pl.BlockSpec((None, 2, None, dk, dv), lambda b, h: (b, 0, h, 0, 0)),
                      pl.BlockSpec((None, 2, None, 1, dk), lambda b, h: (b, 0, h, 0, 0)),
                      pl.BlockSpec((None, None, 2, LANES), lambda b, h: (b, h, 0, 0))]
        out_shape += [jax.ShapeDtypeStruct((batch, 2, heads, dk, dv), F32),
                      jax.ShapeDtypeStruct((batch, 2, heads, 1, dk), F32),
                      jax.ShapeDtypeStruct((batch, heads, 2, LANES), F32)]
    return pl.pallas_call(
        functools.partial(_mlstm_kernel, seq=seq, chunk=L, zero_init=zero_init, want_state=want_state),
        grid=(batch, heads),
        in_specs=in_specs,
        out_specs=out_specs,
        out_shape=out_shape,
        scratch_shapes=[pltpu.VMEM((seq, dv), F32), pltpu.VMEM((seq, dv), F32),
                        pltpu.VMEM((2, dk, dv), F32), pltpu.VMEM((2, 1, dk), F32)],
        compiler_params=_cparams("parallel", "parallel"),
        name="mlstm_core",
    )(*args)


def _dft_mats(seq, gc):
    def cs(n):
        idx = np.arange(n)
        ang = 2.0 * np.pi * ((idx[:, None] * idx[None, :]) % n) / n
        return np.cos(ang) / math.sqrt(n), np.sin(ang) / math.sqrt(n)
    ct, st = cs(seq)
    cc, sc = cs(gc)
    w_time = np.concatenate([ct, -st], axis=1)
    w_chan = np.concatenate([cc, sc], axis=1)
    return jnp.asarray(w_time, BF16), jnp.asarray(w_chan, BF16)


def _fnet_kernel(h_ref, wt_ref, wc_ref, o_ref, *, gc):
    y = jnp.dot(h_ref[...], wc_ref[...], preferred_element_type=F32)
    ycat = jnp.concatenate([y[:, :gc], y[:, gc:]], axis=0).astype(BF16)
    o_ref[...] = jnp.dot(wt_ref[...], ycat, preferred_element_type=F32).astype(o_ref.dtype)


def _fnet_mix(h, *, row0, batch, seq):
    d = h.shape[1]
    gc = d // FNET_GROUPS
    b0 = row0 // seq
    w_time, w_chan = _dft_mats(seq, gc)
    return pl.pallas_call(
        functools.partial(_fnet_kernel, gc=gc),
        grid=(batch, FNET_GROUPS),
        in_specs=[pl.BlockSpec((seq, gc), lambda b, g: (b0 + b, g)),
                  pl.BlockSpec((seq, 2 * seq), lambda b, g: (0, 0)),
                  pl.BlockSpec((gc, 2 * gc), lambda b, g: (0, 0))],
        out_specs=pl.BlockSpec((seq, gc), lambda b, g: (b, g)),
        out_shape=jax.ShapeDtypeStruct((batch * seq, d), BF16),
        compiler_params=_cparams("parallel", "parallel"),
        name="fnet_mix",
    )(h, w_time, w_chan)


def _lru_kernel(*refs, seq, zero_init, want_state):
    it = iter(refs)
    br_ref, xb_ref, cw_ref, cb_ref, gw_ref, gbias_ref, lam_ref = (next(it) for _ in range(7))
    if not zero_init:
        h0_ref = next(it)
    y_ref = next(it)
    if want_state:
        hl_ref = next(it)
    a_ref, u_ref, hs_ref = (next(it) for _ in range(3))
    T = seq
    x = xb_ref[...]
    tt = lax.broadcasted_iota(jnp.int32, x.shape, 0)
    xm1 = jnp.where(tt >= 1, pltpu.roll(x, 1, 0), 0.0)
    xp1 = jnp.where(tt < T - 1, pltpu.roll(x, T - 1, 0), 0.0)
    xp2 = jnp.where(tt < T - 2, pltpu.roll(x, T - 2, 0), 0.0)
    xc = (cw_ref[0:1, :] * xm1 + cw_ref[1:2, :] * x + cw_ref[2:3, :] * xp1 + cw_ref[3:4, :] * xp2
          + cb_ref[...])
    xcb = xc.astype(BF16)
    for d in range(2):
        pre_r = jnp.dot(xcb, gw_ref[d, 0], preferred_element_type=F32) + gbias_ref[2 * d:2 * d + 1, :]
        pre_i = jnp.dot(xcb, gw_ref[d, 1], preferred_element_type=F32) + gbias_ref[2 * d + 1:2 * d + 2, :]
        log_a = LRU_C * _sigmoid(pre_r) * _log_sigmoid(lam_ref[d:d + 1, :])
        a = jnp.exp(log_a)
        a_ref[d] = a
        u_ref[d] = jnp.sqrt(-jnp.tanh(log_a) * (a * a + 1.0)) * (_sigmoid(pre_i) * xc)

    if zero_init:
        h_init = (jnp.zeros((1, x.shape[1]), F32), jnp.zeros((1, x.shape[1]), F32))
    else:
        h_init = (h0_ref[0:1, :], h0_ref[1:2, :])

    def step(t, carry):
        h_f, h_b = carry
        h_f = a_ref[0, pl.ds(t, 1), :] * h_f + u_ref[0, pl.ds(t, 1), :]
        hs_ref[0, pl.ds(t, 1), :] = h_f
        tb = T - 1 - t
        h_b = a_ref[1, pl.ds(tb, 1), :] * h_b + u_ref[1, pl.ds(tb, 1), :]
        hs_ref[1, pl.ds(tb, 1), :] = h_b
        return h_f, h_b

    h_f, h_b = lax.fori_loop(0, T, step, h_init, unroll=8)
    y_ref[...] = ((hs_ref[0] + hs_ref[1]) * _gelu_tanh(br_ref[...])).astype(y_ref.dtype)
    if want_state:
        hl_ref[0:1, :] = h_f
        hl_ref[1:2, :] = h_b


def _lru_core(gx, conv_w, conv_b, gate_w, gate_b, lam, h0, *, row0, batch, seq, want_state):
    width = conv_w.shape[1]
    nb = gate_w.shape[2]
    bs = width // nb
    b0 = row0 // seq
    zero_init = h0 is None
    in_specs = [pl.BlockSpec((seq, bs), lambda b, n: (b0 + b, n)),
                pl.BlockSpec((seq, bs), lambda b, n: (b0 + b, nb + n)),
                pl.BlockSpec((conv_w.shape[0], bs), lambda b, n: (0, n)),
                pl.BlockSpec((1, bs), lambda b, n: (0, n)),
                pl.BlockSpec((2, 2, None, bs, bs), lambda b, n: (0, 0, n, 0, 0)),
                pl.BlockSpec((4, bs), lambda b, n: (0, n)),
                pl.BlockSpec((2, bs), lambda b, n: (0, n))]
    args = [gx, gx, conv_w, conv_b.reshape(1, width), gate_w, gate_b.reshape(4, width), lam]
    if not zero_init:
        in_specs.append(pl.BlockSpec((None, 2, bs), lambda b, n: (b, 0, n)))
        args.append(h0)
    out_specs = [pl.BlockSpec((seq, bs), lambda b, n: (b, n))]
    out_shape = [jax.ShapeDtypeStruct((batch * seq, width), BF16)]
    if want_state:
        out_specs.append(pl.BlockSpec((None, 2, bs), lambda b, n: (b, 0, n)))
        out_shape.append(jax.ShapeDtypeStruct((batch, 2, width), F32))
    return pl.pallas_call(
        functools.partial(_lru_kernel, seq=seq, zero_init=zero_init, want_state=want_state),
        grid=(batch, nb),
        in_specs=in_specs,
        out_specs=out_specs,
        out_shape=out_shape,
        scratch_shapes=[pltpu.VMEM((2, seq, bs), F32)] * 3,
        compiler_params=_cparams("parallel", "parallel"),
        name="lru_core",
    )(*args)


def _mlstm_weights(w_in, b_gate, heads, dk, dv):
    d = w_in.shape[0]
    hk, hv = heads * dk, heads * dv
    wq, wk, wv, wo, wg = jnp.split(w_in, [hk, 2 * hk, 2 * hk + hv, 2 * hk + 2 * hv], axis=1)
    w_qkv = jnp.concatenate([wq, wk * (dk ** -0.5), wv], axis=1).astype(BF16)
    wg = wg.reshape(d, 2, 2, heads).transpose(0, 3, 1, 2).reshape(d, heads, 4)
    wg = jnp.pad(wg, ((0, 0), (0, 0), (0, LANES - 4))).reshape(d, heads * LANES)
    w_og = jnp.concatenate([wo, wg], axis=1).astype(BF16)
    bg = b_gate.astype(F32).transpose(2, 0, 1).reshape(heads, 4)
    bg = jnp.pad(bg, ((0, 0), (0, LANES - 4))).reshape(1, heads * LANES)
    return w_qkv, w_og, bg


def kernel(x_prompt, x_sample, c, state_mlstm_C, state_mlstm_n, state_mlstm_m, state_lru_h, c_ctx,
           mod_w, mod_b, norm_g, ffn_w_up, ffn_w_down, mlstm_w_in, mlstm_b_gate, mlstm_norm_g, mlstm_w_out,
           fnet_w_out, fnet_b_out, lru_w_in, lru_conv_w, lru_conv_b, lru_gate_w, lru_gate_b, lru_lambda,
           lru_w_out):
    bp, tp, d = x_prompt.shape
    bs_, ts, _ = x_sample.shape
    depth = mod_w.shape[0]
    heads = mlstm_b_gate.shape[-1]
    dk, dv = state_mlstm_C.shape[-2:]
    rows_ctx = bp * tp
    rows = dict(rows_ctx=rows_ctx, rows_per_sample=ts)
    assert 1 + bs_ <= N_COND
    tm = math.gcd(512, math.gcd(tp, ts))

    x = jnp.concatenate([x_prompt.reshape(rows_ctx, d), x_sample.reshape(bs_ * ts, d)], axis=0)
    cond = jnp.zeros((N_COND, d), F32).at[0].set(c_ctx).at[1:1 + bs_].set(c)
    mods_all = _modulation(cond, mod_w, mod_b).reshape(depth, N_COND, 6, d)
    zero_bias = jnp.zeros((d,), F32)
    trunks = ((0, bp, tp, True), (rows_ctx, bs_, ts, False))

    new_c, new_n, new_m, new_h = [], [], [], []
    for i in range(depth):
        mods = mods_all[i]
        kind, j = i % 3, i // 3
        if kind == 0:
            w_qkv, w_og, bg = _mlstm_weights(mlstm_w_in[j], mlstm_b_gate[j], heads, dk, dv)
            pm = functools.partial(_norm_mod_matmul, x, norm_g[i, 0], mods, sh=0, sc=1, tm=tm, **rows)
            qkv = pm(w_qkv, out_dtype=BF16, tn=math.gcd(w_qkv.shape[1], 1024))
            og = pm(w_og, out_dtype=F32, tn=w_og.shape[1] // 2)
            ys = []
            for row0, nb, seq, is_ctx in trunks:
                if is_ctx:
                    state = None
                else:
                    m0 = jnp.broadcast_to(state_mlstm_m[:, j].transpose(0, 2, 1)[..., None],
                                          (nb, heads, 2, LANES))
                    state = (state_mlstm_C[:, j], state_mlstm_n[:, j].reshape(nb, 2, heads, 1, dk), m0)
                res = _mlstm_core(qkv, og, bg, mlstm_norm_g[j], state, row0=row0, batch=nb, seq=seq,
                                  heads=heads, dk=dk, dv=dv, want_state=is_ctx)
                ys.append(res[0])
                if is_ctx:
                    new_c.append(res[1])
                    new_n.append(res[2].reshape(nb, 2, heads, dk))
                    new_m.append(res[3][..., 0].transpose(0, 2, 1))
            y = jnp.concatenate(ys, axis=0)
            w_out, b_out = mlstm_w_out[j].astype(BF16), zero_bias
        elif kind == 1:
            h = _norm_mod_only(x, norm_g[i, 0], mods, sh=0, sc=1, tm=tm, **rows)
            y = jnp.concatenate([_fnet_mix(h, row0=row0, batch=nb, seq=seq) for row0, nb, seq, _ in trunks],
                                axis=0)
            w_out, b_out = fnet_w_out[j].astype(BF16), fnet_b_out[j]
        else:
            w_in = lru_w_in[j].astype(BF16)
            gx = _norm_mod_matmul(x, norm_g[i, 0], mods, w_in, sh=0, sc=1, out_dtype=F32, tm=tm,
                                  tn=math.gcd(w_in.shape[1], 1024), **rows)
            ys = []
            for row0, nb, seq, is_ctx in trunks:
                res = _lru_core(gx, lru_conv_w[j], lru_conv_b[j], lru_gate_w[j].astype(BF16), lru_gate_b[j],
                                lru_lambda[j], None if is_ctx else state_lru_h[:, j],
                                row0=row0, batch=nb, seq=seq, want_state=is_ctx)
                ys.append(res[0])
                if is_ctx:
                    new_h.append(res[1])
            y = jnp.concatenate(ys, axis=0)
            w_out, b_out = lru_w_out[j].astype(BF16), zero_bias
        x = _matmul_residual(y, w_out, b_out, x, norm_g[i, 1], mods, gate=2, tm=tm, **rows)
        x = _ffn(x, norm_g[i, 2], norm_g[i, 3], mods, ffn_w_up[i].astype(BF16), ffn_w_down[i].astype(BF16),
                 tm=tm, tf=math.gcd(ffn_w_up.shape[2], 512), **rows)

    y_prompt = x[:rows_ctx].reshape(bp, tp, d)
    y_sample = x[rows_ctx:].reshape(bs_, ts, d)
    return (y_prompt, y_sample, jnp.stack(new_c, 1), jnp.stack(new_n, 1), jnp.stack(new_m, 1),
            jnp.stack(new_h, 1))
```

```python
import functools
import math

import numpy as np
import jax
import jax.numpy as jnp
from jax import lax
from jax.experimental import pallas as pl
from jax.experimental.pallas import tpu as pltpu

F32 = jnp.float32
BF16 = jnp.bfloat16
EPS = 1e-6
FNET_GROUPS = 8
LRU_C = 8.0
LANES = 128
MLSTM_L = 256
SUBLANES = 8
VMEM_LIMIT = 56 * 1024 * 1024
N_COND = 16


def _cparams(*sem):
    return pltpu.CompilerParams(dimension_semantics=sem, vmem_limit_bytes=VMEM_LIMIT)


def _sigmoid(x):
    return 1.0 / (1.0 + jnp.exp(-x))


def _log_sigmoid(x):
    return jnp.minimum(x, 0.0) - jnp.log1p(jnp.exp(-jnp.abs(x)))


def _gelu_tanh(x):
    return 0.5 * x * (1.0 + jnp.tanh(math.sqrt(2.0 / math.pi) * (x + 0.044715 * (x * x * x))))


def _rms(x):
    return x * lax.rsqrt(jnp.mean(x * x, axis=-1, keepdims=True) + EPS)


def _norm_mod(x, g, shift, scale):
    return (_rms(x) * g) * (1.0 + scale) + shift


def _cond_of_block(i, tm, rows_ctx, rows_per_sample):
    r = i * tm
    return jnp.where(r < rows_ctx, 0, 1 + (r - rows_ctx) // rows_per_sample)


def _mod_kernel(c_ref, w_ref, b_ref, o_ref):
    c = c_ref[...]
    a = (c * _sigmoid(c)).astype(BF16)
    o_ref[...] = jnp.dot(a, w_ref[...].astype(BF16), preferred_element_type=F32) + b_ref[...]


def _modulation(cond, mod_w, mod_b):
    depth, d, n = mod_w.shape
    tn = min(n, 1024)
    return pl.pallas_call(
        _mod_kernel,
        grid=(depth, n // tn),
        in_specs=[pl.BlockSpec((N_COND, d), lambda l, j: (0, 0)),
                  pl.BlockSpec((None, d, tn), lambda l, j: (l, 0, j)),
                  pl.BlockSpec((None, 1, tn), lambda l, j: (l, 0, j))],
        out_specs=pl.BlockSpec((None, N_COND, tn), lambda l, j: (l, 0, j)),
        out_shape=jax.ShapeDtypeStruct((depth, N_COND, n), F32),
        compiler_params=_cparams("parallel", "parallel"),
        name="modulation",
    )(cond, mod_w, mod_b.reshape(depth, 1, n))


def _pm_kernel(x_ref, g_ref, mod_ref, w_ref, o_ref, *, sh, sc):
    h = _norm_mod(x_ref[...], g_ref[...], mod_ref[sh:sh + 1, :], mod_ref[sc:sc + 1, :]).astype(BF16)
    o_ref[...] = jnp.dot(h, w_ref[...], preferred_element_type=F32).astype(o_ref.dtype)


def _norm_mod_matmul(x, g, mods, w, *, sh, sc, out_dtype, tm, rows_ctx, rows_per_sample):
    m, d = x.shape
    n = w.shape[1]
    cond = functools.partial(_cond_of_block, tm=tm, rows_ctx=rows_ctx, rows_per_sample=rows_per_sample)
    return pl.pallas_call(
        functools.partial(_pm_kernel, sh=sh, sc=sc),
        grid=(m // tm,),
        in_specs=[pl.BlockSpec((tm, d), lambda i: (i, 0)),
                  pl.BlockSpec((1, d), lambda i: (0, 0)),
                  pl.BlockSpec((None, 6, d), lambda i: (cond(i), 0, 0)),
                  pl.BlockSpec((d, n), lambda i: (0, 0), pipeline_mode=pl.Buffered(1))],
        out_specs=pl.BlockSpec((tm, n), lambda i: (i, 0)),
        out_shape=jax.ShapeDtypeStruct((m, n), out_dtype),
        compiler_params=_cparams("parallel"),
        name="norm_mod_matmul",
    )(x, g.reshape(1, d), mods, w)


def _nm_kernel(x_ref, g_ref, mod_ref, o_ref, *, sh, sc):
    o_ref[...] = _norm_mod(x_ref[...], g_ref[...], mod_ref[sh:sh + 1, :], mod_ref[sc:sc + 1, :]).astype(o_ref.dtype)


def _norm_mod_only(x, g, mods, *, sh, sc, tm, rows_ctx, rows_per_sample):
    m, d = x.shape
    cond = functools.partial(_cond_of_block, tm=tm, rows_ctx=rows_ctx, rows_per_sample=rows_per_sample)
    return pl.pallas_call(
        functools.partial(_nm_kernel, sh=sh, sc=sc),
        grid=(m // tm,),
        in_specs=[pl.BlockSpec((tm, d), lambda i: (i, 0)),
                  pl.BlockSpec((1, d), lambda i: (0, 0)),
                  pl.BlockSpec((None, 6, d), lambda i: (cond(i), 0, 0))],
        out_specs=pl.BlockSpec((tm, d), lambda i: (i, 0)),
        out_shape=jax.ShapeDtypeStruct((m, d), BF16),
        compiler_params=_cparams("parallel"),
        name="norm_mod",
    )(x, g.reshape(1, d), mods)


def _me_kernel(y_ref, w_ref, b_ref, x_ref, g_ref, mod_ref, o_ref, *, gate):
    out = jnp.dot(y_ref[...], w_ref[...], preferred_element_type=F32) + b_ref[...]
    o_ref[...] = x_ref[...] + mod_ref[gate:gate + 1, :] * (_rms(out) * g_ref[...])


def _matmul_residual(y, w, b, x, g, mods, *, gate, tm, rows_ctx, rows_per_sample):
    m, k = y.shape
    d = w.shape[1]
    cond = functools.partial(_cond_of_block, tm=tm, rows_ctx=rows_ctx, rows_per_sample=rows_per_sample)
    return pl.pallas_call(
        functools.partial(_me_kernel, gate=gate),
        grid=(m // tm,),
        in_specs=[pl.BlockSpec((tm, k), lambda i: (i, 0)),
                  pl.BlockSpec((k, d), lambda i: (0, 0)),
                  pl.BlockSpec((1, d), lambda i: (0, 0)),
                  pl.BlockSpec((tm, d), lambda i: (i, 0)),
                  pl.BlockSpec((1, d), lambda i: (0, 0)),
                  pl.BlockSpec((None, 6, d), lambda i: (cond(i), 0, 0))],
        out_specs=pl.BlockSpec((tm, d), lambda i: (i, 0)),
        out_shape=jax.ShapeDtypeStruct((m, d), F32),
        compiler_params=_cparams("parallel"),
        name="matmul_residual",
    )(y, w, b.reshape(1, d), x, g.reshape(1, d), mods)


def _ffn_kernel(x_ref, g1_ref, g2_ref, mod_ref, wu_ref, wd_ref, o_ref, h_ref):
    j = pl.program_id(1)

    @pl.when(j == 0)
    def _():
        h_ref[...] = _norm_mod(x_ref[...], g1_ref[...], mod_ref[3:4, :], mod_ref[4:5, :]).astype(BF16)
        o_ref[...] = jnp.zeros_like(o_ref)

    u = jnp.maximum(jnp.dot(h_ref[...], wu_ref[...], preferred_element_type=F32), 0.0)
    o_ref[...] += jnp.dot((u * u).astype(BF16), wd_ref[...], preferred_element_type=F32)

    @pl.when(j == pl.num_programs(1) - 1)
    def _():
        o_ref[...] = x_ref[...] + mod_ref[5:6, :] * (_rms(o_ref[...]) * g2_ref[...])


def _ffn(x, g_pre, g_post, mods, w_up, w_down, *, tm, tf, rows_ctx, rows_per_sample):
    m, d = x.shape
    f = w_up.shape[1]
    cond = functools.partial(_cond_of_block, tm=tm, rows_ctx=rows_ctx, rows_per_sample=rows_per_sample)
    return pl.pallas_call(
        _ffn_kernel,
        grid=(m // tm, f // tf),
        in_specs=[pl.BlockSpec((tm, d), lambda i, j: (i, 0), pipeline_mode=pl.Buffered(1)),
                  pl.BlockSpec((1, d), lambda i, j: (0, 0)),
                  pl.BlockSpec((1, d), lambda i, j: (0, 0)),
                  pl.BlockSpec((None, 6, d), lambda i, j: (cond(i), 0, 0)),
                  pl.BlockSpec((d, tf), lambda i, j: (0, j)),
                  pl.BlockSpec((tf, d), lambda i, j: (j, 0))],
        out_specs=pl.BlockSpec((tm, d), lambda i, j: (i, 0)),
        out_shape=jax.ShapeDtypeStruct((m, d), F32),
        scratch_shapes=[pltpu.VMEM((tm, d), BF16)],
        compiler_params=_cparams("parallel", "arbitrary"),
        name="ffn",
    )(x, g_pre.reshape(1, d), g_post.reshape(1, d), mods, w_up, w_down)


def _mlstm_kernel(*refs, seq, chunk, zero_init, want_state, has_prev):
    it = iter(refs)
    q_ref, k_ref, v_ref, o_ref, gt_ref, gb_ref, ng_ref = (next(it) for _ in range(7))
    if not zero_init:
        c0_ref, n0_ref, m0_ref = next(it), next(it), next(it)
    if has_prev:
        next(it)
    y_ref = next(it)
    if want_state:
        co_ref, no_ref, mo_ref = next(it), next(it), next(it)
    hf_ref, hb_ref, c_ref, n_ref = (next(it) for _ in range(4))
    L = chunk
    nc = seq // L

    if zero_init:
        c_ref[...] = jnp.zeros_like(c_ref)
        n_ref[...] = jnp.zeros_like(n_ref)
        m_init = (jnp.zeros((1, 1), F32), jnp.zeros((1, 1), F32))
    else:
        c_ref[...] = c0_ref[...]
        n_ref[...] = n0_ref[...]
        m_init = (m0_ref[0:1, 0:1], m0_ref[1:2, 0:1])

    row = lax.broadcasted_iota(jnp.int32, (L, L), 0)
    col = lax.broadcasted_iota(jnp.int32, (L, L), 1)
    masks = (col <= row, col >= row)
    lane = lax.broadcasted_iota(jnp.int32, (L, LANES), 1)
    is_forget = (lane % 2) == 1

    def direction(d, c, m_prev):
        r0 = pl.multiple_of(c * L, L)
        rows = pl.ds(r0, L)
        q = q_ref[rows, :]
        k = k_ref[rows, :]
        v = v_ref[rows, :]
        gts = gt_ref[rows, :] + gb_ref[...]
        x = jnp.where(is_forget, _log_sigmoid(gts), gts)
        y = jnp.dot(masks[d].astype(F32), x, precision=lax.Precision.HIGHEST, preferred_element_type=F32)
        xt = x.T
        yt = y.T
        li_c = x[:, 2 * d:2 * d + 1]
        b_c = y[:, 2 * d + 1:2 * d + 2]
        a_c = li_c - b_c
        a_r = xt[2 * d:2 * d + 1, :] - yt[2 * d + 1:2 * d + 2, :]
        dm = jnp.where(masks[d], b_c + a_r, -jnp.inf)
        inter = b_c + m_prev
        m_t = jnp.maximum(inter, jnp.max(dm, axis=1, keepdims=True))
        w_inter = jnp.exp(inter - m_t)
        s = lax.dot_general(q, k, (((1,), (1,)), ((), ())), preferred_element_type=F32) * jnp.exp(dm - m_t)
        c_old = c_ref[d]
        n_old = n_ref[d]
        num = (jnp.dot(s.astype(BF16), v, preferred_element_type=F32)
               + w_inter * jnp.dot(q, c_old.astype(BF16), preferred_element_type=F32))
        qn = jnp.sum(q.astype(F32) * n_old, axis=1, keepdims=True)
        den = jnp.sum(s, axis=1, keepdims=True) + w_inter * qn
        h = num / jnp.maximum(jnp.abs(den), jnp.exp(-m_t))
        b_last = b_c[L - 1:L, :] if d == 0 else b_c[0:1, :]
        g_c = b_last + a_c
        m_new = jnp.maximum(b_last + m_prev, jnp.max(g_c, axis=0, keepdims=True))
        w_c = jnp.exp(b_last + m_prev - m_new)
        kw = k.astype(F32) * jnp.exp(g_c - m_new)
        c_ref[d] = w_c * c_old + lax.dot_general(kw.astype(BF16), v, (((0,), (0,)), ((), ())),
                                                 preferred_element_type=F32)
        n_ref[d] = w_c * n_old + jnp.sum(kw, axis=0, keepdims=True)
        return rows, h, m_new

    def body(i, carry):
        m_f, m_b = carry
        rows, h, m_f = direction(0, i, m_f)
        hf_ref[rows, :] = h
        rows, h, m_b = direction(1, nc - 1 - i, m_b)
        hb_ref[rows, :] = h
        return m_f, m_b

    m_f, m_b = lax.fori_loop(0, nc, body, m_init)

    def finish(c, _):
        rows = pl.ds(pl.multiple_of(c * L, L), L)
        hs = hf_ref[rows, :] + hb_ref[rows, :]
        y_ref[rows, :] = (_rms(hs) * ng_ref[...] * _sigmoid(o_ref[rows, :])).astype(y_ref.dtype)
        return 0

    lax.fori_loop(0, nc, finish, 0)

    if want_state:
        co_ref[...] = c_ref[...]
        no_ref[...] = n_ref[...]
        mo_ref[0:1, :] = jnp.broadcast_to(m_f, (1, LANES))
        mo_ref[1:2, :] = jnp.broadcast_to(m_b, (1, LANES))


def _mlstm_core(qkv, og, gate_b, norm_g, state, y_prev, *, row0, batch, seq, heads, dk, dv, want_state):
    L = math.gcd(MLSTM_L, seq)
    assert row0 % seq == 0
    b0 = row0 // seq
    hk = heads * dk
    zero_init = state is None
    in_specs = [pl.BlockSpec((seq, dk), lambda b, h: (b0 + b, h)),
                pl.BlockSpec((seq, dk), lambda b, h: (b0 + b, heads + h)),
                pl.BlockSpec((seq, dv), lambda b, h: (b0 + b, 2 * hk // dv + h)),
                pl.BlockSpec((seq, dv), lambda b, h: (b0 + b, h)),
                pl.BlockSpec((seq, LANES), lambda b, h: (b0 + b, heads * dv // LANES + h)),
                pl.BlockSpec((1, LANES), lambda b, h: (0, h)),
                pl.BlockSpec((1, dv), lambda b, h: (0, h))]
    args = [qkv, qkv, qkv, og, og, gate_b, norm_g.reshape(1, heads * dv)]
    if not zero_init:
        c0, n0, m0 = state
        in_specs += [pl.BlockSpec((None, 2, None, dk, dv), lambda b, h: (b, 0, h, 0, 0)),
                     pl.BlockSpec((None, 2, None, 1, dk), lambda b, h: (b, 0, h, 0, 0)),
                     pl.BlockSpec((None, None, 2, LANES), lambda b, h: (b, h, 0, 0))]
        args += [c0, n0, m0]
    aliases = {}
    if y_prev is not None:
        in_specs.append(pl.BlockSpec(memory_space=pl.ANY))
        args.append(y_prev)
        aliases = {len(args) - 1: 0}
    out_specs = [pl.BlockSpec((seq, dv), lambda b, h: (b0 + b, h))]
    out_shape = [jax.ShapeDtypeStruct((qkv.shape[0], heads * dv), BF16)]
    if want_state:
        out_specs += [pl.BlockSpec((None, 2, None, dk, dv), lambda b, h: (b, 0, h, 0, 0)),
                      pl.BlockSpec((None, 2, None, 1, dk), lambda b, h: (b, 0, h, 0, 0)),
                      pl.BlockSpec((None, None, 2, LANES), lambda b, h: (b, h, 0, 0))]
        out_shape += [jax.ShapeDtypeStruct((batch, 2, heads, dk, dv), F32),
                      jax.ShapeDtypeStruct((batch, 2, heads, 1, dk), F32),
                      jax.ShapeDtypeStruct((batch, heads, 2, LANES), F32)]
    return pl.pallas_call(
        functools.partial(_mlstm_kernel, seq=seq, chunk=L, zero_init=zero_init, want_state=want_state,
                          has_prev=y_prev is not None),
        grid=(batch, heads),
        in_specs=in_specs,
        out_specs=out_specs,
        out_shape=out_shape,
        input_output_aliases=aliases,
        scratch_shapes=[pltpu.VMEM((seq, dv), F32), pltpu.VMEM((seq, dv), F32),
                        pltpu.VMEM((2, dk, dv), F32), pltpu.VMEM((2, 1, dk), F32)],
        compiler_params=_cparams("parallel", "parallel"),
        name="mlstm_core",
    )(*args)


def _dft_mats(seq, gc):
    def cs(n):
        idx = np.arange(n)
        ang = 2.0 * np.pi * ((idx[:, None] * idx[None, :]) % n) / n
        return np.cos(ang) / math.sqrt(n), np.sin(ang) / math.sqrt(n)
    ct, st = cs(seq)
    cc, sc = cs(gc)
    w_time = np.concatenate([ct, -st], axis=1)
    w_chan = np.concatenate([cc, sc], axis=1)
    return jnp.asarray(w_time, BF16), jnp.asarray(w_chan, BF16)


def _fnet_kernel(h_ref, wt_ref, wc_ref, *rest, gc):
    o_ref = rest[-1]
    y = jnp.dot(h_ref[...], wc_ref[...], preferred_element_type=F32)
    ycat = jnp.concatenate([y[:, :gc], y[:, gc:]], axis=0).astype(BF16)
    o_ref[...] = jnp.dot(wt_ref[...], ycat, preferred_element_type=F32).astype(o_ref.dtype)


def _fnet_mix(h, y_prev, *, row0, batch, seq):
    d = h.shape[1]
    gc = d // FNET_GROUPS
    b0 = row0 // seq
    w_time, w_chan = _dft_mats(seq, gc)
    in_specs = [pl.BlockSpec((seq, gc), lambda b, g: (b0 + b, g)),
                pl.BlockSpec((seq, 2 * seq), lambda b, g: (0, 0)),
                pl.BlockSpec((gc, 2 * gc), lambda b, g: (0, 0))]
    args = [h, w_time, w_chan]
    aliases = {}
    if y_prev is not None:
        in_specs.append(pl.BlockSpec(memory_space=pl.ANY))
        args.append(y_prev)
        aliases = {len(args) - 1: 0}
    return pl.pallas_call(
        functools.partial(_fnet_kernel, gc=gc),
        grid=(batch, FNET_GROUPS),
        in_specs=in_specs,
        out_specs=pl.BlockSpec((seq, gc), lambda b, g: (b0 + b, g)),
        out_shape=jax.ShapeDtypeStruct(h.shape, BF16),
        input_output_aliases=aliases,
        compiler_params=_cparams("parallel", "parallel"),
        name="fnet_mix",
    )(*args)


def _lru_kernel(*refs, seq, zero_init, want_state, has_prev):
    it = iter(refs)
    br_ref, xb_ref, cw_ref, cb_ref, gw_ref, gbias_ref, lam_ref = (next(it) for _ in range(7))
    if not zero_init:
        h0_ref = next(it)
    if has_prev:
        next(it)
    y_ref = next(it)
    if want_state:
        hl_ref = next(it)
    a_ref, u_ref, hs_ref = (next(it) for _ in range(3))
    T = seq
    S = SUBLANES
    x = xb_ref[...]
    tt = lax.broadcasted_iota(jnp.int32, x.shape, 0)
    xm1 = jnp.where(tt >= 1, pltpu.roll(x, 1, 0), 0.0)
    xp1 = jnp.where(tt < T - 1, pltpu.roll(x, T - 1, 0), 0.0)
    xp2 = jnp.where(tt < T - 2, pltpu.roll(x, T - 2, 0), 0.0)
    xc = (cw_ref[0:1, :] * xm1 + cw_ref[1:2, :] * x + cw_ref[2:3, :] * xp1 + cw_ref[3:4, :] * xp2
          + cb_ref[...])
    xcb = xc.astype(BF16)
    r8 = tt % S
    for d in range(2):
        pre_r = jnp.dot(xcb, gw_ref[d, 0], preferred_element_type=F32) + gbias_ref[2 * d:2 * d + 1, :]
        pre_i = jnp.dot(xcb, gw_ref[d, 1], preferred_element_type=F32) + gbias_ref[2 * d + 1:2 * d + 2, :]
        log_a = LRU_C * _sigmoid(pre_r) * _log_sigmoid(lam_ref[d:d + 1, :])
        a = jnp.exp(log_a)
        u = jnp.sqrt(-jnp.tanh(log_a) * (a * a + 1.0)) * (_sigmoid(pre_i) * xc)
        s = 1
        while s < S:
            shift, valid = (s, r8 >= s) if d == 0 else (T - s, r8 < S - s)
            u = jnp.where(valid, a * pltpu.roll(u, shift, 0) + u, u)
            a = jnp.where(valid, a * pltpu.roll(a, shift, 0), a)
            s *= 2
        a_ref[d] = a
        u_ref[d] = u

    if zero_init:
        h_init = (jnp.zeros((1, x.shape[1]), F32), jnp.zeros((1, x.shape[1]), F32))
    else:
        h_init = (h0_ref[0:1, :], h0_ref[1:2, :])
    groups = T // S

    def group(i, carry):
        c_f, c_b = carry
        rf = pl.ds(pl.multiple_of(i * S, S), S)
        h_f = a_ref[0, rf, :] * c_f + u_ref[0, rf, :]
        hs_ref[0, rf, :] = h_f
        rb = pl.ds(pl.multiple_of((groups - 1 - i) * S, S), S)
        h_b = a_ref[1, rb, :] * c_b + u_ref[1, rb, :]
        hs_ref[1, rb, :] = h_b
        return h_f[S - 1:S, :], h_b[0:1, :]

    h_f, h_b = lax.fori_loop(0, groups, group, h_init, unroll=4)
    y_ref[...] = ((hs_ref[0] + hs_ref[1]) * _gelu_tanh(br_ref[...])).astype(y_ref.dtype)
    if want_state:
        hl_ref[0:1, :] = h_f
        hl_ref[1:2, :] = h_b


def _lru_core(gx, conv_w, conv_b, gate_w, gate_b, lam, h0, y_prev, *, row0, batch, seq, want_state):
    width = conv_w.shape[1]
    nb = gate_w.shape[2]
    bs = width // nb
    b0 = row0 // seq
    zero_init = h0 is None
    in_specs = [pl.BlockSpec((seq, bs), lambda b, n: (b0 + b, n)),
                pl.BlockSpec((seq, bs), lambda b, n: (b0 + b, nb + n)),
                pl.BlockSpec((conv_w.shape[0], bs), lambda b, n: (0, n)),
                pl.BlockSpec((1, bs), lambda b, n: (0, n)),
                pl.BlockSpec((2, 2, None, bs, bs), lambda b, n: (0, 0, n, 0, 0)),
                pl.BlockSpec((4, bs), lambda b, n: (0, n)),
                pl.BlockSpec((2, bs), lambda b, n: (0, n))]
    args = [gx, gx, conv_w, conv_b.reshape(1, width), gate_w, gate_b.reshape(4, width), lam]
    if not zero_init:
        in_specs.append(pl.BlockSpec((None, 2, bs), lambda b, n: (b, 0, n)))
        args.append(h0)
    aliases = {}
    if y_prev is not None:
        in_specs.append(pl.BlockSpec(memory_space=pl.ANY))
        args.append(y_prev)
        aliases = {len(args) - 1: 0}
    out_specs = [pl.BlockSpec((seq, bs), lambda b, n: (b0 + b, n))]
    out_shape = [jax.ShapeDtypeStruct((gx.shape[0], width), BF16)]
    if want_state:
        out_specs.append(pl.BlockSpec((None, 2, bs), lambda b, n: (b, 0, n)))
        out_shape.append(jax.ShapeDtypeStruct((batch, 2, width), F32))
    return pl.pallas_call(
        functools.partial(_lru_kernel, seq=seq, zero_init=zero_init, want_state=want_state,
                          has_prev=y_prev is not None),
        grid=(batch, nb),
        in_specs=in_specs,
        out_specs=out_specs,
        out_shape=out_shape,
        input_output_aliases=aliases,
        scratch_shapes=[pltpu.VMEM((2, seq, bs), F32)] * 3,
        compiler_params=_cparams("parallel", "parallel"),
        name="lru_core",
    )(*args)


def _mlstm_weights(w_in, b_gate, heads, dk, dv):
    d = w_in.shape[0]
    hk, hv = heads * dk, heads * dv
    wq, wk, wv, wo, wg = jnp.split(w_in, [hk, 2 * hk, 2 * hk + hv, 2 * hk + 2 * hv], axis=1)
    w_qkv = jnp.concatenate([wq, wk * (dk ** -0.5), wv], axis=1).astype(BF16)
    wg = wg.reshape(d, 2, 2, heads).transpose(0, 3, 1, 2).reshape(d, heads, 4)
    wg = jnp.pad(wg, ((0, 0), (0, 0), (0, LANES - 4))).reshape(d, heads * LANES)
    w_og = jnp.concatenate([wo, wg], axis=1).astype(BF16)
    bg = b_gate.astype(F32).transpose(2, 0, 1).reshape(heads, 4)
    bg = jnp.pad(bg, ((0, 0), (0, LANES - 4))).reshape(1, heads * LANES)
    return w_qkv, w_og, bg


def kernel(x_prompt, x_sample, c, state_mlstm_C, state_mlstm_n, state_mlstm_m, state_lru_h, c_ctx,
           mod_w, mod_b, norm_g, ffn_w_up, ffn_w_down, mlstm_w_in, mlstm_b_gate, mlstm_norm_g, mlstm_w_out,
           fnet_w_out, fnet_b_out, lru_w_in, lru_conv_w, lru_conv_b, lru_gate_w, lru_gate_b, lru_lambda,
           lru_w_out):
    bp, tp, d = x_prompt.shape
    bs_, ts, _ = x_sample.shape
    depth = mod_w.shape[0]
    heads = mlstm_b_gate.shape[-1]
    dk, dv = state_mlstm_C.shape[-2:]
    rows_ctx = bp * tp
    rows = dict(rows_ctx=rows_ctx, rows_per_sample=ts)
    assert 1 + bs_ <= N_COND
    tm = math.gcd(512, math.gcd(rows_ctx, ts))
    tm_ffn = math.gcd(1024, math.gcd(rows_ctx, ts))

    x = jnp.concatenate([x_prompt.reshape(rows_ctx, d), x_sample.reshape(bs_ * ts, d)], axis=0)
    cond = jnp.zeros((N_COND, d), F32).at[0].set(c_ctx).at[1:1 + bs_].set(c)
    mods_all = _modulation(cond, mod_w, mod_b).reshape(depth, N_COND, 6, d)
    zero_bias = jnp.zeros((d,), F32)
    trunks = ((0, bp, tp, True), (rows_ctx, bs_, ts, False))

    new_c, new_n, new_m, new_h = [], [], [], []
    for i in range(depth):
        mods = mods_all[i]
        kind, j = i % 3, i // 3
        if kind == 0:
            w_qkv, w_og, bg = _mlstm_weights(mlstm_w_in[j], mlstm_b_gate[j], heads, dk, dv)
            pm = functools.partial(_norm_mod_matmul, x, norm_g[i, 0], mods, sh=0, sc=1, tm=tm, **rows)
            qkv = pm(w_qkv, out_dtype=BF16)
            og = pm(w_og, out_dtype=F32)
            y = None
            for row0, nb, seq, is_ctx in trunks:
                if is_ctx:
                    state = None
                else:
                    m0 = jnp.broadcast_to(state_mlstm_m[:, j].transpose(0, 2, 1)[..., None],
                                          (nb, heads, 2, LANES))
                    state = (state_mlstm_C[:, j], state_mlstm_n[:, j].reshape(nb, 2, heads, 1, dk), m0)
                res = _mlstm_core(qkv, og, bg, mlstm_norm_g[j], state, y, row0=row0, batch=nb, seq=seq,
                                  heads=heads, dk=dk, dv=dv, want_state=is_ctx)
                y = res[0]
                if is_ctx:
                    new_c.append(res[1])
                    new_n.append(res[2].reshape(nb, 2, heads, dk))
                    new_m.append(res[3][..., 0].transpose(0, 2, 1))
            w_out, b_out = mlstm_w_out[j].astype(BF16), zero_bias
        elif kind == 1:
            h = _norm_mod_only(x, norm_g[i, 0], mods, sh=0, sc=1, tm=tm, **rows)
            y = None
            for row0, nb, seq, _ in trunks:
                y = _fnet_mix(h, y, row0=row0, batch=nb, seq=seq)
            w_out, b_out = fnet_w_out[j].astype(BF16), fnet_b_out[j]
        else:
            w_in = lru_w_in[j].astype(BF16)
            gx = _norm_mod_matmul(x, norm_g[i, 0], mods, w_in, sh=0, sc=1, out_dtype=F32, tm=tm, **rows)
            y = None
            for row0, nb, seq, is_ctx in trunks:
                res = _lru_core(gx, lru_conv_w[j], lru_conv_b[j], lru_gate_w[j].astype(BF16), lru_gate_b[j],
                                lru_lambda[j], None if is_ctx else state_lru_h[:, j], y,
                                row0=row0, batch=nb, seq=seq, want_state=is_ctx)
                y = res[0]
                if is_ctx:
                    new_h.append(res[1])
            w_out, b_out = lru_w_out[j].astype(BF16), zero_bias
        x = _matmul_residual(y, w_out, b_out, x, norm_g[i, 1], mods, gate=2, tm=tm, **rows)
        x = _ffn(x, norm_g[i, 2], norm_g[i, 3], mods, ffn_w_up[i].astype(BF16), ffn_w_down[i].astype(BF16),
                 tm=tm_ffn, tf=math.gcd(ffn_w_up.shape[2], 512), **rows)

    y_prompt = x[:rows_ctx].reshape(bp, tp, d)
    y_sample = x[rows_ctx:].reshape(bs_, ts, d)
    return (y_prompt, y_sample, jnp.stack(new_c, 1), jnp.stack(new_n, 1), jnp.stack(new_m, 1),
            jnp.stack(new_h, 1))
```

```python
import functools
import math

import numpy as np
import jax
import jax.numpy as jnp
from jax import lax
from jax.experimental import pallas as pl
from jax.experimental.pallas import tpu as pltpu

F32 = jnp.float32
BF16 = jnp.bfloat16
EPS = 1e-6
FNET_GROUPS = 8
LRU_C = 8.0
LANES = 128
MLSTM_L = 256
SUBLANES = 8
VMEM_LIMIT = 56 * 1024 * 1024
N_COND = 16


def _cparams(*sem):
    return pltpu.CompilerParams(dimension_semantics=sem, vmem_limit_bytes=VMEM_LIMIT)


def _sigmoid(x):
    return 1.0 / (1.0 + jnp.exp(-x))


def _log_sigmoid(x):
    return jnp.minimum(x, 0.0) - jnp.log1p(jnp.exp(-jnp.abs(x)))


def _gelu_tanh(x):
    return 0.5 * x * (1.0 + jnp.tanh(math.sqrt(2.0 / math.pi) * (x + 0.044715 * (x * x * x))))


def _rms(x):
    return x * lax.rsqrt(jnp.mean(x * x, axis=-1, keepdims=True) + EPS)


def _norm_mod(x, g, shift, scale):
    return (_rms(x) * g) * (1.0 + scale) + shift


def _cond_of_block(i, tm, rows_ctx, rows_per_sample):
    r = i * tm
    return jnp.where(r < rows_ctx, 0, 1 + (r - rows_ctx) // rows_per_sample)


def _mod_kernel(c_ref, w_ref, b_ref, o_ref):
    c = c_ref[...]
    a = (c * _sigmoid(c)).astype(BF16)
    o_ref[...] = jnp.dot(a, w_ref[...].astype(BF16), preferred_element_type=F32) + b_ref[...]


def _modulation(cond, mod_w, mod_b):
    depth, d, n = mod_w.shape
    tn = min(n, 1024)
    return pl.pallas_call(
        _mod_kernel,
        grid=(depth, n // tn),
        in_specs=[pl.BlockSpec((N_COND, d), lambda l, j: (0, 0)),
                  pl.BlockSpec((None, d, tn), lambda l, j: (l, 0, j)),
                  pl.BlockSpec((None, 1, tn), lambda l, j: (l, 0, j))],
        out_specs=pl.BlockSpec((None, N_COND, tn), lambda l, j: (l, 0, j)),
        out_shape=jax.ShapeDtypeStruct((depth, N_COND, n), F32),
        compiler_params=_cparams("parallel", "parallel"),
        name="modulation",
    )(cond, mod_w, mod_b.reshape(depth, 1, n))


def _pm_kernel(x_ref, g_ref, mod_ref, w_ref, o_ref, *, sh, sc):
    h = _norm_mod(x_ref[...], g_ref[...], mod_ref[sh:sh + 1, :], mod_ref[sc:sc + 1, :]).astype(BF16)
    o_ref[...] = jnp.dot(h, w_ref[...], preferred_element_type=F32).astype(o_ref.dtype)


def _norm_mod_matmul(x, g, mods, w, *, sh, sc, out_dtype, tm, rows_ctx, rows_per_sample):
    m, d = x.shape
    n = w.shape[1]
    cond = functools.partial(_cond_of_block, tm=tm, rows_ctx=rows_ctx, rows_per_sample=rows_per_sample)
    return pl.pallas_call(
        functools.partial(_pm_kernel, sh=sh, sc=sc),
        grid=(m // tm,),
        in_specs=[pl.BlockSpec((tm, d), lambda i: (i, 0)),
                  pl.BlockSpec((1, d), lambda i: (0, 0)),
                  pl.BlockSpec((None, 6, d), lambda i: (cond(i), 0, 0)),
                  pl.BlockSpec((d, n), lambda i: (0, 0), pipeline_mode=pl.Buffered(1))],
        out_specs=pl.BlockSpec((tm, n), lambda i: (i, 0)),
        out_shape=jax.ShapeDtypeStruct((m, n), out_dtype),
        compiler_params=_cparams("parallel"),
        name="norm_mod_matmul",
    )(x, g.reshape(1, d), mods, w)


def _nm_kernel(x_ref, g_ref, mod_ref, o_ref, *, sh, sc):
    o_ref[...] = _norm_mod(x_ref[...], g_ref[...], mod_ref[sh:sh + 1, :], mod_ref[sc:sc + 1, :]).astype(o_ref.dtype)


def _norm_mod_only(x, g, mods, *, sh, sc, tm, rows_ctx, rows_per_sample):
    m, d = x.shape
    cond = functools.partial(_cond_of_block, tm=tm, rows_ctx=rows_ctx, rows_per_sample=rows_per_sample)
    return pl.pallas_call(
        functools.partial(_nm_kernel, sh=sh, sc=sc),
        grid=(m // tm,),
        in_specs=[pl.BlockSpec((tm, d), lambda i: (i, 0)),
                  pl.BlockSpec((1, d), lambda i: (0, 0)),
                  pl.BlockSpec((None, 6, d), lambda i: (cond(i), 0, 0))],
        out_specs=pl.BlockSpec((tm, d), lambda i: (i, 0)),
        out_shape=jax.ShapeDtypeStruct((m, d), BF16),
        compiler_params=_cparams("parallel"),
        name="norm_mod",
    )(x, g.reshape(1, d), mods)


def _me_kernel(y_ref, w_ref, b_ref, x_ref, g_ref, mod_ref, o_ref, *, gate):
    out = jnp.dot(y_ref[...], w_ref[...], preferred_element_type=F32) + b_ref[...]
    o_ref[...] = x_ref[...] + mod_ref[gate:gate + 1, :] * (_rms(out) * g_ref[...])


def _matmul_residual(y, w, b, x, g, mods, *, gate, tm, rows_ctx, rows_per_sample):
    m, k = y.shape
    d = w.shape[1]
    cond = functools.partial(_cond_of_block, tm=tm, rows_ctx=rows_ctx, rows_per_sample=rows_per_sample)
    return pl.pallas_call(
        functools.partial(_me_kernel, gate=gate),
        grid=(m // tm,),
        in_specs=[pl.BlockSpec((tm, k), lambda i: (i, 0)),
                  pl.BlockSpec((k, d), lambda i: (0, 0)),
                  pl.BlockSpec((1, d), lambda i: (0, 0)),
                  pl.BlockSpec((tm, d), lambda i: (i, 0)),
                  pl.BlockSpec((1, d), lambda i: (0, 0)),
                  pl.BlockSpec((None, 6, d), lambda i: (cond(i), 0, 0))],
        out_specs=pl.BlockSpec((tm, d), lambda i: (i, 0)),
        out_shape=jax.ShapeDtypeStruct((m, d), F32),
        compiler_params=_cparams("parallel"),
        name="matmul_residual",
    )(y, w, b.reshape(1, d), x, g.reshape(1, d), mods)


def _ffn_kernel(x_ref, g1_ref, g2_ref, mod_ref, wu_ref, wd_ref, o_ref, h_ref):
    j = pl.program_id(1)

    @pl.when(j == 0)
    def _():
        h_ref[...] = _norm_mod(x_ref[...], g1_ref[...], mod_ref[3:4, :], mod_ref[4:5, :]).astype(BF16)
        o_ref[...] = jnp.zeros_like(o_ref)

    u = jnp.maximum(jnp.dot(h_ref[...], wu_ref[...], preferred_element_type=F32), 0.0)
    o_ref[...] += jnp.dot((u * u).astype(BF16), wd_ref[...], preferred_element_type=F32)

    @pl.when(j == pl.num_programs(1) - 1)
    def _():
        o_ref[...] = x_ref[...] + mod_ref[5:6, :] * (_rms(o_ref[...]) * g2_ref[...])


def _ffn(x, g_pre, g_post, mods, w_up, w_down, *, row0, nrows, tm, rows_ctx, rows_per_sample):
    d = x.shape[1]
    nf, _, tf = w_up.shape
    i0 = row0 // tm
    cond = functools.partial(_cond_of_block, tm=tm, rows_ctx=rows_ctx, rows_per_sample=rows_per_sample)
    return pl.pallas_call(
        _ffn_kernel,
        grid=(nrows // tm, nf),
        in_specs=[pl.BlockSpec((tm, d), lambda i, j: (i0 + i, 0)),
                  pl.BlockSpec((1, d), lambda i, j: (0, 0)),
                  pl.BlockSpec((1, d), lambda i, j: (0, 0)),
                  pl.BlockSpec((None, 6, d), lambda i, j: (cond(i0 + i), 0, 0)),
                  pl.BlockSpec((None, d, tf), lambda i, j: (j, 0, 0)),
                  pl.BlockSpec((tf, d), lambda i, j: (j, 0))],
        out_specs=pl.BlockSpec((tm, d), lambda i, j: (i, 0)),
        out_shape=jax.ShapeDtypeStruct((nrows, d), F32),
        scratch_shapes=[pltpu.VMEM((tm, d), BF16)],
        compiler_params=_cparams("parallel", "arbitrary"),
        name="ffn",
    )(x, g_pre.reshape(1, d), g_post.reshape(1, d), mods, w_up, w_down)


def _mlstm_kernel(*refs, seq, nsub, chunk, zero_init, want_state, has_prev):
    it = iter(refs)
    q_ref, k_ref, v_ref, o_ref, gt_ref, gb_ref, ng_ref = (next(it) for _ in range(7))
    if not zero_init:
        c0_ref, n0_ref, m0_ref = next(it), next(it), next(it)
    if has_prev:
        next(it)
    y_ref = next(it)
    if want_state:
        co_ref, no_ref, mo_ref = next(it), next(it), next(it)
    hf_ref, hb_ref, c_ref, n_ref = (next(it) for _ in range(4))
    L = chunk
    nc = seq // L

    if zero_init:
        c_ref[...] = jnp.zeros_like(c_ref)
        n_ref[...] = jnp.zeros_like(n_ref)
        m_state = [[jnp.zeros((1, 1), F32), jnp.zeros((1, 1), F32)] for _ in range(nsub)]
    else:
        c_ref[...] = c0_ref[...]
        n_ref[...] = n0_ref[...]
        m_state = [[m0_ref[sb, 0:1, 0:1], m0_ref[sb, 1:2, 0:1]] for sb in range(nsub)]

    row = lax.broadcasted_iota(jnp.int32, (L, L), 0)
    col = lax.broadcasted_iota(jnp.int32, (L, L), 1)
    masks = (col <= row, col >= row)
    lane = lax.broadcasted_iota(jnp.int32, (L, LANES), 1)
    is_forget = (lane % 2) == 1

    def rows_of(sb, c):
        return pl.ds(sb * seq + c * L, L)

    scan_order = (tuple(range(nc)), tuple(range(nc - 1, -1, -1)))
    units = [(sb, c, d) for step in range(nc) for sb in range(nsub) for d, c in ((0, step), (1, nc - 1 - step))]

    gate = {}
    for sb, c, d in units:
        gts = gt_ref[rows_of(sb, c), :] + gb_ref[...]
        x = jnp.where(is_forget, _log_sigmoid(gts), gts)
        x_hi = x.astype(BF16)
        r_1 = x - x_hi.astype(F32)
        x_mid = r_1.astype(BF16)
        x_lo = (r_1 - x_mid.astype(F32)).astype(BF16)
        y3 = jnp.dot(masks[d].astype(BF16), jnp.concatenate([x_hi, x_mid, x_lo], axis=1),
                     preferred_element_type=F32)
        y = (y3[:, :LANES] + y3[:, LANES:2 * LANES]) + y3[:, 2 * LANES:]
        xt = x.T
        yt = y.T
        b_c = y[:, 2 * d + 1:2 * d + 2]
        a_c = x[:, 2 * d:2 * d + 1] - b_c
        a_r = xt[2 * d:2 * d + 1, :] - yt[2 * d + 1:2 * d + 2, :]
        dm = jnp.where(masks[d], b_c + a_r, -jnp.inf)
        b_last = b_c[L - 1:L, :] if d == 0 else b_c[0:1, :]
        g_c = b_last + a_c
        gate[sb, c, d] = (dm, jnp.max(dm, axis=1, keepdims=True), b_c, b_last, g_c,
                          jnp.max(g_c, axis=0, keepdims=True))

    m_in, m_out = {}, {}
    for sb in range(nsub):
        for d in range(2):
            m = m_state[sb][d]
            for c in scan_order[d]:
                _, _, _, b_last, _, g_max = gate[sb, c, d]
                m_in[sb, c, d] = m
                m = jnp.maximum(b_last + m, g_max)
                m_out[sb, c, d] = m
            m_state[sb][d] = m

    part = {}
    for sb, c, d in units:
        dm, r_max, b_c, b_last, g_c, _ = gate[sb, c, d]
        rows = rows_of(sb, c)
        q = q_ref[rows, :]
        k = k_ref[rows, :]
        v = v_ref[rows, :]
        inter = b_c + m_in[sb, c, d]
        m_t = jnp.maximum(inter, r_max)
        s = lax.dot_general(q, k, (((1,), (1,)), ((), ())), preferred_element_type=F32) * jnp.exp(dm - m_t)
        kw = k.astype(F32) * jnp.exp(g_c - m_out[sb, c, d])
        part[sb, c, d] = (jnp.dot(s.astype(BF16), v, preferred_element_type=F32),
                          jnp.sum(s, axis=1, keepdims=True), jnp.exp(inter - m_t), jnp.exp(-m_t),
                          jnp.exp(b_last + m_in[sb, c, d] - m_out[sb, c, d]), kw.astype(BF16),
                          jnp.sum(kw, axis=0, keepdims=True))

    for sb, c, d in units:
        sv, s_sum, w_inter, den_floor, w_c, kw, kw_sum = part[sb, c, d]
        rows = rows_of(sb, c)
        q = q_ref[rows, :]
        c_old = c_ref[sb, d]
        n_old = n_ref[sb, d]
        num = sv + w_inter * jnp.dot(q, c_old.astype(BF16), preferred_element_type=F32)
        den = s_sum + w_inter * jnp.sum(q.astype(F32) * n_old, axis=1, keepdims=True)
        (hf_ref, hb_ref)[d][rows, :] = num / jnp.maximum(jnp.abs(den), den_floor)
        c_ref[sb, d] = w_c * c_old + lax.dot_general(kw, v_ref[rows, :], (((0,), (0,)), ((), ())),
                                                     preferred_element_type=F32)
        n_ref[sb, d] = w_c * n_old + kw_sum

    def finish(c, _):
        rows = pl.ds(pl.multiple_of(c * L, L), L)
        hs = hf_ref[rows, :] + hb_ref[rows, :]
        y_ref[rows, :] = (_rms(hs) * ng_ref[...] * _sigmoid(o_ref[rows, :])).astype(y_ref.dtype)
        return 0

    lax.fori_loop(0, nsub * nc, finish, 0)

    if want_state:
        co_ref[...] = c_ref[...]
        no_ref[...] = n_ref[...]
        for sb in range(nsub):
            for d in range(2):
                mo_ref[sb, d:d + 1, :] = jnp.broadcast_to(m_state[sb][d], (1, LANES))


def _mlstm_core(qkv, og, gate_b, norm_g, state, y_prev, *, row0, batch, seq, nsub, heads, dk, dv, want_state):
    L = math.gcd(MLSTM_L, seq)
    rb = nsub * seq
    assert row0 % rb == 0 and batch % nsub == 0
    b0 = row0 // rb
    hk = heads * dk
    zero_init = state is None
    state_specs = [pl.BlockSpec((nsub, 2, None, dk, dv), lambda b, h: (b, 0, h, 0, 0)),
                   pl.BlockSpec((nsub, 2, None, 1, dk), lambda b, h: (b, 0, h, 0, 0)),
                   pl.BlockSpec((nsub, None, 2, LANES), lambda b, h: (b, h, 0, 0))]
    in_specs = [pl.BlockSpec((rb, dk), lambda b, h: (b0 + b, h)),
                pl.BlockSpec((rb, dk), lambda b, h: (b0 + b, heads + h)),
                pl.BlockSpec((rb, dv), lambda b, h: (b0 + b, 2 * hk // dv + h)),
                pl.BlockSpec((rb, dv), lambda b, h: (b0 + b, h)),
                pl.BlockSpec((rb, LANES), lambda b, h: (b0 + b, heads * dv // LANES + h)),
                pl.BlockSpec((1, LANES), lambda b, h: (0, h)),
                pl.BlockSpec((1, dv), lambda b, h: (0, h))]
    args = [qkv, qkv, qkv, og, og, gate_b, norm_g.reshape(1, heads * dv)]
    if not zero_init:
        in_specs += state_specs
        args += list(state)
    aliases = {}
    if y_prev is not None:
        in_specs.append(pl.BlockSpec(memory_space=pl.ANY))
        args.append(y_prev)
        aliases = {len(args) - 1: 0}
    out_specs = [pl.BlockSpec((rb, dv), lambda b, h: (b0 + b, h))]
    out_shape = [jax.ShapeDtypeStruct((qkv.shape[0], heads * dv), BF16)]
    if want_state:
        out_specs += state_specs
        out_shape += [jax.ShapeDtypeStruct((batch, 2, heads, dk, dv), F32),
                      jax.ShapeDtypeStruct((batch, 2, heads, 1, dk), F32),
                      jax.ShapeDtypeStruct((batch, heads, 2, LANES), F32)]
    return pl.pallas_call(
        functools.partial(_mlstm_kernel, seq=seq, nsub=nsub, chunk=L, zero_init=zero_init,
                          want_state=want_state, has_prev=y_prev is not None),
        grid=(batch // nsub, heads),
        in_specs=in_specs,
        out_specs=out_specs,
        out_shape=out_shape,
        input_output_aliases=aliases,
        scratch_shapes=[pltpu.VMEM((rb, dv), F32), pltpu.VMEM((rb, dv), F32),
                        pltpu.VMEM((nsub, 2, dk, dv), F32), pltpu.VMEM((nsub, 2, 1, dk), F32)],
        compiler_params=_cparams("parallel", "parallel"),
        name="mlstm_core",
    )(*args)


def _dft_mats(seq, gc):
    def cs(n):
        idx = np.arange(n)
        ang = 2.0 * np.pi * ((idx[:, None] * idx[None, :]) % n) / n
        return np.cos(ang) / math.sqrt(n), np.sin(ang) / math.sqrt(n)
    ct, st = cs(seq)
    cc, sc = cs(gc)
    w_time = np.concatenate([ct, -st], axis=1)
    w_chan = np.concatenate([cc, sc], axis=1)
    return jnp.asarray(w_time, BF16), jnp.asarray(w_chan, BF16)


def _fnet_kernel(h_ref, wt_ref, wc_ref, *rest, gc):
    o_ref = rest[-1]
    y = jnp.dot(h_ref[...], wc_ref[...], preferred_element_type=F32)
    ycat = jnp.concatenate([y[:, :gc], y[:, gc:]], axis=0).astype(BF16)
    o_ref[...] = jnp.dot(wt_ref[...], ycat, preferred_element_type=F32).astype(o_ref.dtype)


def _fnet_mix(h, y_prev, *, row0, batch, seq):
    d = h.shape[1]
    gc = d // FNET_GROUPS
    b0 = row0 // seq
    w_time, w_chan = _dft_mats(seq, gc)
    in_specs = [pl.BlockSpec((seq, gc), lambda b, g: (b0 + b, g)),
                pl.BlockSpec((seq, 2 * seq), lambda b, g: (0, 0)),
                pl.BlockSpec((gc, 2 * gc), lambda b, g: (0, 0))]
    args = [h, w_time, w_chan]
    aliases = {}
    if y_prev is not None:
        in_specs.append(pl.BlockSpec(memory_space=pl.ANY))
        args.append(y_prev)
        aliases = {len(args) - 1: 0}
    return pl.pallas_call(
        functools.partial(_fnet_kernel, gc=gc),
        grid=(batch, FNET_GROUPS),
        in_specs=in_specs,
        out_specs=pl.BlockSpec((seq, gc), lambda b, g: (b0 + b, g)),
        out_shape=jax.ShapeDtypeStruct(h.shape, BF16),
        input_output_aliases=aliases,
        compiler_params=_cparams("parallel", "parallel"),
        name="fnet_mix",
    )(*args)


def _lru_kernel(*refs, seq, zero_init, want_state, has_prev):
    it = iter(refs)
    br_ref, xb_ref, cw_ref, cb_ref, gw_ref, gbias_ref, lam_ref = (next(it) for _ in range(7))
    if not zero_init:
        h0_ref = next(it)
    if has_prev:
        next(it)
    y_ref = next(it)
    if want_state:
        hl_ref = next(it)
    a_ref, u_ref, hs_ref = (next(it) for _ in range(3))
    T = seq
    S = SUBLANES
    x = xb_ref[...]
    tt = lax.broadcasted_iota(jnp.int32, x.shape, 0)
    xm1 = jnp.where(tt >= 1, pltpu.roll(x, 1, 0), 0.0)
    xp1 = jnp.where(tt < T - 1, pltpu.roll(x, T - 1, 0), 0.0)
    xp2 = jnp.where(tt < T - 2, pltpu.roll(x, T - 2, 0), 0.0)
    xc = (cw_ref[0:1, :] * xm1 + cw_ref[1:2, :] * x + cw_ref[2:3, :] * xp1 + cw_ref[3:4, :] * xp2
          + cb_ref[...])
    xcb = xc.astype(BF16)
    r8 = lax.broadcasted_iota(jnp.int32, (T // S, S, x.shape[1]), 1)
    for d in range(2):
        pre_r = jnp.dot(xcb, gw_ref[d, 0], preferred_element_type=F32) + gbias_ref[2 * d:2 * d + 1, :]
        pre_i = jnp.dot(xcb, gw_ref[d, 1], preferred_element_type=F32) + gbias_ref[2 * d + 1:2 * d + 2, :]
        log_a = LRU_C * _sigmoid(pre_r) * _log_sigmoid(lam_ref[d:d + 1, :])
        a = jnp.exp(log_a)
        u = jnp.sqrt(-jnp.tanh(log_a) * (a * a + 1.0)) * (_sigmoid(pre_i) * xc)
        a = a.reshape(T // S, S, a.shape[1])
        u = u.reshape(a.shape)
        s = 1
        while s < S:
            shift, valid = (s, r8 >= s) if d == 0 else (S - s, r8 < S - s)
            u = jnp.where(valid, a * pltpu.roll(u, shift, 1) + u, u)
            a = jnp.where(valid, a * pltpu.roll(a, shift, 1), a)
            s *= 2
        a_ref[d] = a.reshape(x.shape)
        u_ref[d] = u.reshape(x.shape)

    if zero_init:
        h_init = (jnp.zeros((1, x.shape[1]), F32), jnp.zeros((1, x.shape[1]), F32))
    else:
        h_init = (h0_ref[0:1, :], h0_ref[1:2, :])
    groups = T // S

    def group(i, carry):
        c_f, c_b = carry
        rf = pl.ds(pl.multiple_of(i * S, S), S)
        h_f = a_ref[0, rf, :] * c_f + u_ref[0, rf, :]
        hs_ref[0, rf, :] = h_f
        rb = pl.ds(pl.multiple_of((groups - 1 - i) * S, S), S)
        h_b = a_ref[1, rb, :] * c_b + u_ref[1, rb, :]
        hs_ref[1, rb, :] = h_b
        return h_f[S - 1:S, :], h_b[0:1, :]

    h_f, h_b = lax.fori_loop(0, groups, group, h_init, unroll=4)
    y_ref[...] = ((hs_ref[0] + hs_ref[1]) * _gelu_tanh(br_ref[...])).astype(y_ref.dtype)
    if want_state:
        hl_ref[0:1, :] = h_f
        hl_ref[1:2, :] = h_b


def _lru_core(gx, conv_w, conv_b, gate_w, gate_b, lam, h0, y_prev, *, row0, batch, seq, want_state):
    width = conv_w.shape[1]
    nb = gate_w.shape[2]
    bs = width // nb
    b0 = row0 // seq
    zero_init = h0 is None
    in_specs = [pl.BlockSpec((seq, bs), lambda b, n: (b0 + b, n)),
                pl.BlockSpec((seq, bs), lambda b, n: (b0 + b, nb + n)),
                pl.BlockSpec((conv_w.shape[0], bs), lambda b, n: (0, n)),
                pl.BlockSpec((1, bs), lambda b, n: (0, n)),
                pl.BlockSpec((2, 2, None, bs, bs), lambda b, n: (0, 0, n, 0, 0)),
                pl.BlockSpec((4, bs), lambda b, n: (0, n)),
                pl.BlockSpec((2, bs), lambda b, n: (0, n))]
    args = [gx, gx, conv_w, conv_b.reshape(1, width), gate_w, gate_b.reshape(4, width), lam]
    if not zero_init:
        in_specs.append(pl.BlockSpec((None, 2, bs), lambda b, n: (b, 0, n)))
        args.append(h0)
    aliases = {}
    if y_prev is not None:
        in_specs.append(pl.BlockSpec(memory_space=pl.ANY))
        args.append(y_prev)
        aliases = {len(args) - 1: 0}
    out_specs = [pl.BlockSpec((seq, bs), lambda b, n: (b0 + b, n))]
    out_shape = [jax.ShapeDtypeStruct((gx.shape[0], width), BF16)]
    if want_state:
        out_specs.append(pl.BlockSpec((None, 2, bs), lambda b, n: (b, 0, n)))
        out_shape.append(jax.ShapeDtypeStruct((batch, 2, width), F32))
    return pl.pallas_call(
        functools.partial(_lru_kernel, seq=seq, zero_init=zero_init, want_state=want_state,
                          has_prev=y_prev is not None),
        grid=(batch, nb),
        in_specs=in_specs,
        out_specs=out_specs,
        out_shape=out_shape,
        input_output_aliases=aliases,
        scratch_shapes=[pltpu.VMEM((2, seq, bs), F32)] * 3,
        compiler_params=_cparams("parallel", "parallel"),
        name="lru_core",
    )(*args)


def _mlstm_weights(w_in, b_gate, heads, dk, dv):
    d = w_in.shape[0]
    hk, hv = heads * dk, heads * dv
    wq, wk, wv, wo, wg = jnp.split(w_in, [hk, 2 * hk, 2 * hk + hv, 2 * hk + 2 * hv], axis=1)
    w_qkv = jnp.concatenate([wq, wk * (dk ** -0.5), wv], axis=1).astype(BF16)
    wg = wg.reshape(d, 2, 2, heads).transpose(0, 3, 1, 2).reshape(d, heads, 4)
    wg = jnp.pad(wg, ((0, 0), (0, 0), (0, LANES - 4))).reshape(d, heads * LANES)
    w_og = jnp.concatenate([wo, wg], axis=1).astype(BF16)
    bg = b_gate.astype(F32).transpose(2, 0, 1).reshape(heads, 4)
    bg = jnp.pad(bg, ((0, 0), (0, LANES - 4))).reshape(1, heads * LANES)
    return w_qkv, w_og, bg


def kernel(x_prompt, x_sample, c, state_mlstm_C, state_mlstm_n, state_mlstm_m, state_lru_h, c_ctx,
           mod_w, mod_b, norm_g, ffn_w_up, ffn_w_down, mlstm_w_in, mlstm_b_gate, mlstm_norm_g, mlstm_w_out,
           fnet_w_out, fnet_b_out, lru_w_in, lru_conv_w, lru_conv_b, lru_gate_w, lru_gate_b, lru_lambda,
           lru_w_out):
    bp, tp, d = x_prompt.shape
    bs_, ts, _ = x_sample.shape
    depth = mod_w.shape[0]
    heads = mlstm_b_gate.shape[-1]
    dk, dv = state_mlstm_C.shape[-2:]
    rows_ctx = bp * tp
    rows_all = rows_ctx + bs_ * ts
    rows = dict(rows_ctx=rows_ctx, rows_per_sample=ts)
    assert 1 + bs_ <= N_COND
    tm = math.gcd(512, math.gcd(rows_ctx, ts))
    tm_ffn = math.gcd(1024, math.gcd(rows_ctx, ts))

    x = jnp.concatenate([x_prompt.reshape(rows_ctx, d), x_sample.reshape(bs_ * ts, d)], axis=0)
    cond = jnp.zeros((N_COND, d), F32).at[0].set(c_ctx).at[1:1 + bs_].set(c)
    mods_all = _modulation(cond, mod_w, mod_b).reshape(depth, N_COND, 6, d)
    zero_bias = jnp.zeros((d,), F32)
    trunks = ((0, bp, tp, True), (rows_ctx, bs_, ts, False))

    new_c, new_n, new_m, new_h = [], [], [], []
    for i in range(depth):
        mods = mods_all[i]
        kind, j = i % 3, i // 3
        if kind == 0:
            w_qkv, w_og, bg = _mlstm_weights(mlstm_w_in[j], mlstm_b_gate[j], heads, dk, dv)
            pm = functools.partial(_norm_mod_matmul, x, norm_g[i, 0], mods, sh=0, sc=1, tm=tm, **rows)
            qkv = pm(w_qkv, out_dtype=BF16)
            og = pm(w_og, out_dtype=F32)
            y = None
            for row0, nb, seq, is_ctx in trunks:
                if is_ctx:
                    state = None
                else:
                    m0 = jnp.broadcast_to(state_mlstm_m[:, j].transpose(0, 2, 1)[..., None],
                                          (nb, heads, 2, LANES))
                    state = (state_mlstm_C[:, j], state_mlstm_n[:, j].reshape(nb, 2, heads, 1, dk), m0)
                nsub = math.gcd(nb, max(1, ts // seq))
                res = _mlstm_core(qkv, og, bg, mlstm_norm_g[j], state, y, row0=row0, batch=nb, seq=seq,
                                  nsub=nsub, heads=heads, dk=dk, dv=dv, want_state=is_ctx)
                y = res[0]
                if is_ctx:
                    new_c.append(res[1])
                    new_n.append(res[2].reshape(nb, 2, heads, dk))
                    new_m.append(res[3][..., 0].transpose(0, 2, 1))
            w_out, b_out = mlstm_w_out[j].astype(BF16), zero_bias
        elif kind == 1:
            h = _norm_mod_only(x, norm_g[i, 0], mods, sh=0, sc=1, tm=tm, **rows)
            y = None
            for row0, nb, seq, _ in trunks:
                y = _fnet_mix(h, y, row0=row0, batch=nb, seq=seq)
            w_out, b_out = fnet_w_out[j].astype(BF16), fnet_b_out[j]
        else:
            w_in = lru_w_in[j].astype(BF16)
            gx = _norm_mod_matmul(x, norm_g[i, 0], mods, w_in, sh=0, sc=1, out_dtype=F32, tm=tm, **rows)
            y = None
            for row0, nb, seq, is_ctx in trunks:
                res = _lru_core(gx, lru_conv_w[j], lru_conv_b[j], lru_gate_w[j].astype(BF16), lru_gate_b[j],
                                lru_lambda[j], None if is_ctx else state_lru_h[:, j], y,
                                row0=row0, batch=nb, seq=seq, want_state=is_ctx)
                y = res[0]
                if is_ctx:
                    new_h.append(res[1])
            w_out, b_out = lru_w_out[j].astype(BF16), zero_bias
        x = _matmul_residual(y, w_out, b_out, x, norm_g[i, 1], mods, gate=2, tm=tm, **rows)
        d_ff = ffn_w_up.shape[2]
        tf = math.gcd(d_ff, 512)
        w_up = ffn_w_up[i].reshape(d, d_ff // tf, tf).transpose(1, 0, 2).astype(BF16)
        ffn = functools.partial(_ffn, x, norm_g[i, 2], norm_g[i, 3], mods, w_up, ffn_w_down[i].astype(BF16),
                                tm=tm_ffn, **rows)
        if i < depth - 1:
            x = ffn(row0=0, nrows=rows_all)
        else:
            y_prompt = ffn(row0=0, nrows=rows_ctx).reshape(bp, tp, d)
            y_sample = ffn(row0=rows_ctx, nrows=rows_all - rows_ctx).reshape(bs_, ts, d)

    return (y_prompt, y_sample, jnp.stack(new_c, 1), jnp.stack(new_n, 1), jnp.stack(new_m, 1),
            jnp.stack(new_h, 1))
```

```python
import functools
import math

import numpy as np
import jax
import jax.numpy as jnp
from jax import lax
from jax.experimental import pallas as pl
from jax.experimental.pallas import tpu as pltpu

F32 = jnp.float32
BF16 = jnp.bfloat16
EPS = 1e-6
FNET_GROUPS = 8
LRU_C = 8.0
LANES = 128
MLSTM_L = 256
SUBLANES = 8
VMEM_LIMIT = 56 * 1024 * 1024
N_COND = 16


def _cparams(*sem):
    return pltpu.CompilerParams(dimension_semantics=sem, vmem_limit_bytes=VMEM_LIMIT)


def _sigmoid(x):
    return 0.5 * jnp.tanh(0.5 * x) + 0.5


def _log_sigmoid(x):
    return jnp.minimum(x, 0.0) - jnp.log1p(jnp.exp(-jnp.abs(x)))


def _gelu_tanh(x):
    return 0.5 * x * (1.0 + jnp.tanh(math.sqrt(2.0 / math.pi) * (x + 0.044715 * (x * x * x))))


def _rms(x):
    return x * lax.rsqrt(jnp.mean(x * x, axis=-1, keepdims=True) + EPS)


def _norm_mod(x, g, shift, scale):
    return (_rms(x) * g) * (1.0 + scale) + shift


def _cond_of_block(i, tm, rows_ctx, rows_per_sample):
    r = i * tm
    return jnp.where(r < rows_ctx, 0, 1 + (r - rows_ctx) // rows_per_sample)


def _mod_kernel(c_ref, w_ref, b_ref, o_ref):
    c = c_ref[...]
    a = (c * _sigmoid(c)).astype(BF16)
    o_ref[...] = jnp.dot(a, w_ref[...].astype(BF16), preferred_element_type=F32) + b_ref[...]


def _modulation(cond, mod_w, mod_b):
    depth, d, n = mod_w.shape
    tn = min(n, 1024)
    return pl.pallas_call(
        _mod_kernel,
        grid=(depth, n // tn),
        in_specs=[pl.BlockSpec((N_COND, d), lambda l, j: (0, 0)),
                  pl.BlockSpec((None, d, tn), lambda l, j: (l, 0, j)),
                  pl.BlockSpec((None, 1, tn), lambda l, j: (l, 0, j))],
        out_specs=pl.BlockSpec((None, N_COND, tn), lambda l, j: (l, 0, j)),
        out_shape=jax.ShapeDtypeStruct((depth, N_COND, n), F32),
        compiler_params=_cparams("parallel", "parallel"),
        name="modulation",
    )(cond, mod_w, mod_b.reshape(depth, 1, n))


def _pm_kernel(x_ref, g_ref, mod_ref, w_ref, o_ref, *, sh, sc):
    h = _norm_mod(x_ref[...], g_ref[...], mod_ref[sh:sh + 1, :], mod_ref[sc:sc + 1, :]).astype(BF16)
    o_ref[...] = jnp.dot(h, w_ref[...], preferred_element_type=F32).astype(o_ref.dtype)


def _norm_mod_matmul(x, g, mods, w, layer, *, sh, sc, out_dtype, tm, rows_ctx, rows_per_sample):
    m, d = x.shape
    n = w.shape[2]
    cond = functools.partial(_cond_of_block, tm=tm, rows_ctx=rows_ctx, rows_per_sample=rows_per_sample)
    return pl.pallas_call(
        functools.partial(_pm_kernel, sh=sh, sc=sc),
        grid=(m // tm,),
        in_specs=[pl.BlockSpec((tm, d), lambda i: (i, 0)),
                  pl.BlockSpec((1, d), lambda i: (0, 0)),
                  pl.BlockSpec((None, 6, d), lambda i: (cond(i), 0, 0)),
                  pl.BlockSpec((None, d, n), lambda i: (layer, 0, 0), pipeline_mode=pl.Buffered(1))],
        out_specs=pl.BlockSpec((tm, n), lambda i: (i, 0)),
        out_shape=jax.ShapeDtypeStruct((m, n), out_dtype),
        compiler_params=_cparams("parallel"),
        name="norm_mod_matmul",
    )(x, g.reshape(1, d), mods, w)


def _nm_kernel(x_ref, g_ref, mod_ref, o_ref, *, sh, sc):
    o_ref[...] = _norm_mod(x_ref[...], g_ref[...], mod_ref[sh:sh + 1, :], mod_ref[sc:sc + 1, :]).astype(o_ref.dtype)


def _norm_mod_only(x, g, mods, *, sh, sc, tm, rows_ctx, rows_per_sample):
    m, d = x.shape
    cond = functools.partial(_cond_of_block, tm=tm, rows_ctx=rows_ctx, rows_per_sample=rows_per_sample)
    return pl.pallas_call(
        functools.partial(_nm_kernel, sh=sh, sc=sc),
        grid=(m // tm,),
        in_specs=[pl.BlockSpec((tm, d), lambda i: (i, 0)),
                  pl.BlockSpec((1, d), lambda i: (0, 0)),
                  pl.BlockSpec((None, 6, d), lambda i: (cond(i), 0, 0))],
        out_specs=pl.BlockSpec((tm, d), lambda i: (i, 0)),
        out_shape=jax.ShapeDtypeStruct((m, d), BF16),
        compiler_params=_cparams("parallel"),
        name="norm_mod",
    )(x, g.reshape(1, d), mods)


def _me_kernel(y_ref, w_ref, b_ref, x_ref, g_ref, mod_ref, o_ref, *, gate):
    out = jnp.dot(y_ref[...], w_ref[...], preferred_element_type=F32) + b_ref[...]
    o_ref[...] = x_ref[...] + mod_ref[gate:gate + 1, :] * (_rms(out) * g_ref[...])


def _matmul_residual(y, w, layer, b, x, g, mods, *, gate, tm, rows_ctx, rows_per_sample):
    m, k = y.shape
    d = w.shape[2]
    cond = functools.partial(_cond_of_block, tm=tm, rows_ctx=rows_ctx, rows_per_sample=rows_per_sample)
    return pl.pallas_call(
        functools.partial(_me_kernel, gate=gate),
        grid=(m // tm,),
        in_specs=[pl.BlockSpec((tm, k), lambda i: (i, 0)),
                  pl.BlockSpec((None, k, d), lambda i: (layer, 0, 0)),
                  pl.BlockSpec((1, d), lambda i: (0, 0)),
                  pl.BlockSpec((tm, d), lambda i: (i, 0)),
                  pl.BlockSpec((1, d), lambda i: (0, 0)),
                  pl.BlockSpec((None, 6, d), lambda i: (cond(i), 0, 0))],
        out_specs=pl.BlockSpec((tm, d), lambda i: (i, 0)),
        out_shape=jax.ShapeDtypeStruct((m, d), F32),
        compiler_params=_cparams("parallel"),
        name="matmul_residual",
    )(y, w, b.reshape(1, d), x, g.reshape(1, d), mods)


def _ffn_kernel(x_ref, g1_ref, g2_ref, mod_ref, wu_ref, wd_ref, o_ref, h_ref):
    j = pl.program_id(1)

    @pl.when(j == 0)
    def _():
        h_ref[...] = _norm_mod(x_ref[...], g1_ref[...], mod_ref[3:4, :], mod_ref[4:5, :]).astype(BF16)
        o_ref[...] = jnp.zeros_like(o_ref)

    u = jnp.maximum(jnp.dot(h_ref[...], wu_ref[...], preferred_element_type=F32), 0.0)
    o_ref[...] += jnp.dot((u * u).astype(BF16), wd_ref[...], preferred_element_type=F32)

    @pl.when(j == pl.num_programs(1) - 1)
    def _():
        o_ref[...] = x_ref[...] + mod_ref[5:6, :] * (_rms(o_ref[...]) * g2_ref[...])


def _ffn(x, g_pre, g_post, mods, w_up, w_down, layer, *, row0, nrows, tm, rows_ctx, rows_per_sample):
    d = x.shape[1]
    _, nf, _, tf = w_up.shape
    i0 = row0 // tm
    cond = functools.partial(_cond_of_block, tm=tm, rows_ctx=rows_ctx, rows_per_sample=rows_per_sample)
    return pl.pallas_call(
        _ffn_kernel,
        grid=(nrows // tm, nf),
        in_specs=[pl.BlockSpec((tm, d), lambda i, j: (i0 + i, 0)),
                  pl.BlockSpec((1, d), lambda i, j: (0, 0)),
                  pl.BlockSpec((1, d), lambda i, j: (0, 0)),
                  pl.BlockSpec((None, 6, d), lambda i, j: (cond(i0 + i), 0, 0)),
                  pl.BlockSpec((None, None, d, tf), lambda i, j: (layer, j, 0, 0)),
                  pl.BlockSpec((None, tf, d), lambda i, j: (layer, j, 0))],
        out_specs=pl.BlockSpec((tm, d), lambda i, j: (i, 0)),
        out_shape=jax.ShapeDtypeStruct((nrows, d), F32),
        scratch_shapes=[pltpu.VMEM((tm, d), BF16)],
        compiler_params=_cparams("parallel", "arbitrary"),
        name="ffn",
    )(x, g_pre.reshape(1, d), g_post.reshape(1, d), mods, w_up, w_down)


def _mlstm_kernel(*refs, seq, nsub, chunk, zero_init, want_state, n_alias):
    it = iter(refs)
    q_ref, k_ref, v_ref, o_ref, gt_ref, gb_ref, ng_ref = (next(it) for _ in range(7))
    if not zero_init:
        c0_ref, n0_ref, m0_ref = next(it), next(it), next(it)
    for _ in range(n_alias):
        next(it)
    y_ref = next(it)
    if want_state:
        co_ref, no_ref, mo_ref = next(it), next(it), next(it)
    hf_ref, hb_ref, c_ref, n_ref = (next(it) for _ in range(4))
    L = chunk
    nc = seq // L

    if zero_init:
        c_ref[...] = jnp.zeros_like(c_ref)
        n_ref[...] = jnp.zeros_like(n_ref)
        m_state = [[jnp.zeros((1, 1), F32), jnp.zeros((1, 1), F32)] for _ in range(nsub)]
    else:
        c_ref[...] = c0_ref[...]
        n_ref[...] = n0_ref[...]
        m_state = [[m0_ref[sb, 0:1, 0:1], m0_ref[sb, 1:2, 0:1]] for sb in range(nsub)]

    row = lax.broadcasted_iota(jnp.int32, (L, L), 0)
    col = lax.broadcasted_iota(jnp.int32, (L, L), 1)
    masks = (col <= row, col >= row)
    lane = lax.broadcasted_iota(jnp.int32, (L, LANES), 1)
    is_forget = (lane % 2) == 1

    def rows_of(sb, c):
        return pl.ds(sb * seq + c * L, L)

    scan_order = (tuple(range(nc)), tuple(range(nc - 1, -1, -1)))
    units = [(sb, c, d) for step in range(nc) for sb in range(nsub) for d, c in ((0, step), (1, nc - 1 - step))]

    def log_gates(sb, c):
        gts = gt_ref[rows_of(sb, c), :] + gb_ref[...]
        x = jnp.where(is_forget, _log_sigmoid(gts), gts)
        x_hi = x.astype(BF16)
        r_1 = x - x_hi.astype(F32)
        x_mid = r_1.astype(BF16)
        x_lo = (r_1 - x_mid.astype(F32)).astype(BF16)
        return x, x.T, jnp.concatenate([x_hi, x_mid, x_lo], axis=1)

    chunk_gates = {(sb, c): log_gates(sb, c) for sb in range(nsub) for c in range(nc)}
    gate = {}
    for sb, c, d in units:
        x, xt, x3 = chunk_gates[sb, c]
        y3 = jnp.dot(masks[d].astype(BF16), x3, preferred_element_type=F32)
        y = (y3[:, :LANES] + y3[:, LANES:2 * LANES]) + y3[:, 2 * LANES:]
        yt = y.T
        b_c = y[:, 2 * d + 1:2 * d + 2]
        a_c = x[:, 2 * d:2 * d + 1] - b_c
        a_r = xt[2 * d:2 * d + 1, :] - yt[2 * d + 1:2 * d + 2, :]
        dm = jnp.where(masks[d], b_c + a_r, -jnp.inf)
        b_last = b_c[L - 1:L, :] if d == 0 else b_c[0:1, :]
        g_c = b_last + a_c
        gate[sb, c, d] = (dm, jnp.max(dm, axis=1, keepdims=True), b_c, b_last, g_c,
                          jnp.max(g_c, axis=0, keepdims=True))

    m_in, m_out = {}, {}
    for sb in range(nsub):
        for d in range(2):
            m = m_state[sb][d]
            for c in scan_order[d]:
                _, _, _, b_last, _, g_max = gate[sb, c, d]
                m_in[sb, c, d] = m
                m = jnp.maximum(b_last + m, g_max)
                m_out[sb, c, d] = m
            m_state[sb][d] = m

    part = {}
    for sb, c, d in units:
        dm, r_max, b_c, b_last, g_c, _ = gate[sb, c, d]
        rows = rows_of(sb, c)
        q = q_ref[rows, :]
        k = k_ref[rows, :]
        v = v_ref[rows, :]
        inter = b_c + m_in[sb, c, d]
        m_t = jnp.maximum(inter, r_max)
        s = lax.dot_general(q, k, (((1,), (1,)), ((), ())), preferred_element_type=F32) * jnp.exp(dm - m_t)
        kw = k.astype(F32) * jnp.exp(g_c - m_out[sb, c, d])
        part[sb, c, d] = (jnp.dot(s.astype(BF16), v, preferred_element_type=F32),
                          jnp.sum(s, axis=1, keepdims=True), jnp.exp(inter - m_t), jnp.exp(-m_t),
                          jnp.exp(b_last + m_in[sb, c, d] - m_out[sb, c, d]), kw.astype(BF16),
                          jnp.sum(kw, axis=0, keepdims=True))

    for sb, c, d in units:
        sv, s_sum, w_inter, den_floor, w_c, kw, kw_sum = part[sb, c, d]
        rows = rows_of(sb, c)
        q = q_ref[rows, :]
        c_old = c_ref[sb, d]
        n_old = n_ref[sb, d]
        num = sv + w_inter * jnp.dot(q, c_old.astype(BF16), preferred_element_type=F32)
        den = s_sum + w_inter * jnp.sum(q.astype(F32) * n_old, axis=1, keepdims=True)
        (hf_ref, hb_ref)[d][rows, :] = num / jnp.maximum(jnp.abs(den), den_floor)
        c_ref[sb, d] = w_c * c_old + lax.dot_general(kw, v_ref[rows, :], (((0,), (0,)), ((), ())),
                                                     preferred_element_type=F32)
        n_ref[sb, d] = w_c * n_old + kw_sum

    def finish(c, _):
        rows = pl.ds(pl.multiple_of(c * L, L), L)
        hs = hf_ref[rows, :] + hb_ref[rows, :]
        y_ref[rows, :] = (_rms(hs) * ng_ref[...] * _sigmoid(o_ref[rows, :])).astype(y_ref.dtype)
        return 0

    lax.fori_loop(0, nsub * nc, finish, 0)

    if want_state:
        co_ref[...] = c_ref[...]
        no_ref[...] = n_ref[...]
        for sb in range(nsub):
            for d in range(2):
                mo_ref[sb, d:d + 1, :] = jnp.broadcast_to(m_state[sb][d], (1, LANES))


def _mlstm_core(qkv, og, gate_b, norm_g, state, y_prev, state_prev, *, layer, n_layers, row0, batch, seq, nsub,
                heads, dk, dv, want_state):
    L = math.gcd(MLSTM_L, seq)
    rb = nsub * seq
    assert row0 % rb == 0 and batch % nsub == 0
    b0 = row0 // rb
    hk = heads * dk
    zero_init = state is None
    state_specs = [pl.BlockSpec((nsub, None, 2, None, dk, dv), lambda b, h: (b, layer, 0, h, 0, 0)),
                   pl.BlockSpec((nsub, None, 2, None, 1, dk), lambda b, h: (b, layer, 0, h, 0, 0)),
                   pl.BlockSpec((nsub, None, None, 2, LANES), lambda b, h: (b, layer, h, 0, 0))]
    in_specs = [pl.BlockSpec((rb, dk), lambda b, h: (b0 + b, h)),
                pl.BlockSpec((rb, dk), lambda b, h: (b0 + b, heads + h)),
                pl.BlockSpec((rb, dv), lambda b, h: (b0 + b, 2 * hk // dv + h)),
                pl.BlockSpec((rb, dv), lambda b, h: (b0 + b, h)),
                pl.BlockSpec((rb, LANES), lambda b, h: (b0 + b, heads * dv // LANES + h)),
                pl.BlockSpec((1, LANES), lambda b, h: (0, h)),
                pl.BlockSpec((1, dv), lambda b, h: (0, h))]
    args = [qkv, qkv, qkv, og, og, gate_b, norm_g.reshape(1, heads * dv)]
    if not zero_init:
        in_specs += state_specs
        args += list(state)
    aliases = {}
    if y_prev is not None:
        in_specs.append(pl.BlockSpec(memory_space=pl.ANY))
        args.append(y_prev)
        aliases[len(args) - 1] = 0
    out_specs = [pl.BlockSpec((rb, dv), lambda b, h: (b0 + b, h))]
    out_shape = [jax.ShapeDtypeStruct((qkv.shape[0], heads * dv), BF16)]
    if want_state:
        out_specs += state_specs
        out_shape += [jax.ShapeDtypeStruct((batch, n_layers, 2, heads, dk, dv), F32),
                      jax.ShapeDtypeStruct((batch, n_layers, 2, heads, 1, dk), F32),
                      jax.ShapeDtypeStruct((batch, n_layers, heads, 2, LANES), F32)]
        if state_prev is not None:
            for k_out, arr in enumerate(state_prev):
                in_specs.append(pl.BlockSpec(memory_space=pl.ANY))
                args.append(arr)
                aliases[len(args) - 1] = 1 + k_out
    return pl.pallas_call(
        functools.partial(_mlstm_kernel, seq=seq, nsub=nsub, chunk=L, zero_init=zero_init,
                          want_state=want_state, n_alias=len(aliases)),
        grid=(batch // nsub, heads),
        in_specs=in_specs,
        out_specs=out_specs,
        out_shape=out_shape,
        input_output_aliases=aliases,
        scratch_shapes=[pltpu.VMEM((rb, dv), F32), pltpu.VMEM((rb, dv), F32),
                        pltpu.VMEM((nsub, 2, dk, dv), F32), pltpu.VMEM((nsub, 2, 1, dk), F32)],
        compiler_params=_cparams("parallel", "parallel"),
        name="mlstm_core",
    )(*args)


def _dft_mats(seq, gc):
    def cs(n):
        idx = np.arange(n)
        ang = 2.0 * np.pi * ((idx[:, None] * idx[None, :]) % n) / n
        return np.cos(ang) / math.sqrt(n), np.sin(ang) / math.sqrt(n)
    ct, st = cs(seq)
    cc, sc = cs(gc)
    w_time = np.concatenate([ct, -st], axis=1)
    w_chan = np.concatenate([cc, sc], axis=1)
    return jnp.asarray(w_time, BF16), jnp.asarray(w_chan, BF16)


def _fnet_kernel(h_ref, wt_ref, wc_ref, *rest, gc):
    o_ref = rest[-1]
    y = jnp.dot(h_ref[...], wc_ref[...], preferred_element_type=F32)
    ycat = jnp.concatenate([y[:, :gc], y[:, gc:]], axis=0).astype(BF16)
    o_ref[...] = jnp.dot(wt_ref[...], ycat, preferred_element_type=F32).astype(o_ref.dtype)


def _fnet_mix(h, y_prev, *, row0, batch, seq):
    d = h.shape[1]
    gc = d // FNET_GROUPS
    b0 = row0 // seq
    w_time, w_chan = _dft_mats(seq, gc)
    in_specs = [pl.BlockSpec((seq, gc), lambda b, g: (b0 + b, g)),
                pl.BlockSpec((seq, 2 * seq), lambda b, g: (0, 0)),
                pl.BlockSpec((gc, 2 * gc), lambda b, g: (0, 0))]
    args = [h, w_time, w_chan]
    aliases = {}
    if y_prev is not None:
        in_specs.append(pl.BlockSpec(memory_space=pl.ANY))
        args.append(y_prev)
        aliases = {len(args) - 1: 0}
    return pl.pallas_call(
        functools.partial(_fnet_kernel, gc=gc),
        grid=(batch, FNET_GROUPS),
        in_specs=in_specs,
        out_specs=pl.BlockSpec((seq, gc), lambda b, g: (b0 + b, g)),
        out_shape=jax.ShapeDtypeStruct(h.shape, BF16),
        input_output_aliases=aliases,
        compiler_params=_cparams("parallel", "parallel"),
        name="fnet_mix",
    )(*args)


def _lru_kernel(*refs, seq, zero_init, want_state, has_prev):
    it = iter(refs)
    br_ref, xb_ref, cw_ref, cb_ref, gw_ref, gbias_ref, lam_ref = (next(it) for _ in range(7))
    if not zero_init:
        h0_ref = next(it)
    if has_prev:
        next(it)
    y_ref = next(it)
    if want_state:
        hl_ref = next(it)
    a_ref, u_ref, hs_ref = (next(it) for _ in range(3))
    T = seq
    S = SUBLANES
    x = xb_ref[...]
    tt = lax.broadcasted_iota(jnp.int32, x.shape, 0)
    xm1 = jnp.where(tt >= 1, pltpu.roll(x, 1, 0), 0.0)
    xp1 = jnp.where(tt < T - 1, pltpu.roll(x, T - 1, 0), 0.0)
    xp2 = jnp.where(tt < T - 2, pltpu.roll(x, T - 2, 0), 0.0)
    xc = (cw_ref[0:1, :] * xm1 + cw_ref[1:2, :] * x + cw_ref[2:3, :] * xp1 + cw_ref[3:4, :] * xp2
          + cb_ref[...])
    xcb = xc.astype(BF16)
    r8 = lax.broadcasted_iota(jnp.int32, (T // S, S, x.shape[1]), 1)
    for d in range(2):
        pre_r = jnp.dot(xcb, gw_ref[d, 0], preferred_element_type=F32) + gbias_ref[2 * d:2 * d + 1, :]
        pre_i = jnp.dot(xcb, gw_ref[d, 1], preferred_element_type=F32) + gbias_ref[2 * d + 1:2 * d + 2, :]
        log_a = LRU_C * _sigmoid(pre_r) * _log_sigmoid(lam_ref[d:d + 1, :])
        a = jnp.exp(log_a)
        u = jnp.sqrt(-jnp.tanh(log_a) * (a * a + 1.0)) * (_sigmoid(pre_i) * xc)
        a = a.reshape(T // S, S, a.shape[1])
        u = u.reshape(a.shape)
        s = 1
        while s < S:
            shift, valid = (s, r8 >= s) if d == 0 else (S - s, r8 < S - s)
            u = jnp.where(valid, a * pltpu.roll(u, shift, 1) + u, u)
            a = jnp.where(valid, a * pltpu.roll(a, shift, 1), a)
            s *= 2
        a_ref[d] = a.reshape(x.shape)
        u_ref[d] = u.reshape(x.shape)

    if zero_init:
        h_init = (jnp.zeros((1, x.shape[1]), F32), jnp.zeros((1, x.shape[1]), F32))
    else:
        h_init = (h0_ref[0:1, :], h0_ref[1:2, :])
    groups = T // S

    def group(i, carry):
        c_f, c_b = carry
        rf = pl.ds(pl.multiple_of(i * S, S), S)
        h_f = a_ref[0, rf, :] * c_f + u_ref[0, rf, :]
        hs_ref[0, rf, :] = h_f
        rb = pl.ds(pl.multiple_of((groups - 1 - i) * S, S), S)
        h_b = a_ref[1, rb, :] * c_b + u_ref[1, rb, :]
        hs_ref[1, rb, :] = h_b
        return h_f[S - 1:S, :], h_b[0:1, :]

    h_f, h_b = lax.fori_loop(0, groups, group, h_init, unroll=4)
    y_ref[...] = ((hs_ref[0] + hs_ref[1]) * _gelu_tanh(br_ref[...])).astype(y_ref.dtype)
    if want_state:
        hl_ref[0:1, :] = h_f
        hl_ref[1:2, :] = h_b


def _lru_core(gx, conv_w, conv_b, gate_w, gate_b, lam, h0, y_prev, *, row0, batch, seq, want_state):
    width = conv_w.shape[1]
    nb = gate_w.shape[2]
    bs = width // nb
    b0 = row0 // seq
    zero_init = h0 is None
    in_specs = [pl.BlockSpec((seq, bs), lambda b, n: (b0 + b, n)),
                pl.BlockSpec((seq, bs), lambda b, n: (b0 + b, nb + n)),
                pl.BlockSpec((conv_w.shape[0], bs), lambda b, n: (0, n)),
                pl.BlockSpec((1, bs), lambda b, n: (0, n)),
                pl.BlockSpec((2, 2, None, bs, bs), lambda b, n: (0, 0, n, 0, 0)),
                pl.BlockSpec((4, bs), lambda b, n: (0, n)),
                pl.BlockSpec((2, bs), lambda b, n: (0, n))]
    args = [gx, gx, conv_w, conv_b.reshape(1, width), gate_w, gate_b.reshape(4, width), lam]
    if not zero_init:
        in_specs.append(pl.BlockSpec((None, 2, bs), lambda b, n: (b, 0, n)))
        args.append(h0)
    aliases = {}
    if y_prev is not None:
        in_specs.append(pl.BlockSpec(memory_space=pl.ANY))
        args.append(y_prev)
        aliases = {len(args) - 1: 0}
    out_specs = [pl.BlockSpec((seq, bs), lambda b, n: (b0 + b, n))]
    out_shape = [jax.ShapeDtypeStruct((gx.shape[0], width), BF16)]
    if want_state:
        out_specs.append(pl.BlockSpec((None, 2, bs), lambda b, n: (b, 0, n)))
        out_shape.append(jax.ShapeDtypeStruct((batch, 2, width), F32))
    return pl.pallas_call(
        functools.partial(_lru_kernel, seq=seq, zero_init=zero_init, want_state=want_state,
                          has_prev=y_prev is not None),
        grid=(batch, nb),
        in_specs=in_specs,
        out_specs=out_specs,
        out_shape=out_shape,
        input_output_aliases=aliases,
        scratch_shapes=[pltpu.VMEM((2, seq, bs), F32)] * 3,
        compiler_params=_cparams("parallel", "parallel"),
        name="lru_core",
    )(*args)


def _mlstm_weights(w_in, b_gate, heads, dk, dv):
    nl, d, _ = w_in.shape
    hk, hv = heads * dk, heads * dv
    wq, wk, wv, wo, wg = jnp.split(w_in, [hk, 2 * hk, 2 * hk + hv, 2 * hk + 2 * hv], axis=2)
    w_qkv = jnp.concatenate([wq, wk * (dk ** -0.5), wv], axis=2).astype(BF16)
    wg = wg.reshape(nl, d, 2, 2, heads).transpose(0, 1, 4, 2, 3).reshape(nl, d, heads, 4)
    wg = jnp.pad(wg, ((0, 0), (0, 0), (0, 0), (0, LANES - 4))).reshape(nl, d, heads * LANES)
    w_og = jnp.concatenate([wo, wg], axis=2).astype(BF16)
    bg = b_gate.astype(F32).transpose(0, 3, 1, 2).reshape(nl, heads, 4)
    bg = jnp.pad(bg, ((0, 0), (0, 0), (0, LANES - 4))).reshape(nl, 1, heads * LANES)
    return w_qkv, w_og, bg


def kernel(x_prompt, x_sample, c, state_mlstm_C, state_mlstm_n, state_mlstm_m, state_lru_h, c_ctx,
           mod_w, mod_b, norm_g, ffn_w_up, ffn_w_down, mlstm_w_in, mlstm_b_gate, mlstm_norm_g, mlstm_w_out,
           fnet_w_out, fnet_b_out, lru_w_in, lru_conv_w, lru_conv_b, lru_gate_w, lru_gate_b, lru_lambda,
           lru_w_out):
    bp, tp, d = x_prompt.shape
    bs_, ts, _ = x_sample.shape
    depth = mod_w.shape[0]
    heads = mlstm_b_gate.shape[-1]
    dk, dv = state_mlstm_C.shape[-2:]
    rows_ctx = bp * tp
    rows_all = rows_ctx + bs_ * ts
    rows = dict(rows_ctx=rows_ctx, rows_per_sample=ts)
    assert 1 + bs_ <= N_COND
    tm = math.gcd(512, math.gcd(rows_ctx, ts))
    tm_ffn = math.gcd(1024, math.gcd(rows_ctx, ts))

    x = jnp.concatenate([x_prompt.reshape(rows_ctx, d), x_sample.reshape(bs_ * ts, d)], axis=0)
    cond = jnp.zeros((N_COND, d), F32).at[0].set(c_ctx).at[1:1 + bs_].set(c)
    mods_all = _modulation(cond, mod_w, mod_b).reshape(depth, N_COND, 6, d)
    zero_bias = jnp.zeros((d,), F32)
    trunks = ((0, bp, tp, True), (rows_ctx, bs_, ts, False))

    n_ml = mlstm_w_in.shape[0]
    w_qkv, w_og, bg = _mlstm_weights(mlstm_w_in, mlstm_b_gate, heads, dk, dv)
    mlstm_state_in = (state_mlstm_C, state_mlstm_n.reshape(bs_, n_ml, 2, heads, 1, dk),
                      jnp.broadcast_to(state_mlstm_m.transpose(0, 1, 3, 2)[..., None], (bs_, n_ml, heads, 2, LANES)))
    w_out_of = (mlstm_w_out.astype(BF16), fnet_w_out.astype(BF16), lru_w_out.astype(BF16))
    lru_w_in_b, lru_gate_w_b = lru_w_in.astype(BF16), lru_gate_w.astype(BF16)
    d_ff = ffn_w_up.shape[2]
    tf = math.gcd(d_ff, 512)
    ffn_w_up_b = ffn_w_up.reshape(depth, d, d_ff // tf, tf).transpose(0, 2, 1, 3).astype(BF16)
    ffn_w_down_b = ffn_w_down.astype(BF16)

    mlstm_state_out, new_h = None, []
    for i in range(depth):
        mods = mods_all[i]
        kind, j = i % 3, i // 3
        b_out = zero_bias
        if kind == 0:
            pm = functools.partial(_norm_mod_matmul, x, norm_g[i, 0], mods, sh=0, sc=1, tm=tm, **rows)
            qkv = pm(w_qkv, j, out_dtype=BF16)
            og = pm(w_og, j, out_dtype=F32)
            y = None
            for row0, nb, seq, is_ctx in trunks:
                nsub = math.gcd(nb, max(1, ts // seq))
                res = _mlstm_core(qkv, og, bg[j], mlstm_norm_g[j], None if is_ctx else mlstm_state_in, y,
                                  mlstm_state_out if is_ctx else None, layer=j, n_layers=n_ml, row0=row0,
                                  batch=nb, seq=seq, nsub=nsub, heads=heads, dk=dk, dv=dv, want_state=is_ctx)
                y = res[0]
                if is_ctx:
                    mlstm_state_out = res[1:]
        elif kind == 1:
            h = _norm_mod_only(x, norm_g[i, 0], mods, sh=0, sc=1, tm=tm, **rows)
            y = None
            for row0, nb, seq, _ in trunks:
                y = _fnet_mix(h, y, row0=row0, batch=nb, seq=seq)
            b_out = fnet_b_out[j]
        else:
            gx = _norm_mod_matmul(x, norm_g[i, 0], mods, lru_w_in_b, j, sh=0, sc=1, out_dtype=F32, tm=tm, **rows)
            y = None
            for row0, nb, seq, is_ctx in trunks:
                res = _lru_core(gx, lru_conv_w[j], lru_conv_b[j], lru_gate_w_b[j], lru_gate_b[j],
                                lru_lambda[j], None if is_ctx else state_lru_h[:, j], y,
                                row0=row0, batch=nb, seq=seq, want_state=is_ctx)
                y = res[0]
                if is_ctx:
                    new_h.append(res[1])
        x = _matmul_residual(y, w_out_of[kind], j, b_out, x, norm_g[i, 1], mods, gate=2, tm=tm, **rows)
        ffn = functools.partial(_ffn, x, norm_g[i, 2], norm_g[i, 3], mods, ffn_w_up_b, ffn_w_down_b, i,
                                tm=tm_ffn, **rows)
        if i < depth - 1:
            x = ffn(row0=0, nrows=rows_all)
        else:
            y_prompt = ffn(row0=0, nrows=rows_ctx).reshape(bp, tp, d)
            y_sample = ffn(row0=rows_ctx, nrows=rows_all - rows_ctx).reshape(bs_, ts, d)

    new_c, new_n, new_m = mlstm_state_out
    return (y_prompt, y_sample, new_c, new_n.reshape(bp, n_ml, 2, heads, dk),
            new_m[..., 0].transpose(0, 1, 3, 2), jnp.stack(new_h, 1))
```

```python
import functools
import math

import numpy as np
import jax
import jax.numpy as jnp
from jax import lax
from jax.experimental import pallas as pl
from jax.experimental.pallas import tpu as pltpu

F32 = jnp.float32
BF16 = jnp.bfloat16
EPS = 1e-6
FNET_GROUPS = 8
LRU_C = 8.0
LANES = 128
MLSTM_L = 256
SUBLANES = 8
VMEM_LIMIT = 56 * 1024 * 1024
N_COND = 16


def _cparams(*sem):
    return pltpu.CompilerParams(dimension_semantics=sem, vmem_limit_bytes=VMEM_LIMIT)


def _sigmoid(x):
    return 0.5 * jnp.tanh(0.5 * x) + 0.5


def _log_sigmoid(x):
    return jnp.minimum(x, 0.0) - jnp.log1p(jnp.exp(-jnp.abs(x)))


def _gelu_tanh(x):
    return 0.5 * x * (1.0 + jnp.tanh(math.sqrt(2.0 / math.pi) * (x + 0.044715 * (x * x * x))))


def _rms(x):
    return x * lax.rsqrt(jnp.mean(x * x, axis=-1, keepdims=True) + EPS)


def _norm_mod(x, g, shift, scale):
    return (_rms(x) * g) * (1.0 + scale) + shift


def _cond_of_block(i, tm, rows_ctx, rows_per_sample):
    r = i * tm
    return jnp.where(r < rows_ctx, 0, 1 + (r - rows_ctx) // rows_per_sample)


def _mod_kernel(c_ref, w_ref, b_ref, o_ref):
    @pl.when(pl.program_id(1) == 0)
    def _():
        o_ref[...] = jnp.broadcast_to(b_ref[...], o_ref.shape)

    c = c_ref[...]
    a = (c * _sigmoid(c)).astype(BF16)
    o_ref[...] += jnp.dot(a, w_ref[...].astype(BF16), preferred_element_type=F32)


def _modulation(cond, mod_w, mod_b):
    depth, d, n = mod_w.shape
    tk = math.gcd(d, 2 * LANES)
    return pl.pallas_call(
        _mod_kernel,
        grid=(depth, d // tk),
        in_specs=[pl.BlockSpec((N_COND, tk), lambda l, k: (0, k)),
                  pl.BlockSpec((None, tk, n), lambda l, k: (l, k, 0)),
                  pl.BlockSpec((None, 1, n), lambda l, k: (l, 0, 0))],
        out_specs=pl.BlockSpec((None, N_COND, n), lambda l, k: (l, 0, 0)),
        out_shape=jax.ShapeDtypeStruct((depth, N_COND, n), F32),
        compiler_params=_cparams("parallel", "arbitrary"),
        name="modulation",
    )(cond, mod_w, mod_b.reshape(depth, 1, n))


def _pm_kernel(x_ref, g_ref, mod_ref, w_ref, *rest, sh, sc):
    o_ref = rest[-1]
    h = _norm_mod(x_ref[...], g_ref[...], mod_ref[sh:sh + 1, :], mod_ref[sc:sc + 1, :]).astype(BF16)
    o_ref[...] = jnp.dot(h, w_ref[...], preferred_element_type=F32).astype(o_ref.dtype)


def _norm_mod_matmul(x_parts, g, mods, w, layer, *, sh, sc, out_dtype, tm, rows_all, rows_ctx, rows_per_sample):
    d, n = w.shape[1:]
    cond = functools.partial(_cond_of_block, tm=tm, rows_ctx=rows_ctx, rows_per_sample=rows_per_sample)
    out = None
    for x, row0 in x_parts:
        i0 = row0 // tm
        in_specs = [pl.BlockSpec((tm, d), lambda i: (i, 0)),
                    pl.BlockSpec((1, d), lambda i: (0, 0)),
                    pl.BlockSpec((None, 6, d), lambda i, i0=i0: (cond(i0 + i), 0, 0)),
                    pl.BlockSpec((None, d, n), lambda i: (layer, 0, 0), pipeline_mode=pl.Buffered(1))]
        args = [x, g.reshape(1, d), mods, w]
        if out is not None:
            in_specs.append(pl.BlockSpec(memory_space=pl.ANY))
            args.append(out)
        out = pl.pallas_call(
            functools.partial(_pm_kernel, sh=sh, sc=sc),
            grid=(x.shape[0] // tm,),
            in_specs=in_specs,
            out_specs=pl.BlockSpec((tm, n), lambda i, i0=i0: (i0 + i, 0)),
            out_shape=jax.ShapeDtypeStruct((rows_all, n), out_dtype),
            input_output_aliases={4: 0} if len(args) == 5 else {},
            compiler_params=_cparams("parallel"),
            name="norm_mod_matmul",
        )(*args)
    return out


def _nm_kernel(x_ref, g_ref, mod_ref, o_ref, *, sh, sc):
    o_ref[...] = _norm_mod(x_ref[...], g_ref[...], mod_ref[sh:sh + 1, :], mod_ref[sc:sc + 1, :]).astype(o_ref.dtype)


def _norm_mod_only(x, g, mods, *, sh, sc, tm, rows_all, rows_ctx, rows_per_sample):
    m, d = x.shape
    assert m == rows_all
    cond = functools.partial(_cond_of_block, tm=tm, rows_ctx=rows_ctx, rows_per_sample=rows_per_sample)
    return pl.pallas_call(
        functools.partial(_nm_kernel, sh=sh, sc=sc),
        grid=(m // tm,),
        in_specs=[pl.BlockSpec((tm, d), lambda i: (i, 0)),
                  pl.BlockSpec((1, d), lambda i: (0, 0)),
                  pl.BlockSpec((None, 6, d), lambda i: (cond(i), 0, 0))],
        out_specs=pl.BlockSpec((tm, d), lambda i: (i, 0)),
        out_shape=jax.ShapeDtypeStruct((m, d), BF16),
        compiler_params=_cparams("parallel"),
        name="norm_mod",
    )(x, g.reshape(1, d), mods)


def _me_kernel(y_ref, w_ref, b_ref, x_ref, g_ref, mod_ref, *rest, gate):
    o_ref = rest[-1]
    out = jnp.dot(y_ref[...], w_ref[...], preferred_element_type=F32) + b_ref[...]
    o_ref[...] = x_ref[...] + mod_ref[gate:gate + 1, :] * (_rms(out) * g_ref[...])


def _matmul_residual(y, w, layer, b, x_parts, g, mods, *, gate, tm, rows_all, rows_ctx, rows_per_sample):
    k, d = w.shape[1:]
    cond = functools.partial(_cond_of_block, tm=tm, rows_ctx=rows_ctx, rows_per_sample=rows_per_sample)
    out = None
    for x, row0 in x_parts:
        i0 = row0 // tm
        in_specs = [pl.BlockSpec((tm, k), lambda i, i0=i0: (i0 + i, 0)),
                    pl.BlockSpec((None, k, d), lambda i: (layer, 0, 0)),
                    pl.BlockSpec((1, d), lambda i: (0, 0)),
                    pl.BlockSpec((tm, d), lambda i: (i, 0)),
                    pl.BlockSpec((1, d), lambda i: (0, 0)),
                    pl.BlockSpec((None, 6, d), lambda i, i0=i0: (cond(i0 + i), 0, 0))]
        args = [y, w, b.reshape(1, d), x, g.reshape(1, d), mods]
        if out is not None:
            in_specs.append(pl.BlockSpec(memory_space=pl.ANY))
            args.append(out)
        out = pl.pallas_call(
            functools.partial(_me_kernel, gate=gate),
            grid=(x.shape[0] // tm,),
            in_specs=in_specs,
            out_specs=pl.BlockSpec((tm, d), lambda i, i0=i0: (i0 + i, 0)),
            out_shape=jax.ShapeDtypeStruct((rows_all, d), F32),
            input_output_aliases={6: 0} if len(args) == 7 else {},
            compiler_params=_cparams("parallel"),
            name="matmul_residual",
        )(*args)
    return out


def _ffn_kernel(x_ref, g1_ref, g2_ref, mod_ref, wu_ref, wd_ref, o_ref, h_ref):
    j = pl.program_id(1)

    @pl.when(j == 0)
    def _():
        h_ref[...] = _norm_mod(x_ref[...], g1_ref[...], mod_ref[3:4, :], mod_ref[4:5, :]).astype(BF16)
        o_ref[...] = jnp.zeros_like(o_ref)

    u = jnp.maximum(jnp.dot(h_ref[...], wu_ref[...], preferred_element_type=F32), 0.0)
    o_ref[...] += jnp.dot((u * u).astype(BF16), wd_ref[...], preferred_element_type=F32)

    @pl.when(j == pl.num_programs(1) - 1)
    def _():
        o_ref[...] = x_ref[...] + mod_ref[5:6, :] * (_rms(o_ref[...]) * g2_ref[...])


def _ffn(x, g_pre, g_post, mods, w_up, w_down, layer, *, row0, nrows, tm, rows_ctx, rows_per_sample):
    d = x.shape[1]
    _, nf, _, tf = w_up.shape
    i0 = row0 // tm
    cond = functools.partial(_cond_of_block, tm=tm, rows_ctx=rows_ctx, rows_per_sample=rows_per_sample)
    return pl.pallas_call(
        _ffn_kernel,
        grid=(nrows // tm, nf),
        in_specs=[pl.BlockSpec((tm, d), lambda i, j: (i0 + i, 0)),
                  pl.BlockSpec((1, d), lambda i, j: (0, 0)),
                  pl.BlockSpec((1, d), lambda i, j: (0, 0)),
                  pl.BlockSpec((None, 6, d), lambda i, j: (cond(i0 + i), 0, 0)),
                  pl.BlockSpec((None, None, d, tf), lambda i, j: (layer, j, 0, 0)),
                  pl.BlockSpec((None, tf, d), lambda i, j: (layer, j, 0))],
        out_specs=pl.BlockSpec((tm, d), lambda i, j: (i, 0)),
        out_shape=jax.ShapeDtypeStruct((nrows, d), F32),
        scratch_shapes=[pltpu.VMEM((tm, d), BF16)],
        compiler_params=_cparams("parallel", "arbitrary"),
        name="ffn",
    )(x, g_pre.reshape(1, d), g_post.reshape(1, d), mods, w_up, w_down)


def _mlstm_kernel(*refs, seq, nsub, chunk, zero_init, want_state, n_alias):
    it = iter(refs)
    q_ref, k_ref, v_ref, o_ref, gt_ref, gb_ref, ng_ref = (next(it) for _ in range(7))
    if not zero_init:
        c0_ref, n0_ref, m0_ref = next(it), next(it), next(it)
    for _ in range(n_alias):
        next(it)
    y_ref = next(it)
    if want_state:
        co_ref, no_ref, mo_ref = next(it), next(it), next(it)
    hf_ref, hb_ref, c_ref, n_ref = (next(it) for _ in range(4))
    L = chunk
    nc = seq // L

    if zero_init:
        c_ref[...] = jnp.zeros_like(c_ref)
        n_ref[...] = jnp.zeros_like(n_ref)
        m_state = [[jnp.zeros((1, 1), F32), jnp.zeros((1, 1), F32)] for _ in range(nsub)]
    else:
        c_ref[...] = c0_ref[...]
        n_ref[...] = n0_ref[...]
        m_state = [[m0_ref[sb, 0:1, 0:1], m0_ref[sb, 1:2, 0:1]] for sb in range(nsub)]

    row = lax.broadcasted_iota(jnp.int32, (L, L), 0)
    col = lax.broadcasted_iota(jnp.int32, (L, L), 1)
    masks = (col <= row, col >= row)
    lane = lax.broadcasted_iota(jnp.int32, (L, LANES), 1)
    is_forget = (lane % 2) == 1

    def rows_of(sb, c):
        return pl.ds(sb * seq + c * L, L)

    scan_order = (tuple(range(nc)), tuple(range(nc - 1, -1, -1)))
    units = [(sb, c, d) for step in range(nc) for sb in range(nsub) for d, c in ((0, step), (1, nc - 1 - step))]

    def log_gates(sb, c):
        gts = gt_ref[rows_of(sb, c), :] + gb_ref[...]
        x = jnp.where(is_forget, _log_sigmoid(gts), gts)
        x_hi = x.astype(BF16)
        r_1 = x - x_hi.astype(F32)
        x_mid = r_1.astype(BF16)
        x_lo = (r_1 - x_mid.astype(F32)).astype(BF16)
        return x, x.T, jnp.concatenate([x_hi, x_mid, x_lo], axis=1)

    chunk_gates = {(sb, c): log_gates(sb, c) for sb in range(nsub) for c in range(nc)}
    gate = {}
    for sb, c, d in units:
        x, xt, x3 = chunk_gates[sb, c]
        y3 = jnp.dot(masks[d].astype(BF16), x3, preferred_element_type=F32)
        y = (y3[:, :LANES] + y3[:, LANES:2 * LANES]) + y3[:, 2 * LANES:]
        yt = y.T
        b_c = y[:, 2 * d + 1:2 * d + 2]
        a_c = x[:, 2 * d:2 * d + 1] - b_c
        a_r = xt[2 * d:2 * d + 1, :] - yt[2 * d + 1:2 * d + 2, :]
        dm = jnp.where(masks[d], b_c + a_r, -jnp.inf)
        b_last = b_c[L - 1:L, :] if d == 0 else b_c[0:1, :]
        g_c = b_last + a_c
        gate[sb, c, d] = (dm, jnp.max(dm, axis=1, keepdims=True), b_c, b_last, g_c,
                          jnp.max(g_c, axis=0, keepdims=True))

    m_in, m_out = {}, {}
    for sb in range(nsub):
        for d in range(2):
            m = m_state[sb][d]
            for c in scan_order[d]:
                _, _, _, b_last, _, g_max = gate[sb, c, d]
                m_in[sb, c, d] = m
                m = jnp.maximum(b_last + m, g_max)
                m_out[sb, c, d] = m
            m_state[sb][d] = m

    part = {}
    for sb, c, d in units:
        dm, r_max, b_c, b_last, g_c, _ = gate[sb, c, d]
        rows = rows_of(sb, c)
        q = q_ref[rows, :]
        k = k_ref[rows, :]
        v = v_ref[rows, :]
        inter = b_c + m_in[sb, c, d]
        m_t = jnp.maximum(inter, r_max)
        s = lax.dot_general(q, k, (((1,), (1,)), ((), ())), preferred_element_type=F32) * jnp.exp(dm - m_t)
        kw = k.astype(F32) * jnp.exp(g_c - m_out[sb, c, d])
        sv = jnp.dot(s.astype(BF16), jnp.concatenate([v, jnp.ones((L, LANES), BF16)], axis=1),
                     preferred_element_type=F32)
        part[sb, c, d] = (sv[:, :v.shape[1]], sv[:, v.shape[1]:], jnp.exp(inter - m_t), jnp.exp(-m_t),
                          jnp.exp(b_last + m_in[sb, c, d] - m_out[sb, c, d]), kw.astype(BF16),
                          jnp.sum(kw, axis=0, keepdims=True))

    for sb, c, d in units:
        sv, s_sum, w_inter, den_floor, w_c, kw, kw_sum = part[sb, c, d]
        rows = rows_of(sb, c)
        q = q_ref[rows, :]
        c_old = c_ref[sb, d]
        n_old = n_ref[sb, d]
        num = sv + w_inter * jnp.dot(q, c_old.astype(BF16), preferred_element_type=F32)
        den = s_sum + w_inter * jnp.sum(q.astype(F32) * n_old, axis=1, keepdims=True)
        inv = 1.0 / jnp.maximum(jnp.abs(den), den_floor)
        (hf_ref, hb_ref)[d][rows, :] = num * jnp.tile(inv, (1, num.shape[1] // LANES))
        c_ref[sb, d] = w_c * c_old + lax.dot_general(kw, v_ref[rows, :], (((0,), (0,)), ((), ())),
                                                     preferred_element_type=F32)
        n_ref[sb, d] = w_c * n_old + kw_sum

    def finish(c, _):
        rows = pl.ds(pl.multiple_of(c * L, L), L)
        hs = hf_ref[rows, :] + hb_ref[rows, :]
        y_ref[rows, :] = (_rms(hs) * ng_ref[...] * _sigmoid(o_ref[rows, :])).astype(y_ref.dtype)
        return 0

    lax.fori_loop(0, nsub * nc, finish, 0)

    if want_state:
        co_ref[...] = c_ref[...]
        no_ref[...] = n_ref[...]
        for sb in range(nsub):
            for d in range(2):
                mo_ref[sb, d:d + 1, :] = jnp.broadcast_to(m_state[sb][d], (1, LANES))


def _mlstm_core(qkv, og, gate_b, norm_g, state, y_prev, state_prev, *, layer, n_layers, row0, batch, seq, nsub,
                heads, dk, dv, want_state):
    L = math.gcd(MLSTM_L, seq)
    rb = nsub * seq
    assert row0 % rb == 0 and batch % nsub == 0
    b0 = row0 // rb
    hk = heads * dk
    zero_init = state is None
    state_specs = [pl.BlockSpec((nsub, None, 2, None, dk, dv), lambda b, h: (b, layer, 0, h, 0, 0)),
                   pl.BlockSpec((nsub, None, 2, None, 1, dk), lambda b, h: (b, layer, 0, h, 0, 0)),
                   pl.BlockSpec((nsub, None, None, 2, LANES), lambda b, h: (b, layer, h, 0, 0))]
    in_specs = [pl.BlockSpec((rb, dk), lambda b, h: (b0 + b, h)),
                pl.BlockSpec((rb, dk), lambda b, h: (b0 + b, heads + h)),
                pl.BlockSpec((rb, dv), lambda b, h: (b0 + b, 2 * hk // dv + h)),
                pl.BlockSpec((rb, dv), lambda b, h: (b0 + b, h)),
                pl.BlockSpec((rb, LANES), lambda b, h: (b0 + b, heads * dv // LANES + h)),
                pl.BlockSpec((1, LANES), lambda b, h: (0, h)),
                pl.BlockSpec((1, dv), lambda b, h: (0, h))]
    args = [qkv, qkv, qkv, og, og, gate_b, norm_g.reshape(1, heads * dv)]
    if not zero_init:
        in_specs += state_specs
        args += list(state)
    aliases = {}
    if y_prev is not None:
        in_specs.append(pl.BlockSpec(memory_space=pl.ANY))
        args.append(y_prev)
        aliases[len(args) - 1] = 0
    out_specs = [pl.BlockSpec((rb, dv), lambda b, h: (b0 + b, h))]
    out_shape = [jax.ShapeDtypeStruct((qkv.shape[0], heads * dv), BF16)]
    if want_state:
        out_specs += state_specs
        out_shape += [jax.ShapeDtypeStruct((batch, n_layers, 2, heads, dk, dv), F32),
                      jax.ShapeDtypeStruct((batch, n_layers, 2, heads, 1, dk), F32),
                      jax.ShapeDtypeStruct((batch, n_layers, heads, 2, LANES), F32)]
        if state_prev is not None:
            for k_out, arr in enumerate(state_prev):
                in_specs.append(pl.BlockSpec(memory_space=pl.ANY))
                args.append(arr)
                aliases[len(args) - 1] = 1 + k_out
    return pl.pallas_call(
        functools.partial(_mlstm_kernel, seq=seq, nsub=nsub, chunk=L, zero_init=zero_init,
                          want_state=want_state, n_alias=len(aliases)),
        grid=(batch // nsub, heads),
        in_specs=in_specs,
        out_specs=out_specs,
        out_shape=out_shape,
        input_output_aliases=aliases,
        scratch_shapes=[pltpu.VMEM((rb, dv), F32), pltpu.VMEM((rb, dv), F32),
                        pltpu.VMEM((nsub, 2, dk, dv), F32), pltpu.VMEM((nsub, 2, 1, dk), F32)],
        compiler_params=_cparams("parallel", "parallel"),
        name="mlstm_core",
    )(*args)


def _dft_mats(seq, gc):
    def cs(n):
        idx = np.arange(n)
        ang = 2.0 * np.pi * ((idx[:, None] * idx[None, :]) % n) / n
        return np.cos(ang) / math.sqrt(n), np.sin(ang) / math.sqrt(n)
    ct, st = cs(seq)
    cc, sc = cs(gc)
    w_time = np.concatenate([ct, -st], axis=1)
    w_chan = np.concatenate([cc, sc], axis=1)
    return jnp.asarray(w_time, BF16), jnp.asarray(w_chan, BF16)


def _fnet_kernel(h_ref, wt_ref, wc_ref, *rest, gc):
    o_ref = rest[-1]
    y = jnp.dot(h_ref[...], wc_ref[...], preferred_element_type=F32)
    ycat = jnp.concatenate([y[:, :gc], y[:, gc:]], axis=0).astype(BF16)
    o_ref[...] = jnp.dot(wt_ref[...], ycat, preferred_element_type=F32).astype(o_ref.dtype)


def _fnet_mix(h, y_prev, *, row0, batch, seq):
    d = h.shape[1]
    gc = d // FNET_GROUPS
    b0 = row0 // seq
    w_time, w_chan = _dft_mats(seq, gc)
    in_specs = [pl.BlockSpec((seq, gc), lambda b, g: (b0 + b, g)),
                pl.BlockSpec((seq, 2 * seq), lambda b, g: (0, 0)),
                pl.BlockSpec((gc, 2 * gc), lambda b, g: (0, 0))]
    args = [h, w_time, w_chan]
    aliases = {}
    if y_prev is not None:
        in_specs.append(pl.BlockSpec(memory_space=pl.ANY))
        args.append(y_prev)
        aliases = {len(args) - 1: 0}
    return pl.pallas_call(
        functools.partial(_fnet_kernel, gc=gc),
        grid=(batch, FNET_GROUPS),
        in_specs=in_specs,
        out_specs=pl.BlockSpec((seq, gc), lambda b, g: (b0 + b, g)),
        out_shape=jax.ShapeDtypeStruct(h.shape, BF16),
        input_output_aliases=aliases,
        compiler_params=_cparams("parallel", "parallel"),
        name="fnet_mix",
    )(*args)


def _lru_kernel(*refs, seq, zero_init, want_state, has_prev):
    it = iter(refs)
    br_ref, xb_ref, cw_ref, cb_ref, gw_ref, gbias_ref, lam_ref = (next(it) for _ in range(7))
    if not zero_init:
        h0_ref = next(it)
    if has_prev:
        next(it)
    y_ref = next(it)
    if want_state:
        hl_ref = next(it)
    a_ref, u_ref, hs_ref = (next(it) for _ in range(3))
    T = seq
    S = SUBLANES
    x = xb_ref[...]
    tt = lax.broadcasted_iota(jnp.int32, x.shape, 0)
    xm1 = jnp.where(tt >= 1, pltpu.roll(x, 1, 0), 0.0)
    xp1 = jnp.where(tt < T - 1, pltpu.roll(x, T - 1, 0), 0.0)
    xp2 = jnp.where(tt < T - 2, pltpu.roll(x, T - 2, 0), 0.0)
    xc = (cw_ref[0:1, :] * xm1 + cw_ref[1:2, :] * x + cw_ref[2:3, :] * xp1 + cw_ref[3:4, :] * xp2
          + cb_ref[...])
    xcb = xc.astype(BF16)
    r8 = lax.broadcasted_iota(jnp.int32, (T // S, S, x.shape[1]), 1)
    for d in range(2):
        pre_r = jnp.dot(xcb, gw_ref[d, 0], preferred_element_type=F32) + gbias_ref[2 * d:2 * d + 1, :]
        pre_i = jnp.dot(xcb, gw_ref[d, 1], preferred_element_type=F32) + gbias_ref[2 * d + 1:2 * d + 2, :]
        log_a = LRU_C * _sigmoid(pre_r) * _log_sigmoid(lam_ref[d:d + 1, :])
        a = jnp.exp(log_a)
        u = jnp.sqrt(-jnp.tanh(log_a) * (a * a + 1.0)) * (_sigmoid(pre_i) * xc)
        a = a.reshape(T // S, S, a.shape[1])
        u = u.reshape(a.shape)
        s = 1
        while s < S:
            shift, valid = (s, r8 >= s) if d == 0 else (S - s, r8 < S - s)
            u = jnp.where(valid, a * pltpu.roll(u, shift, 1) + u, u)
            a = jnp.where(valid, a * pltpu.roll(a, shift, 1), a)
            s *= 2
        a_ref[d] = a.reshape(x.shape)
        u_ref[d] = u.reshape(x.shape)

    if zero_init:
        h_init = (jnp.zeros((1, x.shape[1]), F32), jnp.zeros((1, x.shape[1]), F32))
    else:
        h_init = (h0_ref[0:1, :], h0_ref[1:2, :])
    groups = T // S

    def group(i, carry):
        c_f, c_b = carry
        rf = pl.ds(pl.multiple_of(i * S, S), S)
        h_f = a_ref[0, rf, :] * c_f + u_ref[0, rf, :]
        hs_ref[0, rf, :] = h_f
        rb = pl.ds(pl.multiple_of((groups - 1 - i) * S, S), S)
        h_b = a_ref[1, rb, :] * c_b + u_ref[1, rb, :]
        hs_ref[1, rb, :] = h_b
        return h_f[S - 1:S, :], h_b[0:1, :]

    h_f, h_b = lax.fori_loop(0, groups, group, h_init, unroll=4)
    y_ref[...] = ((hs_ref[0] + hs_ref[1]) * _gelu_tanh(br_ref[...])).astype(y_ref.dtype)
    if want_state:
        hl_ref[0:1, :] = h_f
        hl_ref[1:2, :] = h_b


def _lru_core(gx, conv_w, conv_b, gate_w, gate_b, lam, h0, y_prev, *, row0, batch, seq, want_state):
    width = conv_w.shape[1]
    nb = gate_w.shape[2]
    bs = width // nb
    b0 = row0 // seq
    zero_init = h0 is None
    in_specs = [pl.BlockSpec((seq, bs), lambda b, n: (b0 + b, n)),
                pl.BlockSpec((seq, bs), lambda b, n: (b0 + b, nb + n)),
                pl.BlockSpec((conv_w.shape[0], bs), lambda b, n: (0, n)),
                pl.BlockSpec((1, bs), lambda b, n: (0, n)),
                pl.BlockSpec((2, 2, None, bs, bs), lambda b, n: (0, 0, n, 0, 0)),
                pl.BlockSpec((4, bs), lambda b, n: (0, n)),
                pl.BlockSpec((2, bs), lambda b, n: (0, n))]
    args = [gx, gx, conv_w, conv_b.reshape(1, width), gate_w, gate_b.reshape(4, width), lam]
    if not zero_init:
        in_specs.append(pl.BlockSpec((None, 2, bs), lambda b, n: (b, 0, n)))
        args.append(h0)
    aliases = {}
    if y_prev is not None:
        in_specs.append(pl.BlockSpec(memory_space=pl.ANY))
        args.append(y_prev)
        aliases = {len(args) - 1: 0}
    out_specs = [pl.BlockSpec((seq, bs), lambda b, n: (b0 + b, n))]
    out_shape = [jax.ShapeDtypeStruct((gx.shape[0], width), BF16)]
    if want_state:
        out_specs.append(pl.BlockSpec((None, 2, bs), lambda b, n: (b, 0, n)))
        out_shape.append(jax.ShapeDtypeStruct((batch, 2, width), F32))
    return pl.pallas_call(
        functools.partial(_lru_kernel, seq=seq, zero_init=zero_init, want_state=want_state,
                          has_prev=y_prev is not None),
        grid=(batch, nb),
        in_specs=in_specs,
        out_specs=out_specs,
        out_shape=out_shape,
        input_output_aliases=aliases,
        scratch_shapes=[pltpu.VMEM((2, seq, bs), F32)] * 3,
        compiler_params=_cparams("parallel", "parallel"),
        name="lru_core",
    )(*args)


def _mlstm_weights(w_in, b_gate, heads, dk, dv):
    nl, d, _ = w_in.shape
    hk, hv = heads * dk, heads * dv
    wq, wk, wv, wo, wg = jnp.split(w_in, [hk, 2 * hk, 2 * hk + hv, 2 * hk + 2 * hv], axis=2)
    w_qkv = jnp.concatenate([wq, wk * (dk ** -0.5), wv], axis=2).astype(BF16)
    wg = wg.reshape(nl, d, 2, 2, heads).transpose(0, 1, 4, 2, 3).reshape(nl, d, heads, 4)
    wg = jnp.pad(wg, ((0, 0), (0, 0), (0, 0), (0, LANES - 4))).reshape(nl, d, heads * LANES)
    w_og = jnp.concatenate([wo, wg], axis=2).astype(BF16)
    bg = b_gate.astype(F32).transpose(0, 3, 1, 2).reshape(nl, heads, 4)
    bg = jnp.pad(bg, ((0, 0), (0, 0), (0, LANES - 4))).reshape(nl, 1, heads * LANES)
    return w_qkv, w_og, bg


def kernel(x_prompt, x_sample, c, state_mlstm_C, state_mlstm_n, state_mlstm_m, state_lru_h, c_ctx,
           mod_w, mod_b, norm_g, ffn_w_up, ffn_w_down, mlstm_w_in, mlstm_b_gate, mlstm_norm_g, mlstm_w_out,
           fnet_w_out, fnet_b_out, lru_w_in, lru_conv_w, lru_conv_b, lru_gate_w, lru_gate_b, lru_lambda,
           lru_w_out):
    bp, tp, d = x_prompt.shape
    bs_, ts, _ = x_sample.shape
    depth = mod_w.shape[0]
    heads = mlstm_b_gate.shape[-1]
    dk, dv = state_mlstm_C.shape[-2:]
    rows_ctx = bp * tp
    rows_all = rows_ctx + bs_ * ts
    rows = dict(rows_all=rows_all, rows_ctx=rows_ctx, rows_per_sample=ts)
    assert 1 + bs_ <= N_COND
    tm = math.gcd(512, math.gcd(rows_ctx, ts))
    tm_ffn = math.gcd(1024, math.gcd(rows_ctx, ts))

    x_parts = [(x_prompt.reshape(rows_ctx, d), 0), (x_sample.reshape(bs_ * ts, d), rows_ctx)]
    cond = jnp.zeros((N_COND, d), F32).at[0].set(c_ctx).at[1:1 + bs_].set(c)
    mods_all = _modulation(cond, mod_w, mod_b).reshape(depth, N_COND, 6, d)
    zero_bias = jnp.zeros((d,), F32)
    trunks = ((0, bp, tp, True), (rows_ctx, bs_, ts, False))

    n_ml = mlstm_w_in.shape[0]
    w_qkv, w_og, bg = _mlstm_weights(mlstm_w_in, mlstm_b_gate, heads, dk, dv)
    mlstm_state_in = (state_mlstm_C, state_mlstm_n.reshape(bs_, n_ml, 2, heads, 1, dk),
                      jnp.broadcast_to(state_mlstm_m.transpose(0, 1, 3, 2)[..., None], (bs_, n_ml, heads, 2, LANES)))
    w_out_of = (mlstm_w_out.astype(BF16), fnet_w_out.astype(BF16), lru_w_out.astype(BF16))
    lru_w_in_b, lru_gate_w_b = lru_w_in.astype(BF16), lru_gate_w.astype(BF16)
    d_ff = ffn_w_up.shape[2]
    tf = math.gcd(d_ff, 512)
    ffn_w_up_b = ffn_w_up.reshape(depth, d, d_ff // tf, tf).transpose(0, 2, 1, 3).astype(BF16)
    ffn_w_down_b = ffn_w_down.astype(BF16)

    mlstm_state_out, new_h = None, []
    for i in range(depth):
        mods = mods_all[i]
        kind, j = i % 3, i // 3
        b_out = zero_bias
        if kind == 0:
            pm = functools.partial(_norm_mod_matmul, x_parts, norm_g[i, 0], mods, sh=0, sc=1, tm=tm, **rows)
            qkv = pm(w_qkv, j, out_dtype=BF16)
            og = pm(w_og, j, out_dtype=F32)
            y = None
            for row0, nb, seq, is_ctx in trunks:
                nsub = math.gcd(nb, max(1, ts // seq))
                res = _mlstm_core(qkv, og, bg[j], mlstm_norm_g[j], None if is_ctx else mlstm_state_in, y,
                                  mlstm_state_out if is_ctx else None, layer=j, n_layers=n_ml, row0=row0,
                                  batch=nb, seq=seq, nsub=nsub, heads=heads, dk=dk, dv=dv, want_state=is_ctx)
                y = res[0]
                if is_ctx:
                    mlstm_state_out = res[1:]
        elif kind == 1:
            h = _norm_mod_only(x_parts[0][0], norm_g[i, 0], mods, sh=0, sc=1, tm=tm, **rows)
            y = None
            for row0, nb, seq, _ in trunks:
                y = _fnet_mix(h, y, row0=row0, batch=nb, seq=seq)
            b_out = fnet_b_out[j]
        else:
            gx = _norm_mod_matmul(x_parts, norm_g[i, 0], mods, lru_w_in_b, j, sh=0, sc=1, out_dtype=F32, tm=tm,
                                  **rows)
            y = None
            for row0, nb, seq, is_ctx in trunks:
                res = _lru_core(gx, lru_conv_w[j], lru_conv_b[j], lru_gate_w_b[j], lru_gate_b[j],
                                lru_lambda[j], None if is_ctx else state_lru_h[:, j], y,
                                row0=row0, batch=nb, seq=seq, want_state=is_ctx)
                y = res[0]
                if is_ctx:
                    new_h.append(res[1])
        x = _matmul_residual(y, w_out_of[kind], j, b_out, x_parts, norm_g[i, 1], mods, gate=2, tm=tm, **rows)
        ffn = functools.partial(_ffn, x, norm_g[i, 2], norm_g[i, 3], mods, ffn_w_up_b, ffn_w_down_b, i,
                                tm=tm_ffn, rows_ctx=rows_ctx, rows_per_sample=ts)
        if i < depth - 1:
            x_parts = [(ffn(row0=0, nrows=rows_all), 0)]
        else:
            y_prompt = ffn(row0=0, nrows=rows_ctx).reshape(bp, tp, d)
            y_sample = ffn(row0=rows_ctx, nrows=rows_all - rows_ctx).reshape(bs_, ts, d)

    new_c, new_n, new_m = mlstm_state_out
    return (y_prompt, y_sample, new_c, new_n.reshape(bp, n_ml, 2, heads, dk),
            new_m[..., 0].transpose(0, 1, 3, 2), jnp.stack(new_h, 1))
```

```python
import functools
import math

import numpy as np
import jax
import jax.numpy as jnp
from jax import lax
from jax.experimental import pallas as pl
from jax.experimental.pallas import tpu as pltpu

F32 = jnp.float32
BF16 = jnp.bfloat16
EPS = 1e-6
FNET_GROUPS = 8
LRU_C = 8.0
LANES = 128
MLSTM_L = 256
SUBLANES = 8
VMEM_LIMIT = 56 * 1024 * 1024
N_COND = 16


def _cparams(*sem):
    return pltpu.CompilerParams(dimension_semantics=sem, vmem_limit_bytes=VMEM_LIMIT)


def _sigmoid(x):
    return 0.5 * jnp.tanh(0.5 * x) + 0.5


def _log_sigmoid(x):
    return jnp.minimum(x, 0.0) - jnp.log1p(jnp.exp(-jnp.abs(x)))


def _gelu_tanh(x):
    return 0.5 * x * (1.0 + jnp.tanh(math.sqrt(2.0 / math.pi) * (x + 0.044715 * (x * x * x))))


def _rms(x):
    return x * lax.rsqrt(jnp.mean(x * x, axis=-1, keepdims=True) + EPS)


def _norm_mod(x, g, shift, scale):
    return (_rms(x) * g) * (1.0 + scale) + shift


def _cond_of_block(i, tm, rows_ctx, rows_per_sample):
    r = i * tm
    return jnp.where(r < rows_ctx, 0, 1 + (r - rows_ctx) // rows_per_sample)


def _mod_kernel(c_ref, w_ref, b_ref, o_ref):
    @pl.when(pl.program_id(1) == 0)
    def _():
        o_ref[...] = jnp.broadcast_to(b_ref[...], o_ref.shape)

    c = c_ref[...]
    a = (c * _sigmoid(c)).astype(BF16)
    o_ref[...] += jnp.dot(a, w_ref[...].astype(BF16), preferred_element_type=F32)


def _modulation(cond, mod_w, mod_b):
    depth, d, n = mod_w.shape
    tk = math.gcd(d, 2 * LANES)
    return pl.pallas_call(
        _mod_kernel,
        grid=(depth, d // tk),
        in_specs=[pl.BlockSpec((N_COND, tk), lambda l, k: (0, k)),
                  pl.BlockSpec((None, tk, n), lambda l, k: (l, k, 0)),
                  pl.BlockSpec((None, 1, n), lambda l, k: (l, 0, 0))],
        out_specs=pl.BlockSpec((None, N_COND, n), lambda l, k: (l, 0, 0)),
        out_shape=jax.ShapeDtypeStruct((depth, N_COND, n), F32),
        compiler_params=_cparams("parallel", "arbitrary"),
        name="modulation",
    )(cond, mod_w, mod_b.reshape(depth, 1, n))


def _pm_kernel(x_ref, g_ref, mod_ref, w_ref, *rest, sh, sc):
    o_ref = rest[-1]
    h = _norm_mod(x_ref[...], g_ref[...], mod_ref[sh:sh + 1, :], mod_ref[sc:sc + 1, :]).astype(BF16)
    o_ref[...] = jnp.dot(h, w_ref[...], preferred_element_type=F32).astype(o_ref.dtype)


def _norm_mod_matmul(x_parts, g, mods, w, layer, *, sh, sc, out_dtype, tm, rows_all, rows_ctx, rows_per_sample):
    d, n = w.shape[1:]
    cond = functools.partial(_cond_of_block, tm=tm, rows_ctx=rows_ctx, rows_per_sample=rows_per_sample)
    out = None
    for x, row0 in x_parts:
        i0 = row0 // tm
        in_specs = [pl.BlockSpec((tm, d), lambda i: (i, 0)),
                    pl.BlockSpec((1, d), lambda i: (0, 0)),
                    pl.BlockSpec((None, 6, d), lambda i, i0=i0: (cond(i0 + i), 0, 0)),
                    pl.BlockSpec((None, d, n), lambda i: (layer, 0, 0), pipeline_mode=pl.Buffered(1))]
        args = [x, g.reshape(1, d), mods, w]
        if out is not None:
            in_specs.append(pl.BlockSpec(memory_space=pl.ANY))
            args.append(out)
        out = pl.pallas_call(
            functools.partial(_pm_kernel, sh=sh, sc=sc),
            grid=(x.shape[0] // tm,),
            in_specs=in_specs,
            out_specs=pl.BlockSpec((tm, n), lambda i, i0=i0: (i0 + i, 0)),
            out_shape=jax.ShapeDtypeStruct((rows_all, n), out_dtype),
            input_output_aliases={4: 0} if len(args) == 5 else {},
            compiler_params=_cparams("parallel"),
            name="norm_mod_matmul",
        )(*args)
    return out


def _me_kernel(y_ref, w_ref, b_ref, x_ref, g_ref, mod_ref, *rest, gate):
    o_ref = rest[-1]
    out = jnp.dot(y_ref[...], w_ref[...], preferred_element_type=F32) + b_ref[...]
    o_ref[...] = x_ref[...] + mod_ref[gate:gate + 1, :] * (_rms(out) * g_ref[...])


def _matmul_residual(y, w, layer, b, x_parts, g, mods, *, gate, tm, rows_all, rows_ctx, rows_per_sample):
    k, d = w.shape[1:]
    cond = functools.partial(_cond_of_block, tm=tm, rows_ctx=rows_ctx, rows_per_sample=rows_per_sample)
    out = None
    for x, row0 in x_parts:
        i0 = row0 // tm
        in_specs = [pl.BlockSpec((tm, k), lambda i, i0=i0: (i0 + i, 0)),
                    pl.BlockSpec((None, k, d), lambda i: (layer, 0, 0)),
                    pl.BlockSpec((1, d), lambda i: (0, 0)),
                    pl.BlockSpec((tm, d), lambda i: (i, 0)),
                    pl.BlockSpec((1, d), lambda i: (0, 0)),
                    pl.BlockSpec((None, 6, d), lambda i, i0=i0: (cond(i0 + i), 0, 0))]
        args = [y, w, b.reshape(1, d), x, g.reshape(1, d), mods]
        if out is not None:
            in_specs.append(pl.BlockSpec(memory_space=pl.ANY))
            args.append(out)
        out = pl.pallas_call(
            functools.partial(_me_kernel, gate=gate),
            grid=(x.shape[0] // tm,),
            in_specs=in_specs,
            out_specs=pl.BlockSpec((tm, d), lambda i, i0=i0: (i0 + i, 0)),
            out_shape=jax.ShapeDtypeStruct((rows_all, d), F32),
            input_output_aliases={6: 0} if len(args) == 7 else {},
            compiler_params=_cparams("parallel"),
            name="matmul_residual",
        )(*args)
    return out


def _ffn_kernel(x_ref, g1_ref, g2_ref, mod_ref, wu_ref, wd_ref, o_ref, h_ref):
    j = pl.program_id(1)

    @pl.when(j == 0)
    def _():
        h_ref[...] = _norm_mod(x_ref[...], g1_ref[...], mod_ref[3:4, :], mod_ref[4:5, :]).astype(BF16)
        o_ref[...] = jnp.zeros_like(o_ref)

    u = jnp.maximum(jnp.dot(h_ref[...], wu_ref[...], preferred_element_type=F32), 0.0)
    o_ref[...] += jnp.dot((u * u).astype(BF16), wd_ref[...], preferred_element_type=F32)

    @pl.when(j == pl.num_programs(1) - 1)
    def _():
        o_ref[...] = x_ref[...] + mod_ref[5:6, :] * (_rms(o_ref[...]) * g2_ref[...])


def _ffn(x, g_pre, g_post, mods, w_up, w_down, layer, *, row0, nrows, tm, rows_ctx, rows_per_sample):
    d = x.shape[1]
    _, nf, _, tf = w_up.shape
    i0 = row0 // tm
    cond = functools.partial(_cond_of_block, tm=tm, rows_ctx=rows_ctx, rows_per_sample=rows_per_sample)
    return pl.pallas_call(
        _ffn_kernel,
        grid=(nrows // tm, nf),
        in_specs=[pl.BlockSpec((tm, d), lambda i, j: (i0 + i, 0)),
                  pl.BlockSpec((1, d), lambda i, j: (0, 0)),
                  pl.BlockSpec((1, d), lambda i, j: (0, 0)),
                  pl.BlockSpec((None, 6, d), lambda i, j: (cond(i0 + i), 0, 0)),
                  pl.BlockSpec((None, None, d, tf), lambda i, j: (layer, j, 0, 0)),
                  pl.BlockSpec((None, tf, d), lambda i, j: (layer, j, 0))],
        out_specs=pl.BlockSpec((tm, d), lambda i, j: (i, 0)),
        out_shape=jax.ShapeDtypeStruct((nrows, d), F32),
        scratch_shapes=[pltpu.VMEM((tm, d), BF16)],
        compiler_params=_cparams("parallel", "arbitrary"),
        name="ffn",
    )(x, g_pre.reshape(1, d), g_post.reshape(1, d), mods, w_up, w_down)


def _mlstm_kernel(*refs, seq, nsub, chunk, zero_init, want_state, n_alias):
    it = iter(refs)
    q_ref, k_ref, v_ref, o_ref, gt_ref, gb_ref, ng_ref = (next(it) for _ in range(7))
    if not zero_init:
        c0_ref, n0_ref, m0_ref = next(it), next(it), next(it)
    for _ in range(n_alias):
        next(it)
    y_ref = next(it)
    if want_state:
        co_ref, no_ref, mo_ref = next(it), next(it), next(it)
    hf_ref, hb_ref, c_ref, n_ref = (next(it) for _ in range(4))
    L = chunk
    nc = seq // L

    if zero_init:
        c_ref[...] = jnp.zeros_like(c_ref)
        n_ref[...] = jnp.zeros_like(n_ref)
        m_state = [[jnp.zeros((1, 1), F32), jnp.zeros((1, 1), F32)] for _ in range(nsub)]
    else:
        c_ref[...] = c0_ref[...]
        n_ref[...] = n0_ref[...]
        m_state = [[m0_ref[sb, 0:1, 0:1], m0_ref[sb, 1:2, 0:1]] for sb in range(nsub)]

    row = lax.broadcasted_iota(jnp.int32, (L, L), 0)
    col = lax.broadcasted_iota(jnp.int32, (L, L), 1)
    masks = (col <= row, col >= row)
    lane = lax.broadcasted_iota(jnp.int32, (L, LANES), 1)
    is_forget = (lane % 2) == 1

    def rows_of(sb, c):
        return pl.ds(sb * seq + c * L, L)

    scan_order = (tuple(range(nc)), tuple(range(nc - 1, -1, -1)))
    units = [(sb, c, d) for step in range(nc) for sb in range(nsub) for d, c in ((0, step), (1, nc - 1 - step))]

    def log_gates(sb, c):
        gts = gt_ref[rows_of(sb, c), :] + gb_ref[...]
        x = jnp.where(is_forget, _log_sigmoid(gts), gts)
        def split3(z, axis):
            z_hi = z.astype(BF16)
            r_1 = z - z_hi.astype(F32)
            z_mid = r_1.astype(BF16)
            return jnp.concatenate([z_hi, z_mid, (r_1 - z_mid.astype(F32)).astype(BF16)], axis=axis)

        xt = x.T
        return x, xt, split3(x, 1), split3(xt, 0)

    chunk_gates = {(sb, c): log_gates(sb, c) for sb in range(nsub) for c in range(nc)}
    gate = {}
    for sb, c, d in units:
        x, xt, x3, xt3 = chunk_gates[sb, c]
        y3 = jnp.dot(masks[d].astype(BF16), x3, preferred_element_type=F32)
        y = (y3[:, :LANES] + y3[:, LANES:2 * LANES]) + y3[:, 2 * LANES:]
        yt3 = jnp.dot(xt3, masks[1 - d].astype(BF16), preferred_element_type=F32)
        yt = (yt3[:LANES] + yt3[LANES:2 * LANES]) + yt3[2 * LANES:]
        b_c = y[:, 2 * d + 1:2 * d + 2]
        a_c = x[:, 2 * d:2 * d + 1] - b_c
        a_r = xt[2 * d:2 * d + 1, :] - yt[2 * d + 1:2 * d + 2, :]
        am = jnp.where(masks[d], jnp.broadcast_to(a_r, (L, L)), -jnp.inf)
        b_last = b_c[L - 1:L, :] if d == 0 else b_c[0:1, :]
        g_c = b_last + a_c
        gate[sb, c, d] = (am, b_c + jnp.max(am, axis=1, keepdims=True), b_c, b_last, g_c,
                          jnp.max(g_c, axis=0, keepdims=True))

    m_in, m_out = {}, {}
    for sb in range(nsub):
        for d in range(2):
            m = m_state[sb][d]
            for c in scan_order[d]:
                _, _, _, b_last, _, g_max = gate[sb, c, d]
                m_in[sb, c, d] = m
                m = jnp.maximum(b_last + m, g_max)
                m_out[sb, c, d] = m
            m_state[sb][d] = m

    part = {}
    for sb, c, d in units:
        am, r_max, b_c, b_last, g_c, _ = gate[sb, c, d]
        rows = rows_of(sb, c)
        q = q_ref[rows, :]
        k = k_ref[rows, :]
        v = v_ref[rows, :]
        inter = b_c + m_in[sb, c, d]
        m_t = jnp.maximum(inter, r_max)
        s = (lax.dot_general(q, k, (((1,), (1,)), ((), ())), preferred_element_type=F32)
             * jnp.exp(am + (b_c - m_t)))
        kw = k.astype(F32) * jnp.exp(g_c - m_out[sb, c, d])
        sv = jnp.dot(s.astype(BF16), jnp.concatenate([v, jnp.ones((L, LANES), BF16)], axis=1),
                     preferred_element_type=F32)
        part[sb, c, d] = (sv[:, :v.shape[1]], sv[:, v.shape[1]:], jnp.exp(inter - m_t), jnp.exp(-m_t),
                          jnp.exp(b_last + m_in[sb, c, d] - m_out[sb, c, d]), kw.astype(BF16),
                          jnp.sum(kw, axis=0, keepdims=True))

    for sb, c, d in units:
        sv, s_sum, w_inter, den_floor, w_c, kw, kw_sum = part[sb, c, d]
        rows = rows_of(sb, c)
        q = q_ref[rows, :]
        c_old = c_ref[sb, d]
        n_old = n_ref[sb, d]
        num = sv + w_inter * jnp.dot(q, c_old.astype(BF16), preferred_element_type=F32)
        den = s_sum + w_inter * jnp.sum(q.astype(F32) * n_old, axis=1, keepdims=True)
        inv = 1.0 / jnp.maximum(jnp.abs(den), den_floor)
        (hf_ref, hb_ref)[d][rows, :] = num * jnp.tile(inv, (1, num.shape[1] // LANES))
        c_ref[sb, d] = w_c * c_old + lax.dot_general(kw, v_ref[rows, :], (((0,), (0,)), ((), ())),
                                                     preferred_element_type=F32)
        n_ref[sb, d] = w_c * n_old + kw_sum

    def finish(c, _):
        rows = pl.ds(pl.multiple_of(c * L, L), L)
        hs = hf_ref[rows, :] + hb_ref[rows, :]
        y_ref[rows, :] = (_rms(hs) * ng_ref[...] * _sigmoid(o_ref[rows, :])).astype(y_ref.dtype)
        return 0

    lax.fori_loop(0, nsub * nc, finish, 0)

    if want_state:
        co_ref[...] = c_ref[...]
        no_ref[...] = n_ref[...]
        for sb in range(nsub):
            for d in range(2):
                mo_ref[sb, d:d + 1, :] = jnp.broadcast_to(m_state[sb][d], (1, LANES))


def _mlstm_core(qkv, og, gate_b, norm_g, state, y_prev, state_prev, *, layer, n_layers, row0, batch, seq, nsub,
                heads, dk, dv, want_state):
    L = math.gcd(MLSTM_L, seq)
    rb = nsub * seq
    assert row0 % rb == 0 and batch % nsub == 0
    b0 = row0 // rb
    hk = heads * dk
    zero_init = state is None
    state_specs = [pl.BlockSpec((nsub, None, 2, None, dk, dv), lambda b, h: (b, layer, 0, h, 0, 0)),
                   pl.BlockSpec((nsub, None, 2, None, 1, dk), lambda b, h: (b, layer, 0, h, 0, 0)),
                   pl.BlockSpec((nsub, None, None, 2, LANES), lambda b, h: (b, layer, h, 0, 0))]
    in_specs = [pl.BlockSpec((rb, dk), lambda b, h: (b0 + b, h)),
                pl.BlockSpec((rb, dk), lambda b, h: (b0 + b, heads + h)),
                pl.BlockSpec((rb, dv), lambda b, h: (b0 + b, 2 * hk // dv + h)),
                pl.BlockSpec((rb, dv), lambda b, h: (b0 + b, h)),
                pl.BlockSpec((rb, LANES), lambda b, h: (b0 + b, heads * dv // LANES + h)),
                pl.BlockSpec((1, LANES), lambda b, h: (0, h)),
                pl.BlockSpec((1, dv), lambda b, h: (0, h))]
    args = [qkv, qkv, qkv, og, og, gate_b, norm_g.reshape(1, heads * dv)]
    if not zero_init:
        in_specs += state_specs
        args += list(state)
    aliases = {}
    if y_prev is not None:
        in_specs.append(pl.BlockSpec(memory_space=pl.ANY))
        args.append(y_prev)
        aliases[len(args) - 1] = 0
    out_specs = [pl.BlockSpec((rb, dv), lambda b, h: (b0 + b, h))]
    out_shape = [jax.ShapeDtypeStruct((qkv.shape[0], heads * dv), BF16)]
    if want_state:
        out_specs += state_specs
        out_shape += [jax.ShapeDtypeStruct((batch, n_layers, 2, heads, dk, dv), F32),
                      jax.ShapeDtypeStruct((batch, n_layers, 2, heads, 1, dk), F32),
                      jax.ShapeDtypeStruct((batch, n_layers, heads, 2, LANES), F32)]
        if state_prev is not None:
            for k_out, arr in enumerate(state_prev):
                in_specs.append(pl.BlockSpec(memory_space=pl.ANY))
                args.append(arr)
                aliases[len(args) - 1] = 1 + k_out
    return pl.pallas_call(
        functools.partial(_mlstm_kernel, seq=seq, nsub=nsub, chunk=L, zero_init=zero_init,
                          want_state=want_state, n_alias=len(aliases)),
        grid=(batch // nsub, heads),
        in_specs=in_specs,
        out_specs=out_specs,
        out_shape=out_shape,
        input_output_aliases=aliases,
        scratch_shapes=[pltpu.VMEM((rb, dv), F32), pltpu.VMEM((rb, dv), F32),
                        pltpu.VMEM((nsub, 2, dk, dv), F32), pltpu.VMEM((nsub, 2, 1, dk), F32)],
        compiler_params=_cparams("parallel", "parallel"),
        name="mlstm_core",
    )(*args)


def _dft_mats(seq, gc):
    def cs(n):
        idx = np.arange(n)
        ang = 2.0 * np.pi * ((idx[:, None] * idx[None, :]) % n) / n
        return np.cos(ang) / math.sqrt(n), np.sin(ang) / math.sqrt(n)
    ct, st = cs(seq)
    cc, sc = cs(gc)
    w_time = np.concatenate([ct, -st], axis=1)
    w_chan = np.concatenate([cc, sc], axis=1)
    return jnp.asarray(w_time, BF16), jnp.asarray(w_chan, BF16)


def _fnet_kernel(x_ref, g_ref, mod_ref, wt_ref, wc_ref, *rest, seq, nsub, gc):
    o_ref = rest[-1]
    h = _norm_mod(x_ref[...], g_ref[...], mod_ref[0:1, :], mod_ref[1:2, :]).astype(BF16)
    for grp in range(h.shape[1] // gc):
        cols = slice(grp * gc, (grp + 1) * gc)
        y = jnp.dot(h[:, cols], wc_ref[...], preferred_element_type=F32)
        for sb in range(nsub):
            ys = y[sb * seq:(sb + 1) * seq]
            ycat = jnp.concatenate([ys[:, :gc], ys[:, gc:]], axis=0).astype(BF16)
            o_ref[sb * seq:(sb + 1) * seq, cols] = jnp.dot(wt_ref[...], ycat,
                                                           preferred_element_type=F32).astype(o_ref.dtype)


def _fnet_mix(x, g, mods, y_prev, *, row0, batch, seq, nsub, rows_all, rows_ctx, rows_per_sample):
    d = x.shape[1]
    gc = d // FNET_GROUPS
    rb = nsub * seq
    assert row0 % rb == 0 and batch % nsub == 0 and x.shape[0] == rows_all
    b0 = row0 // rb
    cond = functools.partial(_cond_of_block, tm=rb, rows_ctx=rows_ctx, rows_per_sample=rows_per_sample)
    w_time, w_chan = _dft_mats(seq, gc)
    in_specs = [pl.BlockSpec((rb, d), lambda b: (b0 + b, 0)),
                pl.BlockSpec((1, d), lambda b: (0, 0)),
                pl.BlockSpec((None, 6, d), lambda b: (cond(b0 + b), 0, 0)),
                pl.BlockSpec((seq, 2 * seq), lambda b: (0, 0)),
                pl.BlockSpec((gc, 2 * gc), lambda b: (0, 0))]
    args = [x, g.reshape(1, d), mods, w_time, w_chan]
    aliases = {}
    if y_prev is not None:
        in_specs.append(pl.BlockSpec(memory_space=pl.ANY))
        args.append(y_prev)
        aliases = {len(args) - 1: 0}
    return pl.pallas_call(
        functools.partial(_fnet_kernel, seq=seq, nsub=nsub, gc=gc),
        grid=(batch // nsub,),
        in_specs=in_specs,
        out_specs=pl.BlockSpec((rb, d), lambda b: (b0 + b, 0)),
        out_shape=jax.ShapeDtypeStruct(x.shape, BF16),
        input_output_aliases=aliases,
        compiler_params=_cparams("parallel"),
        name="fnet_mix",
    )(*args)


def _lru_kernel(*refs, seq, zero_init, want_state, has_prev):
    it = iter(refs)
    br_ref, xb_ref, cw_ref, cb_ref, gw_ref, gbias_ref, lam_ref = (next(it) for _ in range(7))
    if not zero_init:
        h0_ref = next(it)
    if has_prev:
        next(it)
    y_ref = next(it)
    if want_state:
        hl_ref = next(it)
    a_ref, u_ref, hs_ref = (next(it) for _ in range(3))
    T = seq
    S = SUBLANES
    x = xb_ref[...]
    tt = lax.broadcasted_iota(jnp.int32, x.shape, 0)
    xm1 = jnp.where(tt >= 1, pltpu.roll(x, 1, 0), 0.0)
    xp1 = jnp.where(tt < T - 1, pltpu.roll(x, T - 1, 0), 0.0)
    xp2 = jnp.where(tt < T - 2, pltpu.roll(x, T - 2, 0), 0.0)
    xc = (cw_ref[0:1, :] * xm1 + cw_ref[1:2, :] * x + cw_ref[2:3, :] * xp1 + cw_ref[3:4, :] * xp2
          + cb_ref[...])
    xcb = xc.astype(BF16)
    r8 = lax.broadcasted_iota(jnp.int32, (T // S, S, x.shape[1]), 1)
    for d in range(2):
        pre_r = jnp.dot(xcb, gw_ref[d, 0], preferred_element_type=F32) + gbias_ref[2 * d:2 * d + 1, :]
        pre_i = jnp.dot(xcb, gw_ref[d, 1], preferred_element_type=F32) + gbias_ref[2 * d + 1:2 * d + 2, :]
        log_a = LRU_C * _sigmoid(pre_r) * _log_sigmoid(lam_ref[d:d + 1, :])
        a = jnp.exp(log_a)
        u = jnp.sqrt(-jnp.tanh(log_a) * (a * a + 1.0)) * (_sigmoid(pre_i) * xc)
        a = a.reshape(T // S, S, a.shape[1])
        u = u.reshape(a.shape)
        s = 1
        while s < S:
            shift, valid = (s, r8 >= s) if d == 0 else (S - s, r8 < S - s)
            u = jnp.where(valid, a * pltpu.roll(u, shift, 1) + u, u)
            a = jnp.where(valid, a * pltpu.roll(a, shift, 1), a)
            s *= 2
        a_ref[d] = a.reshape(x.shape)
        u_ref[d] = u.reshape(x.shape)

    if zero_init:
        h_init = (jnp.zeros((1, x.shape[1]), F32), jnp.zeros((1, x.shape[1]), F32))
    else:
        h_init = (h0_ref[0:1, :], h0_ref[1:2, :])
    groups = T // S

    def group(i, carry):
        c_f, c_b = carry
        rf = pl.ds(pl.multiple_of(i * S, S), S)
        h_f = a_ref[0, rf, :] * c_f + u_ref[0, rf, :]
        hs_ref[0, rf, :] = h_f
        rb = pl.ds(pl.multiple_of((groups - 1 - i) * S, S), S)
        h_b = a_ref[1, rb, :] * c_b + u_ref[1, rb, :]
        hs_ref[1, rb, :] = h_b
        return h_f[S - 1:S, :], h_b[0:1, :]

    h_f, h_b = lax.fori_loop(0, groups, group, h_init, unroll=4)
    y_ref[...] = ((hs_ref[0] + hs_ref[1]) * _gelu_tanh(br_ref[...])).astype(y_ref.dtype)
    if want_state:
        hl_ref[0:1, :] = h_f
        hl_ref[1:2, :] = h_b


def _lru_core(gx, conv_w, conv_b, gate_w, gate_b, lam, h0, y_prev, *, row0, batch, seq, want_state):
    width = conv_w.shape[1]
    nb = gate_w.shape[2]
    bs = width // nb
    b0 = row0 // seq
    zero_init = h0 is None
    in_specs = [pl.BlockSpec((seq, bs), lambda b, n: (b0 + b, n)),
                pl.BlockSpec((seq, bs), lambda b, n: (b0 + b, nb + n)),
                pl.BlockSpec((conv_w.shape[0], bs), lambda b, n: (0, n)),
                pl.BlockSpec((1, bs), lambda b, n: (0, n)),
                pl.BlockSpec((2, 2, None, bs, bs), lambda b, n: (0, 0, n, 0, 0)),
                pl.BlockSpec((4, bs), lambda b, n: (0, n)),
                pl.BlockSpec((2, bs), lambda b, n: (0, n))]
    args = [gx, gx, conv_w, conv_b.reshape(1, width), gate_w, gate_b.reshape(4, width), lam]
    if not zero_init:
        in_specs.append(pl.BlockSpec((None, 2, bs), lambda b, n: (b, 0, n)))
        args.append(h0)
    aliases = {}
    if y_prev is not None:
        in_specs.append(pl.BlockSpec(memory_space=pl.ANY))
        args.append(y_prev)
        aliases = {len(args) - 1: 0}
    out_specs = [pl.BlockSpec((seq, bs), lambda b, n: (b0 + b, n))]
    out_shape = [jax.ShapeDtypeStruct((gx.shape[0], width), BF16)]
    if want_state:
        out_specs.append(pl.BlockSpec((None, 2, bs), lambda b, n: (b, 0, n)))
        out_shape.append(jax.ShapeDtypeStruct((batch, 2, width), F32))
    return pl.pallas_call(
        functools.partial(_lru_kernel, seq=seq, zero_init=zero_init, want_state=want_state,
                          has_prev=y_prev is not None),
        grid=(batch, nb),
        in_specs=in_specs,
        out_specs=out_specs,
        out_shape=out_shape,
        input_output_aliases=aliases,
        scratch_shapes=[pltpu.VMEM((2, seq, bs), F32)] * 3,
        compiler_params=_cparams("parallel", "parallel"),
        name="lru_core",
    )(*args)


def _mlstm_weights(w_in, b_gate, heads, dk, dv):
    nl, d, _ = w_in.shape
    hk, hv = heads * dk, heads * dv
    wq, wk, wv, wo, wg = jnp.split(w_in, [hk, 2 * hk, 2 * hk + hv, 2 * hk + 2 * hv], axis=2)
    w_qkv = jnp.concatenate([wq, wk * (dk ** -0.5), wv], axis=2).astype(BF16)
    wg = wg.reshape(nl, d, 2, 2, heads).transpose(0, 1, 4, 2, 3).reshape(nl, d, heads, 4)
    wg = jnp.pad(wg, ((0, 0), (0, 0), (0, 0), (0, LANES - 4))).reshape(nl, d, heads * LANES)
    w_og = jnp.concatenate([wo, wg], axis=2).astype(BF16)
    bg = b_gate.astype(F32).transpose(0, 3, 1, 2).reshape(nl, heads, 4)
    bg = jnp.pad(bg, ((0, 0), (0, 0), (0, LANES - 4))).reshape(nl, 1, heads * LANES)
    return w_qkv, w_og, bg


def kernel(x_prompt, x_sample, c, state_mlstm_C, state_mlstm_n, state_mlstm_m, state_lru_h, c_ctx,
           mod_w, mod_b, norm_g, ffn_w_up, ffn_w_down, mlstm_w_in, mlstm_b_gate, mlstm_norm_g, mlstm_w_out,
           fnet_w_out, fnet_b_out, lru_w_in, lru_conv_w, lru_conv_b, lru_gate_w, lru_gate_b, lru_lambda,
           lru_w_out):
    bp, tp, d = x_prompt.shape
    bs_, ts, _ = x_sample.shape
    depth = mod_w.shape[0]
    heads = mlstm_b_gate.shape[-1]
    dk, dv = state_mlstm_C.shape[-2:]
    rows_ctx = bp * tp
    rows_all = rows_ctx + bs_ * ts
    rows = dict(rows_all=rows_all, rows_ctx=rows_ctx, rows_per_sample=ts)
    assert 1 + bs_ <= N_COND
    tm = math.gcd(512, math.gcd(rows_ctx, ts))
    tm_ffn = math.gcd(1024, math.gcd(rows_ctx, ts))

    x_parts = [(x_prompt.reshape(rows_ctx, d), 0), (x_sample.reshape(bs_ * ts, d), rows_ctx)]
    cond = jnp.zeros((N_COND, d), F32).at[0].set(c_ctx).at[1:1 + bs_].set(c)
    mods_all = _modulation(cond, mod_w, mod_b).reshape(depth, N_COND, 6, d)
    zero_bias = jnp.zeros((d,), F32)
    trunks = ((0, bp, tp, True), (rows_ctx, bs_, ts, False))

    n_ml = mlstm_w_in.shape[0]
    w_qkv, w_og, bg = _mlstm_weights(mlstm_w_in, mlstm_b_gate, heads, dk, dv)
    mlstm_state_in = (state_mlstm_C, state_mlstm_n.reshape(bs_, n_ml, 2, heads, 1, dk),
                      jnp.broadcast_to(state_mlstm_m.transpose(0, 1, 3, 2)[..., None], (bs_, n_ml, heads, 2, LANES)))
    w_out_of = (mlstm_w_out.astype(BF16), fnet_w_out.astype(BF16), lru_w_out.astype(BF16))
    lru_w_in_b, lru_gate_w_b = lru_w_in.astype(BF16), lru_gate_w.astype(BF16)
    d_ff = ffn_w_up.shape[2]
    tf = math.gcd(d_ff, 512)
    ffn_w_up_b = ffn_w_up.reshape(depth, d, d_ff // tf, tf).transpose(0, 2, 1, 3).astype(BF16)
    ffn_w_down_b = ffn_w_down.astype(BF16)

    mlstm_state_out, new_h = None, []
    for i in range(depth):
        mods = mods_all[i]
        kind, j = i % 3, i // 3
        b_out = zero_bias
        if kind == 0:
            pm = functools.partial(_norm_mod_matmul, x_parts, norm_g[i, 0], mods, sh=0, sc=1, tm=tm, **rows)
            qkv = pm(w_qkv, j, out_dtype=BF16)
            og = pm(w_og, j, out_dtype=F32)
            y = None
            for row0, nb, seq, is_ctx in trunks:
                nsub = math.gcd(nb, max(1, ts // seq))
                res = _mlstm_core(qkv, og, bg[j], mlstm_norm_g[j], None if is_ctx else mlstm_state_in, y,
                                  mlstm_state_out if is_ctx else None, layer=j, n_layers=n_ml, row0=row0,
                                  batch=nb, seq=seq, nsub=nsub, heads=heads, dk=dk, dv=dv, want_state=is_ctx)
                y = res[0]
                if is_ctx:
                    mlstm_state_out = res[1:]
        elif kind == 1:
            y = None
            for row0, nb, seq, _ in trunks:
                y = _fnet_mix(x_parts[0][0], norm_g[i, 0], mods, y, row0=row0, batch=nb, seq=seq,
                              nsub=math.gcd(nb, max(1, ts // seq)), **rows)
            b_out = fnet_b_out[j]
        else:
            gx = _norm_mod_matmul(x_parts, norm_g[i, 0], mods, lru_w_in_b, j, sh=0, sc=1, out_dtype=F32, tm=tm,
                                  **rows)
            y = None
            for row0, nb, seq, is_ctx in trunks:
                res = _lru_core(gx, lru_conv_w[j], lru_conv_b[j], lru_gate_w_b[j], lru_gate_b[j],
                                lru_lambda[j], None if is_ctx else state_lru_h[:, j], y,
                                row0=row0, batch=nb, seq=seq, want_state=is_ctx)
                y = res[0]
                if is_ctx:
                    new_h.append(res[1])
        x = _matmul_residual(y, w_out_of[kind], j, b_out, x_parts, norm_g[i, 1], mods, gate=2, tm=tm, **rows)
        ffn = functools.partial(_ffn, x, norm_g[i, 2], norm_g[i, 3], mods, ffn_w_up_b, ffn_w_down_b, i,
                                tm=tm_ffn, rows_ctx=rows_ctx, rows_per_sample=ts)
        if i < depth - 1:
            x_parts = [(ffn(row0=0, nrows=rows_all), 0)]
        else:
            y_prompt = ffn(row0=0, nrows=rows_ctx).reshape(bp, tp, d)
            y_sample = ffn(row0=rows_ctx, nrows=rows_all - rows_ctx).reshape(bs_, ts, d)

    new_c, new_n, new_m = mlstm_state_out
    return (y_prompt, y_sample, new_c, new_n.reshape(bp, n_ml, 2, heads, dk),
            new_m[..., 0].transpose(0, 1, 3, 2), jnp.stack(new_h, 1))
```

```python
import functools
import math

import numpy as np
import jax
import jax.numpy as jnp
from jax import lax
from jax.experimental import pallas as pl
from jax.experimental.pallas import tpu as pltpu

F32 = jnp.float32
BF16 = jnp.bfloat16
EPS = 1e-6
FNET_GROUPS = 8
LRU_C = 8.0
LANES = 128
MLSTM_L = 256
SUBLANES = 8
VMEM_LIMIT = 56 * 1024 * 1024
N_COND = 16


def _cparams(*sem):
    return pltpu.CompilerParams(dimension_semantics=sem, vmem_limit_bytes=VMEM_LIMIT)


def _sigmoid(x):
    return 0.5 * jnp.tanh(0.5 * x) + 0.5


def _log_sigmoid(x):
    return jnp.minimum(x, 0.0) - jnp.log1p(jnp.exp(-jnp.abs(x)))


def _gelu_tanh(x):
    return 0.5 * x * (1.0 + jnp.tanh(math.sqrt(2.0 / math.pi) * (x + 0.044715 * (x * x * x))))


def _rms(x):
    return x * lax.rsqrt(jnp.mean(x * x, axis=-1, keepdims=True) + EPS)


def _norm_mod(x, g, shift, scale):
    return (_rms(x) * g) * (1.0 + scale) + shift


def _cond_of_block(i, tm, rows_ctx, rows_per_sample):
    r = i * tm
    return jnp.where(r < rows_ctx, 0, 1 + (r - rows_ctx) // rows_per_sample)


def _mod_kernel(c_ref, w_ref, b_ref, o_ref):
    @pl.when(pl.program_id(1) == 0)
    def _():
        o_ref[...] = jnp.broadcast_to(b_ref[...], o_ref.shape)

    c = c_ref[...]
    a = (c * _sigmoid(c)).astype(BF16)
    o_ref[...] += jnp.dot(a, w_ref[...].astype(BF16), preferred_element_type=F32)


def _modulation(cond, mod_w, mod_b):
    depth, d, n = mod_w.shape
    tk = math.gcd(d, 2 * LANES)
    return pl.pallas_call(
        _mod_kernel,
        grid=(depth, d // tk),
        in_specs=[pl.BlockSpec((N_COND, tk), lambda l, k: (0, k)),
                  pl.BlockSpec((None, tk, n), lambda l, k: (l, k, 0)),
                  pl.BlockSpec((None, 1, n), lambda l, k: (l, 0, 0))],
        out_specs=pl.BlockSpec((None, N_COND, n), lambda l, k: (l, 0, 0)),
        out_shape=jax.ShapeDtypeStruct((depth, N_COND, n), F32),
        compiler_params=_cparams("parallel", "arbitrary"),
        name="modulation",
    )(cond, mod_w, mod_b.reshape(depth, 1, n))


def _piece_specs(x_parts, tm, d):
    specs = []
    for x, row0 in x_parts:
        i0, nblk = row0 // tm, x.shape[0] // tm
        specs.append(pl.BlockSpec((tm, d), lambda i, i0=i0, nblk=nblk: (jnp.clip(i - i0, 0, nblk - 1), 0)))
    return specs


def _for_piece(x_refs, first_blocks, fn):
    if len(x_refs) == 1:
        fn(x_refs[0])
        return
    i = pl.program_id(0)
    bounds = list(first_blocks[1:]) + [None]
    for x_ref, lo, hi in zip(x_refs, first_blocks, bounds):
        pl.when((i >= lo) if hi is None else ((i >= lo) & (i < hi)))(functools.partial(fn, x_ref))


def _pm_kernel(*refs, sh, sc, first_blocks):
    n_x = len(first_blocks)
    g_ref, mod_ref, w_ref, o_ref = refs[n_x:]

    def project(x_ref):
        h = _norm_mod(x_ref[...], g_ref[...], mod_ref[sh:sh + 1, :], mod_ref[sc:sc + 1, :]).astype(BF16)
        o_ref[...] = jnp.dot(h, w_ref[...], preferred_element_type=F32).astype(o_ref.dtype)

    _for_piece(refs[:n_x], first_blocks, project)


def _norm_mod_matmul(x_parts, g, mods, w, layer, *, sh, sc, out_dtype, tm, rows_all, rows_ctx, rows_per_sample):
    d, n = w.shape[1:]
    cond = functools.partial(_cond_of_block, tm=tm, rows_ctx=rows_ctx, rows_per_sample=rows_per_sample)
    return pl.pallas_call(
        functools.partial(_pm_kernel, sh=sh, sc=sc, first_blocks=tuple(r // tm for _, r in x_parts)),
        grid=(rows_all // tm,),
        in_specs=_piece_specs(x_parts, tm, d) + [
            pl.BlockSpec((1, d), lambda i: (0, 0)),
            pl.BlockSpec((None, 6, d), lambda i: (cond(i), 0, 0)),
            pl.BlockSpec((None, d, n), lambda i: (layer, 0, 0), pipeline_mode=pl.Buffered(1))],
        out_specs=pl.BlockSpec((tm, n), lambda i: (i, 0)),
        out_shape=jax.ShapeDtypeStruct((rows_all, n), out_dtype),
        compiler_params=_cparams("parallel"),
        name="norm_mod_matmul",
    )(*[x for x, _ in x_parts], g.reshape(1, d), mods, w)


def _me_kernel(*refs, gate, first_blocks):
    n_x = len(first_blocks)
    y_ref, w_ref, b_ref, g_ref, mod_ref, o_ref = refs[n_x:]
    out = jnp.dot(y_ref[...], w_ref[...], preferred_element_type=F32) + b_ref[...]
    update = mod_ref[gate:gate + 1, :] * (_rms(out) * g_ref[...])

    def residual(x_ref):
        o_ref[...] = x_ref[...] + update

    _for_piece(refs[:n_x], first_blocks, residual)


def _matmul_residual(y, w, layer, b, x_parts, g, mods, *, gate, tm, rows_all, rows_ctx, rows_per_sample):
    k, d = w.shape[1:]
    cond = functools.partial(_cond_of_block, tm=tm, rows_ctx=rows_ctx, rows_per_sample=rows_per_sample)
    return pl.pallas_call(
        functools.partial(_me_kernel, gate=gate, first_blocks=tuple(r // tm for _, r in x_parts)),
        grid=(rows_all // tm,),
        in_specs=_piece_specs(x_parts, tm, d) + [
            pl.BlockSpec((tm, k), lambda i: (i, 0)),
            pl.BlockSpec((None, k, d), lambda i: (layer, 0, 0)),
            pl.BlockSpec((1, d), lambda i: (0, 0)),
            pl.BlockSpec((1, d), lambda i: (0, 0)),
            pl.BlockSpec((None, 6, d), lambda i: (cond(i), 0, 0))],
        out_specs=pl.BlockSpec((tm, d), lambda i: (i, 0)),
        out_shape=jax.ShapeDtypeStruct((rows_all, d), F32),
        compiler_params=_cparams("parallel"),
        name="matmul_residual",
    )(*[x for x, _ in x_parts], y, w, b.reshape(1, d), g.reshape(1, d), mods)


def _ffn_kernel(x_ref, g1_ref, g2_ref, mod_ref, wu_ref, wd_ref, o_ref, h_ref):
    j = pl.program_id(1)

    @pl.when(j == 0)
    def _():
        h_ref[...] = _norm_mod(x_ref[...], g1_ref[...], mod_ref[3:4, :], mod_ref[4:5, :]).astype(BF16)
        o_ref[...] = jnp.zeros_like(o_ref)

    u = jnp.maximum(jnp.dot(h_ref[...], wu_ref[...], preferred_element_type=F32), 0.0)
    o_ref[...] += jnp.dot((u * u).astype(BF16), wd_ref[...], preferred_element_type=F32)

    @pl.when(j == pl.num_programs(1) - 1)
    def _():
        o_ref[...] = x_ref[...] + mod_ref[5:6, :] * (_rms(o_ref[...]) * g2_ref[...])


def _ffn(x, g_pre, g_post, mods, w_up, w_down, layer, *, row0, nrows, tm, rows_ctx, rows_per_sample):
    d = x.shape[1]
    _, nf, _, tf = w_up.shape
    i0 = row0 // tm
    cond = functools.partial(_cond_of_block, tm=tm, rows_ctx=rows_ctx, rows_per_sample=rows_per_sample)
    return pl.pallas_call(
        _ffn_kernel,
        grid=(nrows // tm, nf),
        in_specs=[pl.BlockSpec((tm, d), lambda i, j: (i0 + i, 0)),
                  pl.BlockSpec((1, d), lambda i, j: (0, 0)),
                  pl.BlockSpec((1, d), lambda i, j: (0, 0)),
                  pl.BlockSpec((None, 6, d), lambda i, j: (cond(i0 + i), 0, 0)),
                  pl.BlockSpec((None, None, d, tf), lambda i, j: (layer, j, 0, 0)),
                  pl.BlockSpec((None, tf, d), lambda i, j: (layer, j, 0))],
        out_specs=pl.BlockSpec((tm, d), lambda i, j: (i, 0)),
        out_shape=jax.ShapeDtypeStruct((nrows, d), F32),
        scratch_shapes=[pltpu.VMEM((tm, d), BF16)],
        compiler_params=_cparams("parallel", "arbitrary"),
        name="ffn",
    )(x, g_pre.reshape(1, d), g_post.reshape(1, d), mods, w_up, w_down)


def _mlstm_kernel(*refs, seq, nsub, chunk, zero_init, want_state, n_alias, all_slots, layer):
    it = iter(refs)
    q_ref, k_ref, v_ref, o_ref, gt_ref, gb_ref, ng_ref = (next(it) for _ in range(7))
    if not zero_init:
        c0_ref, n0_ref, m0_ref = next(it), next(it), next(it)
    for _ in range(n_alias):
        next(it)
    y_ref = next(it)
    if want_state:
        co_ref, no_ref, mo_ref = next(it), next(it), next(it)
    hf_ref, hb_ref, c_ref, n_ref = (next(it) for _ in range(4))
    L = chunk
    nc = seq // L

    if zero_init:
        c_ref[...] = jnp.zeros_like(c_ref)
        n_ref[...] = jnp.zeros_like(n_ref)
        m_state = [[jnp.zeros((1, 1), F32), jnp.zeros((1, 1), F32)] for _ in range(nsub)]
    else:
        c_ref[...] = c0_ref[...]
        n_ref[...] = n0_ref[...]
        m_state = [[m0_ref[sb, 0:1, 0:1], m0_ref[sb, 1:2, 0:1]] for sb in range(nsub)]

    row = lax.broadcasted_iota(jnp.int32, (L, L), 0)
    col = lax.broadcasted_iota(jnp.int32, (L, L), 1)
    masks = (col <= row, col >= row)
    lane = lax.broadcasted_iota(jnp.int32, (L, LANES), 1)
    is_forget = (lane % 2) == 1

    def rows_of(sb, c):
        return pl.ds(sb * seq + c * L, L)

    scan_order = (tuple(range(nc)), tuple(range(nc - 1, -1, -1)))
    units = [(sb, c, d) for step in range(nc) for sb in range(nsub) for d, c in ((0, step), (1, nc - 1 - step))]

    def log_gates(sb, c):
        gts = gt_ref[rows_of(sb, c), :] + gb_ref[...]
        x = jnp.where(is_forget, _log_sigmoid(gts), gts)
        def split3(z, axis):
            z_hi = z.astype(BF16)
            r_1 = z - z_hi.astype(F32)
            z_mid = r_1.astype(BF16)
            return jnp.concatenate([z_hi, z_mid, (r_1 - z_mid.astype(F32)).astype(BF16)], axis=axis)

        xt = x.T
        return x, xt, split3(x, 1), split3(xt, 0)

    chunk_gates = {(sb, c): log_gates(sb, c) for sb in range(nsub) for c in range(nc)}
    gate = {}
    for sb, c, d in units:
        x, xt, x3, xt3 = chunk_gates[sb, c]
        y3 = jnp.dot(masks[d].astype(BF16), x3, preferred_element_type=F32)
        y = (y3[:, :LANES] + y3[:, LANES:2 * LANES]) + y3[:, 2 * LANES:]
        yt3 = jnp.dot(xt3, masks[1 - d].astype(BF16), preferred_element_type=F32)
        yt = (yt3[:LANES] + yt3[LANES:2 * LANES]) + yt3[2 * LANES:]
        b_c = y[:, 2 * d + 1:2 * d + 2]
        a_c = x[:, 2 * d:2 * d + 1] - b_c
        a_r = xt[2 * d:2 * d + 1, :] - yt[2 * d + 1:2 * d + 2, :]
        am = jnp.where(masks[d], jnp.broadcast_to(a_r, (L, L)), -jnp.inf)
        b_last = b_c[L - 1:L, :] if d == 0 else b_c[0:1, :]
        g_c = b_last + a_c
        gate[sb, c, d] = (am, b_c + jnp.max(am, axis=1, keepdims=True), b_c, b_last, g_c,
                          jnp.max(g_c, axis=0, keepdims=True))

    m_in, m_out = {}, {}
    for sb in range(nsub):
        for d in range(2):
            m = m_state[sb][d]
            for c in scan_order[d]:
                _, _, _, b_last, _, g_max = gate[sb, c, d]
                m_in[sb, c, d] = m
                m = jnp.maximum(b_last + m, g_max)
                m_out[sb, c, d] = m
            m_state[sb][d] = m

    part = {}
    for sb, c, d in units:
        am, r_max, b_c, b_last, g_c, _ = gate[sb, c, d]
        rows = rows_of(sb, c)
        q = q_ref[rows, :]
        k = k_ref[rows, :]
        v = v_ref[rows, :]
        inter = b_c + m_in[sb, c, d]
        m_t = jnp.maximum(inter, r_max)
        s = (lax.dot_general(q, k, (((1,), (1,)), ((), ())), preferred_element_type=F32)
             * jnp.exp(am + (b_c - m_t)))
        kw = k.astype(F32) * jnp.exp(g_c - m_out[sb, c, d])
        sv = jnp.dot(s.astype(BF16), jnp.concatenate([v, jnp.ones((L, LANES), BF16)], axis=1),
                     preferred_element_type=F32)
        part[sb, c, d] = (sv[:, :v.shape[1]], sv[:, v.shape[1]:], jnp.exp(inter - m_t), jnp.exp(-m_t),
                          jnp.exp(b_last + m_in[sb, c, d] - m_out[sb, c, d]), kw.astype(BF16),
                          jnp.sum(kw, axis=0, keepdims=True))

    for sb, c, d in units:
        sv, s_sum, w_inter, den_floor, w_c, kw, kw_sum = part[sb, c, d]
        rows = rows_of(sb, c)
        q = q_ref[rows, :]
        c_old = c_ref[sb, d]
        n_old = n_ref[sb, d]
        num = sv + w_inter * jnp.dot(q, c_old.astype(BF16), preferred_element_type=F32)
        den = s_sum + w_inter * jnp.sum(q.astype(F32) * n_old, axis=1, keepdims=True)
        inv = 1.0 / jnp.maximum(jnp.abs(den), den_floor)
        (hf_ref, hb_ref)[d][rows, :] = num * jnp.tile(inv, (1, num.shape[1] // LANES))
        c_ref[sb, d] = w_c * c_old + lax.dot_general(kw, v_ref[rows, :], (((0,), (0,)), ((), ())),
                                                     preferred_element_type=F32)
        n_ref[sb, d] = w_c * n_old + kw_sum

    def finish(c, _):
        rows = pl.ds(pl.multiple_of(c * L, L), L)
        hs = hf_ref[rows, :] + hb_ref[rows, :]
        y_ref[rows, :] = (_rms(hs) * ng_ref[...] * _sigmoid(o_ref[rows, :])).astype(y_ref.dtype)
        return 0

    lax.fori_loop(0, nsub * nc, finish, 0)

    if want_state:
        if all_slots:
            for ref in (co_ref, no_ref, mo_ref):
                ref[...] = jnp.zeros_like(ref)
            co_ref, no_ref, mo_ref = co_ref.at[:, layer], no_ref.at[:, layer], mo_ref.at[:, layer]
        co_ref[...] = c_ref[...]
        no_ref[...] = n_ref[...]
        for sb in range(nsub):
            for d in range(2):
                mo_ref[sb, d:d + 1, :] = jnp.broadcast_to(m_state[sb][d], (1, LANES))


def _mlstm_core(qkv, og, gate_b, norm_g, state, y_prev, state_prev, *, layer, n_layers, row0, batch, seq, nsub,
                heads, dk, dv, want_state):
    L = math.gcd(MLSTM_L, seq)
    rb = nsub * seq
    assert row0 % rb == 0 and batch % nsub == 0
    b0 = row0 // rb
    hk = heads * dk
    zero_init = state is None
    state_specs = [pl.BlockSpec((nsub, None, 2, None, dk, dv), lambda b, h: (b, layer, 0, h, 0, 0)),
                   pl.BlockSpec((nsub, None, 2, None, 1, dk), lambda b, h: (b, layer, 0, h, 0, 0)),
                   pl.BlockSpec((nsub, None, None, 2, LANES), lambda b, h: (b, layer, h, 0, 0))]
    in_specs = [pl.BlockSpec((rb, dk), lambda b, h: (b0 + b, h)),
                pl.BlockSpec((rb, dk), lambda b, h: (b0 + b, heads + h)),
                pl.BlockSpec((rb, dv), lambda b, h: (b0 + b, 2 * hk // dv + h)),
                pl.BlockSpec((rb, dv), lambda b, h: (b0 + b, h)),
                pl.BlockSpec((rb, LANES), lambda b, h: (b0 + b, heads * dv // LANES + h)),
                pl.BlockSpec((1, LANES), lambda b, h: (0, h)),
                pl.BlockSpec((1, dv), lambda b, h: (0, h))]
    args = [qkv, qkv, qkv, og, og, gate_b, norm_g.reshape(1, heads * dv)]
    if not zero_init:
        in_specs += state_specs
        args += list(state)
    in_specs.append(pl.BlockSpec(memory_space=pl.ANY))
    args.append(y_prev)
    aliases = {len(args) - 1: 0}
    out_specs = [pl.BlockSpec((rb, dv), lambda b, h: (b0 + b, h))]
    out_shape = [jax.ShapeDtypeStruct(y_prev.shape, y_prev.dtype)]
    all_slots = want_state and state_prev is None
    if want_state:
        if all_slots:
            out_specs += [pl.BlockSpec((nsub, n_layers, 2, None, dk, dv), lambda b, h: (b, 0, 0, h, 0, 0)),
                          pl.BlockSpec((nsub, n_layers, 2, None, 1, dk), lambda b, h: (b, 0, 0, h, 0, 0)),
                          pl.BlockSpec((nsub, n_layers, None, 2, LANES), lambda b, h: (b, 0, h, 0, 0))]
        else:
            out_specs += state_specs
        out_shape += [jax.ShapeDtypeStruct((batch, n_layers, 2, heads, dk, dv), F32),
                      jax.ShapeDtypeStruct((batch, n_layers, 2, heads, 1, dk), F32),
                      jax.ShapeDtypeStruct((batch, n_layers, heads, 2, LANES), F32)]
        if state_prev is not None:
            for k_out, arr in enumerate(state_prev):
                in_specs.append(pl.BlockSpec(memory_space=pl.ANY))
                args.append(arr)
                aliases[len(args) - 1] = 1 + k_out
    return pl.pallas_call(
        functools.partial(_mlstm_kernel, seq=seq, nsub=nsub, chunk=L, zero_init=zero_init,
                          want_state=want_state, n_alias=len(aliases), all_slots=all_slots, layer=layer),
        grid=(batch // nsub, heads),
        in_specs=in_specs,
        out_specs=out_specs,
        out_shape=out_shape,
        input_output_aliases=aliases,
        scratch_shapes=[pltpu.VMEM((rb, dv), F32), pltpu.VMEM((rb, dv), F32),
                        pltpu.VMEM((nsub, 2, dk, dv), F32), pltpu.VMEM((nsub, 2, 1, dk), F32)],
        compiler_params=_cparams("parallel", "parallel"),
        name="mlstm_core",
    )(*args)


def _dft_mats(seq, gc):
    def cs(n):
        idx = np.arange(n)
        ang = 2.0 * np.pi * ((idx[:, None] * idx[None, :]) % n) / n
        return np.cos(ang) / math.sqrt(n), np.sin(ang) / math.sqrt(n)
    ct, st = cs(seq)
    cc, sc = cs(gc)
    w_time = np.concatenate([ct, -st], axis=1)
    w_chan = np.concatenate([cc, sc], axis=1)
    return jnp.asarray(w_time, BF16), jnp.asarray(w_chan, BF16)


def _fnet_kernel(x_ref, g_ref, mod_ref, wt_ref, wc_ref, _, o_ref, *, seq, nsub, gc):
    h = _norm_mod(x_ref[...], g_ref[...], mod_ref[0:1, :], mod_ref[1:2, :]).astype(BF16)
    for grp in range(h.shape[1] // gc):
        cols = slice(grp * gc, (grp + 1) * gc)
        y = jnp.dot(h[:, cols], wc_ref[...], preferred_element_type=F32)
        for sb in range(nsub):
            ys = y[sb * seq:(sb + 1) * seq]
            ycat = jnp.concatenate([ys[:, :gc], ys[:, gc:]], axis=0).astype(BF16)
            o_ref[sb * seq:(sb + 1) * seq, cols] = jnp.dot(wt_ref[...], ycat,
                                                           preferred_element_type=F32).astype(o_ref.dtype)


def _fnet_mix(x, g, mods, y_prev, *, row0, batch, seq, nsub, rows_all, rows_ctx, rows_per_sample):
    d = x.shape[1]
    gc = d // FNET_GROUPS
    rb = nsub * seq
    assert row0 % rb == 0 and batch % nsub == 0 and x.shape[0] == rows_all
    b0 = row0 // rb
    cond = functools.partial(_cond_of_block, tm=rb, rows_ctx=rows_ctx, rows_per_sample=rows_per_sample)
    w_time, w_chan = _dft_mats(seq, gc)
    in_specs = [pl.BlockSpec((rb, d), lambda b: (b0 + b, 0)),
                pl.BlockSpec((1, d), lambda b: (0, 0)),
                pl.BlockSpec((None, 6, d), lambda b: (cond(b0 + b), 0, 0)),
                pl.BlockSpec((seq, 2 * seq), lambda b: (0, 0)),
                pl.BlockSpec((gc, 2 * gc), lambda b: (0, 0)),
                pl.BlockSpec(memory_space=pl.ANY)]
    args = [x, g.reshape(1, d), mods, w_time, w_chan, y_prev]
    return pl.pallas_call(
        functools.partial(_fnet_kernel, seq=seq, nsub=nsub, gc=gc),
        grid=(batch // nsub,),
        in_specs=in_specs,
        out_specs=pl.BlockSpec((rb, d), lambda b: (b0 + b, 0)),
        out_shape=jax.ShapeDtypeStruct(y_prev.shape, y_prev.dtype),
        input_output_aliases={len(args) - 1: 0},
        compiler_params=_cparams("parallel"),
        name="fnet_mix",
    )(*args)


def _lru_kernel(*refs, seq, zero_init, want_state):
    it = iter(refs)
    br_ref, xb_ref, cw_ref, cb_ref, gw_ref, gbias_ref, lam_ref = (next(it) for _ in range(7))
    if not zero_init:
        h0_ref = next(it)
    next(it)
    y_ref = next(it)
    if want_state:
        hl_ref = next(it)
    a_ref, u_ref, hs_ref = (next(it) for _ in range(3))
    T = seq
    S = SUBLANES
    x = xb_ref[...]
    tt = lax.broadcasted_iota(jnp.int32, x.shape, 0)
    xm1 = jnp.where(tt >= 1, pltpu.roll(x, 1, 0), 0.0)
    xp1 = jnp.where(tt < T - 1, pltpu.roll(x, T - 1, 0), 0.0)
    xp2 = jnp.where(tt < T - 2, pltpu.roll(x, T - 2, 0), 0.0)
    xc = (cw_ref[0:1, :] * xm1 + cw_ref[1:2, :] * x + cw_ref[2:3, :] * xp1 + cw_ref[3:4, :] * xp2
          + cb_ref[...])
    xcb = xc.astype(BF16)
    r8 = lax.broadcasted_iota(jnp.int32, (T // S, S, x.shape[1]), 1)
    for d in range(2):
        pre_r = jnp.dot(xcb, gw_ref[d, 0], preferred_element_type=F32) + gbias_ref[2 * d:2 * d + 1, :]
        pre_i = jnp.dot(xcb, gw_ref[d, 1], preferred_element_type=F32) + gbias_ref[2 * d + 1:2 * d + 2, :]
        log_a = LRU_C * _sigmoid(pre_r) * _log_sigmoid(lam_ref[d:d + 1, :])
        a = jnp.exp(log_a)
        u = jnp.sqrt(-jnp.tanh(log_a) * (a * a + 1.0)) * (_sigmoid(pre_i) * xc)
        a = a.reshape(T // S, S, a.shape[1])
        u = u.reshape(a.shape)
        s = 1
        while s < S:
            shift, valid = (s, r8 >= s) if d == 0 else (S - s, r8 < S - s)
            u = jnp.where(valid, a * pltpu.roll(u, shift, 1) + u, u)
            a = jnp.where(valid, a * pltpu.roll(a, shift, 1), a)
            s *= 2
        a_ref[d] = a.reshape(x.shape)
        u_ref[d] = u.reshape(x.shape)

    if zero_init:
        h_init = (jnp.zeros((1, x.shape[1]), F32), jnp.zeros((1, x.shape[1]), F32))
    else:
        h_init = (h0_ref[0:1, :], h0_ref[1:2, :])
    groups = T // S

    def group(i, carry):
        c_f, c_b = carry
        rf = pl.ds(pl.multiple_of(i * S, S), S)
        h_f = a_ref[0, rf, :] * c_f + u_ref[0, rf, :]
        hs_ref[0, rf, :] = h_f
        rb = pl.ds(pl.multiple_of((groups - 1 - i) * S, S), S)
        h_b = a_ref[1, rb, :] * c_b + u_ref[1, rb, :]
        hs_ref[1, rb, :] = h_b
        return h_f[S - 1:S, :], h_b[0:1, :]

    h_f, h_b = lax.fori_loop(0, groups, group, h_init, unroll=4)
    y_ref[...] = ((hs_ref[0] + hs_ref[1]) * _gelu_tanh(br_ref[...])).astype(y_ref.dtype)
    if want_state:
        hl_ref[0:1, :] = h_f
        hl_ref[1:2, :] = h_b


def _lru_core(gx, conv_w, conv_b, gate_w, gate_b, lam, h0, y_prev, *, row0, batch, seq, want_state):
    width = conv_w.shape[1]
    nb = gate_w.shape[2]
    bs = width // nb
    b0 = row0 // seq
    zero_init = h0 is None
    in_specs = [pl.BlockSpec((seq, bs), lambda b, n: (b0 + b, n)),
                pl.BlockSpec((seq, bs), lambda b, n: (b0 + b, nb + n)),
                pl.BlockSpec((conv_w.shape[0], bs), lambda b, n: (0, n)),
                pl.BlockSpec((1, bs), lambda b, n: (0, n)),
                pl.BlockSpec((2, 2, None, bs, bs), lambda b, n: (0, 0, n, 0, 0)),
                pl.BlockSpec((4, bs), lambda b, n: (0, n)),
                pl.BlockSpec((2, bs), lambda b, n: (0, n))]
    args = [gx, gx, conv_w, conv_b.reshape(1, width), gate_w, gate_b.reshape(4, width), lam]
    if not zero_init:
        in_specs.append(pl.BlockSpec((None, 2, bs), lambda b, n: (b, 0, n)))
        args.append(h0)
    in_specs.append(pl.BlockSpec(memory_space=pl.ANY))
    args.append(y_prev)
    aliases = {len(args) - 1: 0}
    out_specs = [pl.BlockSpec((seq, bs), lambda b, n: (b0 + b, n))]
    out_shape = [jax.ShapeDtypeStruct(y_prev.shape, y_prev.dtype)]
    if want_state:
        out_specs.append(pl.BlockSpec((None, 2, bs), lambda b, n: (b, 0, n)))
        out_shape.append(jax.ShapeDtypeStruct((batch, 2, width), F32))
    return pl.pallas_call(
        functools.partial(_lru_kernel, seq=seq, zero_init=zero_init, want_state=want_state),
        grid=(batch, nb),
        in_specs=in_specs,
        out_specs=out_specs,
        out_shape=out_shape,
        input_output_aliases=aliases,
        scratch_shapes=[pltpu.VMEM((2, seq, bs), F32)] * 3,
        compiler_params=_cparams("parallel", "parallel"),
        name="lru_core",
    )(*args)


def _mlstm_weights(w_in, b_gate, heads, dk, dv):
    nl, d, _ = w_in.shape
    hk, hv = heads * dk, heads * dv
    wq, wk, wv, wo, wg = jnp.split(w_in, [hk, 2 * hk, 2 * hk + hv, 2 * hk + 2 * hv], axis=2)
    w_qkv = jnp.concatenate([wq, wk * (dk ** -0.5), wv], axis=2).astype(BF16)
    wg = wg.reshape(nl, d, 2, 2, heads).transpose(0, 1, 4, 2, 3).reshape(nl, d, heads, 4)
    wg = jnp.pad(wg, ((0, 0), (0, 0), (0, 0), (0, LANES - 4))).reshape(nl, d, heads * LANES)
    w_og = jnp.concatenate([wo, wg], axis=2).astype(BF16)
    bg = b_gate.astype(F32).transpose(0, 3, 1, 2).reshape(nl, heads, 4)
    bg = jnp.pad(bg, ((0, 0), (0, 0), (0, LANES - 4))).reshape(nl, 1, heads * LANES)
    return w_qkv, w_og, bg


def kernel(x_prompt, x_sample, c, state_mlstm_C, state_mlstm_n, state_mlstm_m, state_lru_h, c_ctx,
           mod_w, mod_b, norm_g, ffn_w_up, ffn_w_down, mlstm_w_in, mlstm_b_gate, mlstm_norm_g, mlstm_w_out,
           fnet_w_out, fnet_b_out, lru_w_in, lru_conv_w, lru_conv_b, lru_gate_w, lru_gate_b, lru_lambda,
           lru_w_out):
    bp, tp, d = x_prompt.shape
    bs_, ts, _ = x_sample.shape
    depth = mod_w.shape[0]
    heads = mlstm_b_gate.shape[-1]
    dk, dv = state_mlstm_C.shape[-2:]
    rows_ctx = bp * tp
    rows_all = rows_ctx + bs_ * ts
    rows = dict(rows_all=rows_all, rows_ctx=rows_ctx, rows_per_sample=ts)
    assert 1 + bs_ <= N_COND
    tm = math.gcd(512, math.gcd(rows_ctx, ts))
    tm_ffn = math.gcd(1024, math.gcd(rows_ctx, ts))

    x_parts = [(x_prompt.reshape(rows_ctx, d), 0), (x_sample.reshape(bs_ * ts, d), rows_ctx)]
    cond = jnp.zeros((N_COND, d), F32).at[0].set(c_ctx).at[1:1 + bs_].set(c)
    mods_all = _modulation(cond, mod_w, mod_b).reshape(depth, N_COND, 6, d)
    zero_bias = jnp.zeros((d,), F32)
    trunks = ((0, bp, tp, True), (rows_ctx, bs_, ts, False))

    n_ml = mlstm_w_in.shape[0]
    w_qkv, w_og, bg = _mlstm_weights(mlstm_w_in, mlstm_b_gate, heads, dk, dv)
    mlstm_state_in = (state_mlstm_C, state_mlstm_n.reshape(bs_, n_ml, 2, heads, 1, dk),
                      jnp.broadcast_to(state_mlstm_m.transpose(0, 1, 3, 2)[..., None], (bs_, n_ml, heads, 2, LANES)))
    w_out_of = (mlstm_w_out.astype(BF16), fnet_w_out.astype(BF16), lru_w_out.astype(BF16))
    lru_w_in_b, lru_gate_w_b = lru_w_in.astype(BF16), lru_gate_w.astype(BF16)
    d_ff = ffn_w_up.shape[2]
    tf = math.gcd(d_ff, 512)
    ffn_w_up_b = ffn_w_up.reshape(depth, d, d_ff // tf, tf).transpose(0, 2, 1, 3).astype(BF16)
    ffn_w_down_b = ffn_w_down.astype(BF16)

    assert heads * dv == d and lru_conv_w.shape[2] == d
    y = jnp.zeros((rows_all, d), BF16)
    mlstm_state_out, new_h = None, []
    for i in range(depth):
        mods = mods_all[i]
        kind, j = i % 3, i // 3
        b_out = zero_bias
        if kind == 0:
            pm = functools.partial(_norm_mod_matmul, x_parts, norm_g[i, 0], mods, sh=0, sc=1, tm=tm, **rows)
            qkv = pm(w_qkv, j, out_dtype=BF16)
            og = pm(w_og, j, out_dtype=F32)
            for row0, nb, seq, is_ctx in trunks:
                nsub = math.gcd(nb, max(1, ts // seq))
                res = _mlstm_core(qkv, og, bg[j], mlstm_norm_g[j], None if is_ctx else mlstm_state_in, y,
                                  mlstm_state_out if is_ctx else None, layer=j, n_layers=n_ml, row0=row0,
                                  batch=nb, seq=seq, nsub=nsub, heads=heads, dk=dk, dv=dv, want_state=is_ctx)
                y = res[0]
                if is_ctx:
                    mlstm_state_out = res[1:]
        elif kind == 1:
            for row0, nb, seq, _ in trunks:
                y = _fnet_mix(x_parts[0][0], norm_g[i, 0], mods, y, row0=row0, batch=nb, seq=seq,
                              nsub=math.gcd(nb, max(1, ts // seq)), **rows)
            b_out = fnet_b_out[j]
        else:
            gx = _norm_mod_matmul(x_parts, norm_g[i, 0], mods, lru_w_in_b, j, sh=0, sc=1, out_dtype=F32, tm=tm,
                                  **rows)
            for row0, nb, seq, is_ctx in trunks:
                res = _lru_core(gx, lru_conv_w[j], lru_conv_b[j], lru_gate_w_b[j], lru_gate_b[j],
                                lru_lambda[j], None if is_ctx else state_lru_h[:, j], y,
                                row0=row0, batch=nb, seq=seq, want_state=is_ctx)
                y = res[0]
                if is_ctx:
                    new_h.append(res[1])
        x = _matmul_residual(y, w_out_of[kind], j, b_out, x_parts, norm_g[i, 1], mods, gate=2, tm=tm, **rows)
        ffn = functools.partial(_ffn, x, norm_g[i, 2], norm_g[i, 3], mods, ffn_w_up_b, ffn_w_down_b, i,
                                tm=tm_ffn, rows_ctx=rows_ctx, rows_per_sample=ts)
        if i < depth - 1:
            x_parts = [(ffn(row0=0, nrows=rows_all), 0)]
        else:
            y_prompt = ffn(row0=0, nrows=rows_ctx).reshape(bp, tp, d)
            y_sample = ffn(row0=rows_ctx, nrows=rows_all - rows_ctx).reshape(bs_, ts, d)

    new_c, new_n, new_m = mlstm_state_out
    return (y_prompt, y_sample, new_c, new_n.reshape(bp, n_ml, 2, heads, dk),
            new_m[..., 0].transpose(0, 1, 3, 2), jnp.stack(new_h, 1))
```

```python
import functools
import math

import numpy as np
import jax
import jax.numpy as jnp
from jax import lax
from jax.experimental import pallas as pl
from jax.experimental.pallas import tpu as pltpu

F32 = jnp.float32
BF16 = jnp.bfloat16
EPS = 1e-6
FNET_GROUPS = 8
LRU_C = 8.0
LANES = 128
MLSTM_L = 256
SUBLANES = 8
VMEM_LIMIT = 56 * 1024 * 1024
VMEM_LIMIT_FFN = 60 * 1024 * 1024
N_COND = 16


def _cparams(*sem, vmem_limit=VMEM_LIMIT):
    return pltpu.CompilerParams(dimension_semantics=sem, vmem_limit_bytes=vmem_limit)


def _sigmoid(x):
    return 0.5 * jnp.tanh(0.5 * x) + 0.5


def _log_sigmoid(x):
    return jnp.minimum(x, 0.0) - jnp.log1p(jnp.exp(-jnp.abs(x)))


def _gelu_tanh(x):
    return 0.5 * x * (1.0 + jnp.tanh(math.sqrt(2.0 / math.pi) * (x + 0.044715 * (x * x * x))))


def _rms(x):
    return x * lax.rsqrt(jnp.mean(x * x, axis=-1, keepdims=True) + EPS)


def _norm_mod(x, g, shift, scale):
    return (_rms(x) * g) * (1.0 + scale) + shift


def _cond_of_block(i, tm, rows_ctx, rows_per_sample):
    r = i * tm
    return jnp.where(r < rows_ctx, 0, 1 + (r - rows_ctx) // rows_per_sample)


def _mod_kernel(c_ref, w_ref, b_ref, o_ref):
    @pl.when(pl.program_id(1) == 0)
    def _():
        o_ref[...] = jnp.broadcast_to(b_ref[...], o_ref.shape)

    c = c_ref[...]
    a = (c * _sigmoid(c)).astype(BF16)
    o_ref[...] += jnp.dot(a, w_ref[...].astype(BF16), preferred_element_type=F32)


def _modulation(cond, mod_w, mod_b):
    depth, d, n = mod_w.shape
    tk = math.gcd(d, 2 * LANES)
    return pl.pallas_call(
        _mod_kernel,
        grid=(depth, d // tk),
        in_specs=[pl.BlockSpec((N_COND, tk), lambda l, k: (0, k)),
                  pl.BlockSpec((None, tk, n), lambda l, k: (l, k, 0)),
                  pl.BlockSpec((None, 1, n), lambda l, k: (l, 0, 0))],
        out_specs=pl.BlockSpec((None, N_COND, n), lambda l, k: (l, 0, 0)),
        out_shape=jax.ShapeDtypeStruct((depth, N_COND, n), F32),
        compiler_params=_cparams("parallel", "arbitrary"),
        name="modulation",
    )(cond, mod_w, mod_b.reshape(depth, 1, n))


def _piece_specs(x_parts, tm, d):
    specs = []
    for x, row0 in x_parts:
        i0, nblk = row0 // tm, x.shape[0] // tm
        specs.append(pl.BlockSpec((tm, d), lambda i, i0=i0, nblk=nblk: (jnp.clip(i - i0, 0, nblk - 1), 0)))
    return specs


def _for_piece(x_refs, first_blocks, fn):
    if len(x_refs) == 1:
        fn(x_refs[0])
        return
    i = pl.program_id(0)
    bounds = list(first_blocks[1:]) + [None]
    for x_ref, lo, hi in zip(x_refs, first_blocks, bounds):
        pl.when((i >= lo) if hi is None else ((i >= lo) & (i < hi)))(functools.partial(fn, x_ref))


def _pm_kernel(*refs, sh, sc, first_blocks):
    n_x = len(first_blocks)
    g_ref, mod_ref, w_ref, o_ref = refs[n_x:]

    def project(x_ref):
        h = _norm_mod(x_ref[...], g_ref[...], mod_ref[sh:sh + 1, :], mod_ref[sc:sc + 1, :]).astype(BF16)
        o_ref[...] = jnp.dot(h, w_ref[...], preferred_element_type=F32).astype(o_ref.dtype)

    _for_piece(refs[:n_x], first_blocks, project)


def _norm_mod_matmul(x_parts, g, mods, w, layer, *, sh, sc, out_dtype, tm, rows_all, rows_ctx, rows_per_sample):
    d, n = w.shape[1:]
    cond = functools.partial(_cond_of_block, tm=tm, rows_ctx=rows_ctx, rows_per_sample=rows_per_sample)
    return pl.pallas_call(
        functools.partial(_pm_kernel, sh=sh, sc=sc, first_blocks=tuple(r // tm for _, r in x_parts)),
        grid=(rows_all // tm,),
        in_specs=_piece_specs(x_parts, tm, d) + [
            pl.BlockSpec((1, d), lambda i: (0, 0)),
            pl.BlockSpec((None, 6, d), lambda i: (cond(i), 0, 0)),
            pl.BlockSpec((None, d, n), lambda i: (layer, 0, 0), pipeline_mode=pl.Buffered(1))],
        out_specs=pl.BlockSpec((tm, n), lambda i: (i, 0)),
        out_shape=jax.ShapeDtypeStruct((rows_all, n), out_dtype),
        compiler_params=_cparams("parallel"),
        name="norm_mod_matmul",
    )(*[x for x, _ in x_parts], g.reshape(1, d), mods, w)


def _me_kernel(*refs, gate, first_blocks):
    n_x = len(first_blocks)
    y_ref, w_ref, b_ref, g_ref, mod_ref, o_ref = refs[n_x:]

    def residual(x_ref):
        out = jnp.dot(y_ref[...], w_ref[...], preferred_element_type=F32) + b_ref[...]
        o_ref[...] = x_ref[...] + mod_ref[gate:gate + 1, :] * (_rms(out) * g_ref[...])

    _for_piece(refs[:n_x], first_blocks, residual)


def _matmul_residual(y, w, layer, b, x_parts, g, mods, *, gate, tm, rows_all, rows_ctx, rows_per_sample):
    k, d = w.shape[1:]
    cond = functools.partial(_cond_of_block, tm=tm, rows_ctx=rows_ctx, rows_per_sample=rows_per_sample)
    return pl.pallas_call(
        functools.partial(_me_kernel, gate=gate, first_blocks=tuple(r // tm for _, r in x_parts)),
        grid=(rows_all // tm,),
        in_specs=_piece_specs(x_parts, tm, d) + [
            pl.BlockSpec((tm, k), lambda i: (i, 0)),
            pl.BlockSpec((None, k, d), lambda i: (layer, 0, 0)),
            pl.BlockSpec((1, d), lambda i: (0, 0)),
            pl.BlockSpec((1, d), lambda i: (0, 0)),
            pl.BlockSpec((None, 6, d), lambda i: (cond(i), 0, 0))],
        out_specs=pl.BlockSpec((tm, d), lambda i: (i, 0)),
        out_shape=jax.ShapeDtypeStruct((rows_all, d), F32),
        compiler_params=_cparams("parallel"),
        name="matmul_residual",
    )(*[x for x, _ in x_parts], y, w, b.reshape(1, d), g.reshape(1, d), mods)


def _ffn_kernel(x_ref, g1_ref, g2_ref, mod_ref, wu_ref, wd_ref, o_ref, h_ref):
    j = pl.program_id(1)

    @pl.when(j == 0)
    def _():
        h_ref[...] = _norm_mod(x_ref[...], g1_ref[...], mod_ref[3:4, :], mod_ref[4:5, :]).astype(BF16)
        o_ref[...] = jnp.zeros_like(o_ref)

    u = jnp.maximum(jnp.dot(h_ref[...], wu_ref[...], preferred_element_type=F32), 0.0)
    o_ref[...] += jnp.dot((u * u).astype(BF16), wd_ref[...].astype(BF16), preferred_element_type=F32)

    @pl.when(j == pl.num_programs(1) - 1)
    def _():
        o_ref[...] = x_ref[...] + mod_ref[5:6, :] * (_rms(o_ref[...]) * g2_ref[...])


def _ffn(x, g_pre, g_post, mods, w_up, w_down, layer, *, row0, nrows, tm, rows_ctx, rows_per_sample):
    d = x.shape[1]
    _, nf, _, tf = w_up.shape
    i0 = row0 // tm
    cond = functools.partial(_cond_of_block, tm=tm, rows_ctx=rows_ctx, rows_per_sample=rows_per_sample)
    return pl.pallas_call(
        _ffn_kernel,
        grid=(nrows // tm, nf),
        in_specs=[pl.BlockSpec((tm, d), lambda i, j: (i0 + i, 0)),
                  pl.BlockSpec((1, d), lambda i, j: (0, 0)),
                  pl.BlockSpec((1, d), lambda i, j: (0, 0)),
                  pl.BlockSpec((None, 6, d), lambda i, j: (cond(i0 + i), 0, 0)),
                  pl.BlockSpec((None, None, d, tf), lambda i, j: (layer, j, 0, 0)),
                  pl.BlockSpec((None, tf, d), lambda i, j: (layer, j, 0))],
        out_specs=pl.BlockSpec((tm, d), lambda i, j: (i, 0)),
        out_shape=jax.ShapeDtypeStruct((nrows, d), F32),
        scratch_shapes=[pltpu.VMEM((tm, d), BF16)],
        compiler_params=_cparams("parallel", "arbitrary", vmem_limit=VMEM_LIMIT_FFN),
        name="ffn",
    )(x, g_pre.reshape(1, d), g_post.reshape(1, d), mods, w_up, w_down)


def _mlstm_kernel(*refs, seq, nsub, chunk, zero_init, want_state, n_alias, all_slots, layer):
    it = iter(refs)
    q_ref, k_ref, v_ref, o_ref, gt_ref, gb_ref, ng_ref = (next(it) for _ in range(7))
    if not zero_init:
        c0_ref, n0_ref, m0_ref = next(it), next(it), next(it)
    for _ in range(n_alias):
        next(it)
    y_ref = next(it)
    if want_state:
        co_ref, no_ref, mo_ref = next(it), next(it), next(it)
    hf_ref, hb_ref, c_ref, n_ref = (next(it) for _ in range(4))
    L = chunk
    nc = seq // L

    if zero_init:
        c_ref[...] = jnp.zeros_like(c_ref)
        n_ref[...] = jnp.zeros_like(n_ref)
        m_state = [[jnp.zeros((1, 1), F32), jnp.zeros((1, 1), F32)] for _ in range(nsub)]
    else:
        c_ref[...] = c0_ref[...]
        n_ref[...] = n0_ref[...]
        m_state = [[m0_ref[sb, 0:1, 0:1], m0_ref[sb, 1:2, 0:1]] for sb in range(nsub)]

    row = lax.broadcasted_iota(jnp.int32, (L, L), 0)
    col = lax.broadcasted_iota(jnp.int32, (L, L), 1)
    masks = (col <= row, col >= row)
    lane = lax.broadcasted_iota(jnp.int32, (L, LANES), 1)
    is_forget = (lane % 2) == 1

    def rows_of(sb, c):
        return pl.ds(sb * seq + c * L, L)

    scan_order = (tuple(range(nc)), tuple(range(nc - 1, -1, -1)))
    units = [(sb, c, d) for step in range(nc) for sb in range(nsub) for d, c in ((0, step), (1, nc - 1 - step))]

    def log_gates(sb, c):
        gts = gt_ref[rows_of(sb, c), :] + gb_ref[...]
        x = jnp.where(is_forget, _log_sigmoid(gts), gts)
        def split3(z, axis):
            z_hi = z.astype(BF16)
            r_1 = z - z_hi.astype(F32)
            z_mid = r_1.astype(BF16)
            return jnp.concatenate([z_hi, z_mid, (r_1 - z_mid.astype(F32)).astype(BF16)], axis=axis)

        xt = x.T
        return x, xt, split3(x, 1), split3(xt, 0)

    chunk_gates = {(sb, c): log_gates(sb, c) for sb in range(nsub) for c in range(nc)}
    gate = {}
    for sb, c, d in units:
        x, xt, x3, xt3 = chunk_gates[sb, c]
        y3 = jnp.dot(masks[d].astype(BF16), x3, preferred_element_type=F32)
        y = (y3[:, :LANES] + y3[:, LANES:2 * LANES]) + y3[:, 2 * LANES:]
        yt3 = jnp.dot(xt3, masks[1 - d].astype(BF16), preferred_element_type=F32)
        yt = (yt3[:LANES] + yt3[LANES:2 * LANES]) + yt3[2 * LANES:]
        b_c = y[:, 2 * d + 1:2 * d + 2]
        a_c = x[:, 2 * d:2 * d + 1] - b_c
        a_r = xt[2 * d:2 * d + 1, :] - yt[2 * d + 1:2 * d + 2, :]
        am = jnp.where(masks[d], jnp.broadcast_to(a_r, (L, L)), -jnp.inf)
        b_last = b_c[L - 1:L, :] if d == 0 else b_c[0:1, :]
        g_c = b_last + a_c
        gate[sb, c, d] = (am, b_c + jnp.max(am, axis=1, keepdims=True), b_c, b_last, g_c,
                          jnp.max(g_c, axis=0, keepdims=True))

    m_in, m_out = {}, {}
    for sb in range(nsub):
        for d in range(2):
            m = m_state[sb][d]
            for c in scan_order[d]:
                _, _, _, b_last, _, g_max = gate[sb, c, d]
                m_in[sb, c, d] = m
                m = jnp.maximum(b_last + m, g_max)
                m_out[sb, c, d] = m
            m_state[sb][d] = m

    part = {}
    for sb, c, d in units:
        am, r_max, b_c, b_last, g_c, _ = gate[sb, c, d]
        rows = rows_of(sb, c)
        q = q_ref[rows, :]
        k = k_ref[rows, :]
        v = v_ref[rows, :]
        inter = b_c + m_in[sb, c, d]
        m_t = jnp.maximum(inter, r_max)
        s = (lax.dot_general(q, k, (((1,), (1,)), ((), ())), preferred_element_type=F32)
             * jnp.exp(am + (b_c - m_t)))
        kw = k.astype(F32) * jnp.exp(g_c - m_out[sb, c, d])
        sv = jnp.dot(s.astype(BF16), jnp.concatenate([v, jnp.ones((L, LANES), BF16)], axis=1),
                     preferred_element_type=F32)
        part[sb, c, d] = (sv[:, :v.shape[1]], sv[:, v.shape[1]:], jnp.exp(inter - m_t), jnp.exp(-m_t),
                          jnp.exp(b_last + m_in[sb, c, d] - m_out[sb, c, d]), kw.astype(BF16),
                          jnp.sum(kw, axis=0, keepdims=True))

    for sb, c, d in units:
        sv, s_sum, w_inter, den_floor, w_c, kw, kw_sum = part[sb, c, d]
        rows = rows_of(sb, c)
        q = q_ref[rows, :]
        c_old = c_ref[sb, d]
        n_old = n_ref[sb, d]
        num = sv + w_inter * jnp.dot(q, c_old.astype(BF16), preferred_element_type=F32)
        den = s_sum + w_inter * jnp.sum(q.astype(F32) * n_old, axis=1, keepdims=True)
        inv = 1.0 / jnp.maximum(jnp.abs(den), den_floor)
        (hf_ref, hb_ref)[d][rows, :] = num * jnp.tile(inv, (1, num.shape[1] // LANES))
        c_ref[sb, d] = w_c * c_old + lax.dot_general(kw, v_ref[rows, :], (((0,), (0,)), ((), ())),
                                                     preferred_element_type=F32)
        n_ref[sb, d] = w_c * n_old + kw_sum

    def finish(c, _):
        rows = pl.ds(pl.multiple_of(c * L, L), L)
        hs = hf_ref[rows, :] + hb_ref[rows, :]
        y_ref[rows, :] = (_rms(hs) * ng_ref[...] * _sigmoid(o_ref[rows, :])).astype(y_ref.dtype)
        return 0

    lax.fori_loop(0, nsub * nc, finish, 0)

    if want_state:
        if all_slots:
            for ref in (co_ref, no_ref, mo_ref):
                ref[...] = jnp.zeros_like(ref)
            co_ref, no_ref, mo_ref = co_ref.at[:, layer], no_ref.at[:, layer], mo_ref.at[:, layer]
        co_ref[...] = c_ref[...]
        no_ref[...] = n_ref[...]
        for sb in range(nsub):
            for d in range(2):
                mo_ref[sb, d:d + 1, :] = jnp.broadcast_to(m_state[sb][d], (1, LANES))


def _mlstm_core(qkv, og, gate_b, norm_g, state, y_prev, state_prev, *, layer, n_layers, row0, batch, seq, nsub,
                heads, dk, dv, want_state):
    L = math.gcd(MLSTM_L, seq)
    rb = nsub * seq
    assert row0 % rb == 0 and batch % nsub == 0
    b0 = row0 // rb
    hk = heads * dk
    zero_init = state is None
    state_specs = [pl.BlockSpec((nsub, None, 2, None, dk, dv), lambda b, h: (b, layer, 0, h, 0, 0)),
                   pl.BlockSpec((nsub, None, 2, None, 1, dk), lambda b, h: (b, layer, 0, h, 0, 0)),
                   pl.BlockSpec((nsub, None, None, 2, LANES), lambda b, h: (b, layer, h, 0, 0))]
    in_specs = [pl.BlockSpec((rb, dk), lambda b, h: (b0 + b, h)),
                pl.BlockSpec((rb, dk), lambda b, h: (b0 + b, heads + h)),
                pl.BlockSpec((rb, dv), lambda b, h: (b0 + b, 2 * hk // dv + h)),
                pl.BlockSpec((rb, dv), lambda b, h: (b0 + b, h)),
                pl.BlockSpec((rb, LANES), lambda b, h: (b0 + b, heads * dv // LANES + h)),
                pl.BlockSpec((1, LANES), lambda b, h: (0, h)),
                pl.BlockSpec((1, dv), lambda b, h: (0, h))]
    args = [qkv, qkv, qkv, og, og, gate_b, norm_g.reshape(1, heads * dv)]
    if not zero_init:
        in_specs += state_specs
        args += list(state)
    in_specs.append(pl.BlockSpec(memory_space=pl.ANY))
    args.append(y_prev)
    aliases = {len(args) - 1: 0}
    out_specs = [pl.BlockSpec((rb, dv), lambda b, h: (b0 + b, h))]
    out_shape = [jax.ShapeDtypeStruct(y_prev.shape, y_prev.dtype)]
    all_slots = want_state and state_prev is None
    if want_state:
        if all_slots:
            out_specs += [pl.BlockSpec((nsub, n_layers, 2, None, dk, dv), lambda b, h: (b, 0, 0, h, 0, 0)),
                          pl.BlockSpec((nsub, n_layers, 2, None, 1, dk), lambda b, h: (b, 0, 0, h, 0, 0)),
                          pl.BlockSpec((nsub, n_layers, None, 2, LANES), lambda b, h: (b, 0, h, 0, 0))]
        else:
            out_specs += state_specs
        out_shape += [jax.ShapeDtypeStruct((batch, n_layers, 2, heads, dk, dv), F32),
                      jax.ShapeDtypeStruct((batch, n_layers, 2, heads, 1, dk), F32),
                      jax.ShapeDtypeStruct((batch, n_layers, heads, 2, LANES), F32)]
        if state_prev is not None:
            for k_out, arr in enumerate(state_prev):
                in_specs.append(pl.BlockSpec(memory_space=pl.ANY))
                args.append(arr)
                aliases[len(args) - 1] = 1 + k_out
    return pl.pallas_call(
        functools.partial(_mlstm_kernel, seq=seq, nsub=nsub, chunk=L, zero_init=zero_init,
                          want_state=want_state, n_alias=len(aliases), all_slots=all_slots, layer=layer),
        grid=(batch // nsub, heads),
        in_specs=in_specs,
        out_specs=out_specs,
        out_shape=out_shape,
        input_output_aliases=aliases,
        scratch_shapes=[pltpu.VMEM((rb, dv), F32), pltpu.VMEM((rb, dv), F32),
                        pltpu.VMEM((nsub, 2, dk, dv), F32), pltpu.VMEM((nsub, 2, 1, dk), F32)],
        compiler_params=_cparams("parallel", "parallel"),
        name="mlstm_core",
    )(*args)


def _dft_mats(seq, gc):
    def cs(n):
        idx = np.arange(n)
        ang = 2.0 * np.pi * ((idx[:, None] * idx[None, :]) % n) / n
        return np.cos(ang) / math.sqrt(n), np.sin(ang) / math.sqrt(n)
    ct, st = cs(seq)
    cc, sc = cs(gc)
    w_time = np.concatenate([ct, -st], axis=1)
    w_chan = np.concatenate([cc, sc], axis=1)
    return jnp.asarray(w_time, BF16), jnp.asarray(w_chan, BF16)


def _fnet_kernel(x_ref, g_ref, mod_ref, wt_ref, wc_ref, _, o_ref, *, seq, nsub, gc):
    h = _norm_mod(x_ref[...], g_ref[...], mod_ref[0:1, :], mod_ref[1:2, :]).astype(BF16)
    for grp in range(h.shape[1] // gc):
        cols = slice(grp * gc, (grp + 1) * gc)
        y = jnp.dot(h[:, cols], wc_ref[...], preferred_element_type=F32)
        for sb in range(nsub):
            ys = y[sb * seq:(sb + 1) * seq]
            ycat = jnp.concatenate([ys[:, :gc], ys[:, gc:]], axis=0).astype(BF16)
            o_ref[sb * seq:(sb + 1) * seq, cols] = jnp.dot(wt_ref[...], ycat,
                                                           preferred_element_type=F32).astype(o_ref.dtype)


def _fnet_mix(x, g, mods, y_prev, *, row0, batch, seq, nsub, rows_all, rows_ctx, rows_per_sample):
    d = x.shape[1]
    gc = d // FNET_GROUPS
    rb = nsub * seq
    assert row0 % rb == 0 and batch % nsub == 0 and x.shape[0] == rows_all
    b0 = row0 // rb
    cond = functools.partial(_cond_of_block, tm=rb, rows_ctx=rows_ctx, rows_per_sample=rows_per_sample)
    w_time, w_chan = _dft_mats(seq, gc)
    in_specs = [pl.BlockSpec((rb, d), lambda b: (b0 + b, 0)),
                pl.BlockSpec((1, d), lambda b: (0, 0)),
                pl.BlockSpec((None, 6, d), lambda b: (cond(b0 + b), 0, 0)),
                pl.BlockSpec((seq, 2 * seq), lambda b: (0, 0)),
                pl.BlockSpec((gc, 2 * gc), lambda b: (0, 0)),
                pl.BlockSpec(memory_space=pl.ANY)]
    args = [x, g.reshape(1, d), mods, w_time, w_chan, y_prev]
    return pl.pallas_call(
        functools.partial(_fnet_kernel, seq=seq, nsub=nsub, gc=gc),
        grid=(batch // nsub,),
        in_specs=in_specs,
        out_specs=pl.BlockSpec((rb, d), lambda b: (b0 + b, 0)),
        out_shape=jax.ShapeDtypeStruct(y_prev.shape, y_prev.dtype),
        input_output_aliases={len(args) - 1: 0},
        compiler_params=_cparams("parallel"),
        name="fnet_mix",
    )(*args)


def _lru_kernel(*refs, seq, zero_init, want_state):
    it = iter(refs)
    br_ref, xb_ref, cw_ref, cb_ref, gw_ref, gbias_ref, lam_ref = (next(it) for _ in range(7))
    if not zero_init:
        h0_ref = next(it)
    next(it)
    y_ref = next(it)
    if want_state:
        hl_ref = next(it)
    a_ref, u_ref, hs_ref = (next(it) for _ in range(3))
    T = seq
    S = SUBLANES
    x = xb_ref[...]
    tt = lax.broadcasted_iota(jnp.int32, x.shape, 0)
    xm1 = jnp.where(tt >= 1, pltpu.roll(x, 1, 0), 0.0)
    xp1 = jnp.where(tt < T - 1, pltpu.roll(x, T - 1, 0), 0.0)
    xp2 = jnp.where(tt < T - 2, pltpu.roll(x, T - 2, 0), 0.0)
    xc = (cw_ref[0:1, :] * xm1 + cw_ref[1:2, :] * x + cw_ref[2:3, :] * xp1 + cw_ref[3:4, :] * xp2
          + cb_ref[...])
    xcb = xc.astype(BF16)
    r8 = lax.broadcasted_iota(jnp.int32, (T // S, S, x.shape[1]), 1)
    for d in range(2):
        pre_r = jnp.dot(xcb, gw_ref[d, 0], preferred_element_type=F32) + gbias_ref[2 * d:2 * d + 1, :]
        pre_i = jnp.dot(xcb, gw_ref[d, 1], preferred_element_type=F32) + gbias_ref[2 * d + 1:2 * d + 2, :]
        log_a = LRU_C * _sigmoid(pre_r) * _log_sigmoid(lam_ref[d:d + 1, :])
        a = jnp.exp(log_a)
        u = jnp.sqrt(-jnp.tanh(log_a) * (a * a + 1.0)) * (_sigmoid(pre_i) * xc)
        a = a.reshape(T // S, S, a.shape[1])
        u = u.reshape(a.shape)
        s = 1
        while s < S:
            shift, valid = (s, r8 >= s) if d == 0 else (S - s, r8 < S - s)
            u = jnp.where(valid, a * pltpu.roll(u, shift, 1) + u, u)
            a = jnp.where(valid, a * pltpu.roll(a, shift, 1), a)
            s *= 2
        a_ref[d] = a.reshape(x.shape)
        u_ref[d] = u.reshape(x.shape)

    if zero_init:
        h_init = (jnp.zeros((1, x.shape[1]), F32), jnp.zeros((1, x.shape[1]), F32))
    else:
        h_init = (h0_ref[0:1, :], h0_ref[1:2, :])
    groups = T // S

    def group(i, carry):
        c_f, c_b = carry
        rf = pl.ds(pl.multiple_of(i * S, S), S)
        h_f = a_ref[0, rf, :] * c_f + u_ref[0, rf, :]
        hs_ref[0, rf, :] = h_f
        rb = pl.ds(pl.multiple_of((groups - 1 - i) * S, S), S)
        h_b = a_ref[1, rb, :] * c_b + u_ref[1, rb, :]
        hs_ref[1, rb, :] = h_b
        return h_f[S - 1:S, :], h_b[0:1, :]

    h_f, h_b = lax.fori_loop(0, groups, group, h_init, unroll=4)
    y_ref[...] = ((hs_ref[0] + hs_ref[1]) * _gelu_tanh(br_ref[...])).astype(y_ref.dtype)
    if want_state:
        hl_ref[0:1, :] = h_f
        hl_ref[1:2, :] = h_b


def _lru_core(gx, conv_w, conv_b, gate_w, gate_b, lam, h0, y_prev, *, row0, batch, seq, want_state):
    width = conv_w.shape[1]
    nb = gate_w.shape[2]
    bs = width // nb
    b0 = row0 // seq
    zero_init = h0 is None
    in_specs = [pl.BlockSpec((seq, bs), lambda b, n: (b0 + b, n)),
                pl.BlockSpec((seq, bs), lambda b, n: (b0 + b, nb + n)),
                pl.BlockSpec((conv_w.shape[0], bs), lambda b, n: (0, n)),
                pl.BlockSpec((1, bs), lambda b, n: (0, n)),
                pl.BlockSpec((2, 2, None, bs, bs), lambda b, n: (0, 0, n, 0, 0)),
                pl.BlockSpec((4, bs), lambda b, n: (0, n)),
                pl.BlockSpec((2, bs), lambda b, n: (0, n))]
    args = [gx, gx, conv_w, conv_b.reshape(1, width), gate_w, gate_b.reshape(4, width), lam]
    if not zero_init:
        in_specs.append(pl.BlockSpec((None, 2, bs), lambda b, n: (b, 0, n)))
        args.append(h0)
    in_specs.append(pl.BlockSpec(memory_space=pl.ANY))
    args.append(y_prev)
    aliases = {len(args) - 1: 0}
    out_specs = [pl.BlockSpec((seq, bs), lambda b, n: (b0 + b, n))]
    out_shape = [jax.ShapeDtypeStruct(y_prev.shape, y_prev.dtype)]
    if want_state:
        out_specs.append(pl.BlockSpec((None, 2, bs), lambda b, n: (b, 0, n)))
        out_shape.append(jax.ShapeDtypeStruct((batch, 2, width), F32))
    return pl.pallas_call(
        functools.partial(_lru_kernel, seq=seq, zero_init=zero_init, want_state=want_state),
        grid=(batch, nb),
        in_specs=in_specs,
        out_specs=out_specs,
        out_shape=out_shape,
        input_output_aliases=aliases,
        scratch_shapes=[pltpu.VMEM((2, seq, bs), F32)] * 3,
        compiler_params=_cparams("parallel", "parallel"),
        name="lru_core",
    )(*args)


def _mlstm_weights(w_in, b_gate, heads, dk, dv):
    nl, d, _ = w_in.shape
    hk, hv = heads * dk, heads * dv
    wq, wk, wv, wo, wg = jnp.split(w_in, [hk, 2 * hk, 2 * hk + hv, 2 * hk + 2 * hv], axis=2)
    w_qkv = jnp.concatenate([wq, wk * (dk ** -0.5), wv], axis=2).astype(BF16)
    wg = wg.reshape(nl, d, 2, 2, heads).transpose(0, 1, 4, 2, 3).reshape(nl, d, heads, 4)
    wg = jnp.pad(wg, ((0, 0), (0, 0), (0, 0), (0, LANES - 4))).reshape(nl, d, heads * LANES)
    w_og = jnp.concatenate([wo, wg], axis=2).astype(BF16)
    bg = b_gate.astype(F32).transpose(0, 3, 1, 2).reshape(nl, heads, 4)
    bg = jnp.pad(bg, ((0, 0), (0, 0), (0, LANES - 4))).reshape(nl, 1, heads * LANES)
    return w_qkv, w_og, bg


def kernel(x_prompt, x_sample, c, state_mlstm_C, state_mlstm_n, state_mlstm_m, state_lru_h, c_ctx,
           mod_w, mod_b, norm_g, ffn_w_up, ffn_w_down, mlstm_w_in, mlstm_b_gate, mlstm_norm_g, mlstm_w_out,
           fnet_w_out, fnet_b_out, lru_w_in, lru_conv_w, lru_conv_b, lru_gate_w, lru_gate_b, lru_lambda,
           lru_w_out):
    bp, tp, d = x_prompt.shape
    bs_, ts, _ = x_sample.shape
    depth = mod_w.shape[0]
    heads = mlstm_b_gate.shape[-1]
    dk, dv = state_mlstm_C.shape[-2:]
    rows_ctx = bp * tp
    rows_all = rows_ctx + bs_ * ts
    rows = dict(rows_all=rows_all, rows_ctx=rows_ctx, rows_per_sample=ts)
    assert 1 + bs_ <= N_COND
    tm = math.gcd(512, math.gcd(rows_ctx, ts))
    tm_ffn = math.gcd(1024, math.gcd(rows_ctx, ts))

    x_parts = [(x_prompt.reshape(rows_ctx, d), 0), (x_sample.reshape(bs_ * ts, d), rows_ctx)]
    cond = jnp.zeros((N_COND, d), F32).at[0].set(c_ctx).at[1:1 + bs_].set(c)
    mods_all = _modulation(cond, mod_w, mod_b).reshape(depth, N_COND, 6, d)
    zero_bias = jnp.zeros((d,), F32)
    trunks = ((0, bp, tp, True), (rows_ctx, bs_, ts, False))

    n_ml = mlstm_w_in.shape[0]
    w_qkv, w_og, bg = _mlstm_weights(mlstm_w_in, mlstm_b_gate, heads, dk, dv)
    mlstm_state_in = (state_mlstm_C, state_mlstm_n.reshape(bs_, n_ml, 2, heads, 1, dk),
                      jnp.broadcast_to(state_mlstm_m.transpose(0, 1, 3, 2)[..., None], (bs_, n_ml, heads, 2, LANES)))
    w_out_of = (mlstm_w_out.astype(BF16), fnet_w_out.astype(BF16), lru_w_out.astype(BF16))
    lru_w_in_b, lru_gate_w_b = lru_w_in.astype(BF16), lru_gate_w.astype(BF16)
    d_ff = ffn_w_up.shape[2]
    tf = math.gcd(d_ff, 512)
    ffn_w_up_b = ffn_w_up.reshape(depth, d, d_ff // tf, tf).transpose(0, 2, 1, 3).astype(BF16)

    assert heads * dv == d and lru_conv_w.shape[2] == d
    y = jnp.zeros((rows_all, d), BF16)
    mlstm_state_out, new_h = None, []
    for i in range(depth):
        mods = mods_all[i]
        kind, j = i % 3, i // 3
        b_out = zero_bias
        if kind == 0:
            pm = functools.partial(_norm_mod_matmul, x_parts, norm_g[i, 0], mods, sh=0, sc=1, tm=tm, **rows)
            qkv = pm(w_qkv, j, out_dtype=BF16)
            og = pm(w_og, j, out_dtype=F32)
            for row0, nb, seq, is_ctx in trunks:
                nsub = math.gcd(nb, max(1, ts // seq))
                res = _mlstm_core(qkv, og, bg[j], mlstm_norm_g[j], None if is_ctx else mlstm_state_in, y,
                                  mlstm_state_out if is_ctx else None, layer=j, n_layers=n_ml, row0=row0,
                                  batch=nb, seq=seq, nsub=nsub, heads=heads, dk=dk, dv=dv, want_state=is_ctx)
                y = res[0]
                if is_ctx:
                    mlstm_state_out = res[1:]
        elif kind == 1:
            for row0, nb, seq, _ in trunks:
                y = _fnet_mix(x_parts[0][0], norm_g[i, 0], mods, y, row0=row0, batch=nb, seq=seq,
                              nsub=math.gcd(nb, max(1, ts // seq)), **rows)
            b_out = fnet_b_out[j]
        else:
            gx = _norm_mod_matmul(x_parts, norm_g[i, 0], mods, lru_w_in_b, j, sh=0, sc=1, out_dtype=F32, tm=tm,
                                  **rows)
            for row0, nb, seq, is_ctx in trunks:
                res = _lru_core(gx, lru_conv_w[j], lru_conv_b[j], lru_gate_w_b[j], lru_gate_b[j],
                                lru_lambda[j], None if is_ctx else state_lru_h[:, j], y,
                                row0=row0, batch=nb, seq=seq, want_state=is_ctx)
                y = res[0]
                if is_ctx:
                    new_h.append(res[1])
        x = _matmul_residual(y, w_out_of[kind], j, b_out, x_parts, norm_g[i, 1], mods, gate=2, tm=tm, **rows)
        ffn = functools.partial(_ffn, x, norm_g[i, 2], norm_g[i, 3], mods, ffn_w_up_b, ffn_w_down, i,
                                tm=tm_ffn, rows_ctx=rows_ctx, rows_per_sample=ts)
        if i < depth - 1:
            x_parts = [(ffn(row0=0, nrows=rows_all), 0)]
        else:
            y_prompt = ffn(row0=0, nrows=rows_ctx).reshape(bp, tp, d)
            y_sample = ffn(row0=rows_ctx, nrows=rows_all - rows_ctx).reshape(bs_, ts, d)

    new_c, new_n, new_m = mlstm_state_out
    return (y_prompt, y_sample, new_c, new_n.reshape(bp, n_ml, 2, heads, dk),
            new_m[..., 0].transpose(0, 1, 3, 2), jnp.stack(new_h, 1))
```

```python
import functools
import math

import numpy as np
import jax
import jax.numpy as jnp
from jax import lax
from jax.experimental import pallas as pl
from jax.experimental.pallas import tpu as pltpu

F32 = jnp.float32
BF16 = jnp.bfloat16
EPS = 1e-6
FNET_GROUPS = 8
LRU_C = 8.0
LANES = 128
MLSTM_L = 256
SUBLANES = 8
VMEM_LIMIT = 56 * 1024 * 1024
VMEM_LIMIT_FFN = 60 * 1024 * 1024
N_COND = 16


def _cparams(*sem, vmem_limit=VMEM_LIMIT):
    return pltpu.CompilerParams(dimension_semantics=sem, vmem_limit_bytes=vmem_limit)


def _sigmoid(x):
    return 0.5 * jnp.tanh(0.5 * x) + 0.5


def _log_sigmoid(x):
    return jnp.minimum(x, 0.0) - jnp.log1p(jnp.exp(-jnp.abs(x)))


def _gelu_tanh(x):
    return 0.5 * x * (1.0 + jnp.tanh(math.sqrt(2.0 / math.pi) * (x + 0.044715 * (x * x * x))))


def _rms(x):
    return x * lax.rsqrt(jnp.mean(x * x, axis=-1, keepdims=True) + EPS)


def _norm_mod(x, g, shift, scale):
    return (_rms(x) * g) * (1.0 + scale) + shift


def _cond_of_block(i, tm, rows_ctx, rows_per_sample):
    r = i * tm
    return jnp.where(r < rows_ctx, 0, 1 + (r - rows_ctx) // rows_per_sample)


def _mod_kernel(c_ref, w_ref, b_ref, o_ref):
    @pl.when(pl.program_id(1) == 0)
    def _():
        o_ref[...] = jnp.broadcast_to(b_ref[...], o_ref.shape)

    c = c_ref[...]
    a = (c * _sigmoid(c)).astype(BF16)
    o_ref[...] += jnp.dot(a, w_ref[...].astype(BF16), preferred_element_type=F32)


def _modulation(cond, mod_w, mod_b):
    depth, d, n = mod_w.shape
    tk = math.gcd(d, 2 * LANES)
    return pl.pallas_call(
        _mod_kernel,
        grid=(depth, d // tk),
        in_specs=[pl.BlockSpec((N_COND, tk), lambda l, k: (0, k)),
                  pl.BlockSpec((None, tk, n), lambda l, k: (l, k, 0)),
                  pl.BlockSpec((None, 1, n), lambda l, k: (l, 0, 0))],
        out_specs=pl.BlockSpec((None, N_COND, n), lambda l, k: (l, 0, 0)),
        out_shape=jax.ShapeDtypeStruct((depth, N_COND, n), F32),
        compiler_params=_cparams("parallel", "arbitrary"),
        name="modulation",
    )(cond, mod_w, mod_b.reshape(depth, 1, n))


def _piece_specs(x_parts, tm, d):
    specs = []
    for x, row0 in x_parts:
        i0, nblk = row0 // tm, x.shape[0] // tm
        specs.append(pl.BlockSpec((tm, d), lambda i, i0=i0, nblk=nblk: (jnp.clip(i - i0, 0, nblk - 1), 0)))
    return specs


def _for_piece(x_refs, first_blocks, fn):
    if len(x_refs) == 1:
        fn(x_refs[0])
        return
    i = pl.program_id(0)
    bounds = list(first_blocks[1:]) + [None]
    for x_ref, lo, hi in zip(x_refs, first_blocks, bounds):
        pl.when((i >= lo) if hi is None else ((i >= lo) & (i < hi)))(functools.partial(fn, x_ref))


def _pm_kernel(*refs, sh, sc, first_blocks):
    n_x = len(first_blocks)
    g_ref, mod_ref, w_ref, o_ref = refs[n_x:]

    def project(x_ref):
        h = _norm_mod(x_ref[...], g_ref[...], mod_ref[sh:sh + 1, :], mod_ref[sc:sc + 1, :]).astype(BF16)
        o_ref[...] = jnp.dot(h, w_ref[...], preferred_element_type=F32).astype(o_ref.dtype)

    _for_piece(refs[:n_x], first_blocks, project)


def _norm_mod_matmul(x_parts, g, mods, w, layer, *, sh, sc, out_dtype, tm, rows_all, rows_ctx, rows_per_sample):
    d, n = w.shape[1:]
    cond = functools.partial(_cond_of_block, tm=tm, rows_ctx=rows_ctx, rows_per_sample=rows_per_sample)
    return pl.pallas_call(
        functools.partial(_pm_kernel, sh=sh, sc=sc, first_blocks=tuple(r // tm for _, r in x_parts)),
        grid=(rows_all // tm,),
        in_specs=_piece_specs(x_parts, tm, d) + [
            pl.BlockSpec((1, d), lambda i: (0, 0)),
            pl.BlockSpec((None, 6, d), lambda i: (cond(i), 0, 0)),
            pl.BlockSpec((None, d, n), lambda i: (layer, 0, 0), pipeline_mode=pl.Buffered(1))],
        out_specs=pl.BlockSpec((tm, n), lambda i: (i, 0)),
        out_shape=jax.ShapeDtypeStruct((rows_all, n), out_dtype),
        compiler_params=_cparams("parallel"),
        name="norm_mod_matmul",
    )(*[x for x, _ in x_parts], g.reshape(1, d), mods, w)


def _me_kernel(*refs, gate, first_blocks):
    n_x = len(first_blocks)
    y_ref, w_ref, b_ref, g_ref, mod_ref, o_ref = refs[n_x:]

    def residual(x_ref):
        out = jnp.dot(y_ref[...], w_ref[...], preferred_element_type=F32) + b_ref[...]
        o_ref[...] = x_ref[...] + mod_ref[gate:gate + 1, :] * (_rms(out) * g_ref[...])

    _for_piece(refs[:n_x], first_blocks, residual)


def _matmul_residual(y, w, layer, b, x_parts, g, mods, *, gate, tm, rows_all, rows_ctx, rows_per_sample):
    k, d = w.shape[1:]
    cond = functools.partial(_cond_of_block, tm=tm, rows_ctx=rows_ctx, rows_per_sample=rows_per_sample)
    return pl.pallas_call(
        functools.partial(_me_kernel, gate=gate, first_blocks=tuple(r // tm for _, r in x_parts)),
        grid=(rows_all // tm,),
        in_specs=_piece_specs(x_parts, tm, d) + [
            pl.BlockSpec((tm, k), lambda i: (i, 0)),
            pl.BlockSpec((None, k, d), lambda i: (layer, 0, 0)),
            pl.BlockSpec((1, d), lambda i: (0, 0)),
            pl.BlockSpec((1, d), lambda i: (0, 0)),
            pl.BlockSpec((None, 6, d), lambda i: (cond(i), 0, 0))],
        out_specs=pl.BlockSpec((tm, d), lambda i: (i, 0)),
        out_shape=jax.ShapeDtypeStruct((rows_all, d), F32),
        compiler_params=_cparams("parallel"),
        name="matmul_residual",
    )(*[x for x, _ in x_parts], y, w, b.reshape(1, d), g.reshape(1, d), mods)


def _ffn_kernel(x_ref, g1_ref, g2_ref, mod_ref, wu_ref, wd_ref, o_ref, h_ref):
    j = pl.program_id(1)

    @pl.when(j == 0)
    def _():
        h_ref[...] = _norm_mod(x_ref[...], g1_ref[...], mod_ref[3:4, :], mod_ref[4:5, :]).astype(BF16)
        o_ref[...] = jnp.zeros_like(o_ref)

    u = jnp.maximum(jnp.dot(h_ref[...], wu_ref[...], preferred_element_type=F32), 0.0)
    o_ref[...] += jnp.dot((u * u).astype(BF16), wd_ref[...].astype(BF16), preferred_element_type=F32)

    @pl.when(j == pl.num_programs(1) - 1)
    def _():
        o_ref[...] = x_ref[...] + mod_ref[5:6, :] * (_rms(o_ref[...]) * g2_ref[...])


def _ffn(x, g_pre, g_post, mods, w_up, w_down, layer, *, row0, nrows, tm, rows_ctx, rows_per_sample):
    d = x.shape[1]
    _, nf, _, tf = w_up.shape
    i0 = row0 // tm
    cond = functools.partial(_cond_of_block, tm=tm, rows_ctx=rows_ctx, rows_per_sample=rows_per_sample)
    return pl.pallas_call(
        _ffn_kernel,
        grid=(nrows // tm, nf),
        in_specs=[pl.BlockSpec((tm, d), lambda i, j: (i0 + i, 0)),
                  pl.BlockSpec((1, d), lambda i, j: (0, 0)),
                  pl.BlockSpec((1, d), lambda i, j: (0, 0)),
                  pl.BlockSpec((None, 6, d), lambda i, j: (cond(i0 + i), 0, 0)),
                  pl.BlockSpec((None, None, d, tf), lambda i, j: (layer, j, 0, 0)),
                  pl.BlockSpec((None, tf, d), lambda i, j: (layer, j, 0))],
        out_specs=pl.BlockSpec((tm, d), lambda i, j: (i, 0)),
        out_shape=jax.ShapeDtypeStruct((nrows, d), F32),
        scratch_shapes=[pltpu.VMEM((tm, d), BF16)],
        compiler_params=_cparams("parallel", "arbitrary", vmem_limit=VMEM_LIMIT_FFN),
        name="ffn",
    )(x, g_pre.reshape(1, d), g_post.reshape(1, d), mods, w_up, w_down)


def _mlstm_kernel(*refs, seq, nsub, chunk, zero_init, want_state, n_alias, all_slots, layer):
    it = iter(refs)
    q_ref, k_ref, v_ref, o_ref, gt_ref, gb_ref, ng_ref = (next(it) for _ in range(7))
    if not zero_init:
        c0_ref, n0_ref, m0_ref = next(it), next(it), next(it)
    for _ in range(n_alias):
        next(it)
    y_ref = next(it)
    if want_state:
        co_ref, no_ref, mo_ref = next(it), next(it), next(it)
    hf_ref, hb_ref, c_ref, n_ref = (next(it) for _ in range(4))
    L = chunk
    nc = seq // L

    if zero_init:
        c_ref[...] = jnp.zeros_like(c_ref)
        n_ref[...] = jnp.zeros_like(n_ref)
        m_state = [[jnp.zeros((1, 1), F32), jnp.zeros((1, 1), F32)] for _ in range(nsub)]
    else:
        c_ref[...] = c0_ref[...]
        n_ref[...] = n0_ref[...]
        m_state = [[m0_ref[sb, 0:1, 0:1], m0_ref[sb, 1:2, 0:1]] for sb in range(nsub)]

    row = lax.broadcasted_iota(jnp.int32, (L, L), 0)
    col = lax.broadcasted_iota(jnp.int32, (L, L), 1)
    masks = (col <= row, col >= row)
    lane = lax.broadcasted_iota(jnp.int32, (L, LANES), 1)
    is_forget = (lane % 2) == 1

    def rows_of(sb, c):
        return pl.ds(sb * seq + c * L, L)

    scan_order = (tuple(range(nc)), tuple(range(nc - 1, -1, -1)))
    units = [(sb, c, d) for step in range(nc) for sb in range(nsub) for d, c in ((0, step), (1, nc - 1 - step))]

    def log_gates(sb, c):
        gts = gt_ref[rows_of(sb, c), :] + gb_ref[...]
        x = jnp.where(is_forget, _log_sigmoid(gts), gts)
        def split3(z, axis):
            z_hi = z.astype(BF16)
            r_1 = z - z_hi.astype(F32)
            z_mid = r_1.astype(BF16)
            return jnp.concatenate([z_hi, z_mid, (r_1 - z_mid.astype(F32)).astype(BF16)], axis=axis)

        xt = x.T
        return x, xt, split3(x, 1), split3(xt, 0)

    chunk_gates = {(sb, c): log_gates(sb, c) for sb in range(nsub) for c in range(nc)}
    gate = {}
    for sb, c, d in units:
        x, xt, x3, xt3 = chunk_gates[sb, c]
        y3 = jnp.dot(masks[d].astype(BF16), x3, preferred_element_type=F32)
        y = (y3[:, :LANES] + y3[:, LANES:2 * LANES]) + y3[:, 2 * LANES:]
        yt3 = jnp.dot(xt3, masks[1 - d].astype(BF16), preferred_element_type=F32)
        yt = (yt3[:LANES] + yt3[LANES:2 * LANES]) + yt3[2 * LANES:]
        b_c = y[:, 2 * d + 1:2 * d + 2]
        a_c = x[:, 2 * d:2 * d + 1] - b_c
        a_r = xt[2 * d:2 * d + 1, :] - yt[2 * d + 1:2 * d + 2, :]
        am = jnp.where(masks[d], jnp.broadcast_to(a_r, (L, L)), -jnp.inf)
        b_last = b_c[L - 1:L, :] if d == 0 else b_c[0:1, :]
        g_c = b_last + a_c
        gate[sb, c, d] = (am, b_c + jnp.max(am, axis=1, keepdims=True), b_c, b_last, g_c,
                          jnp.max(g_c, axis=0, keepdims=True))

    m_in, m_out = {}, {}
    for sb in range(nsub):
        for d in range(2):
            m = m_state[sb][d]
            for c in scan_order[d]:
                _, _, _, b_last, _, g_max = gate[sb, c, d]
                m_in[sb, c, d] = m
                m = jnp.maximum(b_last + m, g_max)
                m_out[sb, c, d] = m
            m_state[sb][d] = m

    part = {}
    for sb, c, d in units:
        am, r_max, b_c, b_last, g_c, _ = gate[sb, c, d]
        rows = rows_of(sb, c)
        q = q_ref[rows, :]
        k = k_ref[rows, :]
        v = v_ref[rows, :]
        inter = b_c + m_in[sb, c, d]
        m_t = jnp.maximum(inter, r_max)
        s = (lax.dot_general(q, k, (((1,), (1,)), ((), ())), preferred_element_type=F32)
             * jnp.exp(am + (b_c - m_t)))
        kw = k.astype(F32) * jnp.exp(g_c - m_out[sb, c, d])
        sv = jnp.dot(s.astype(BF16), jnp.concatenate([v, jnp.ones((L, LANES), BF16)], axis=1),
                     preferred_element_type=F32)
        part[sb, c, d] = (sv[:, :v.shape[1]], sv[:, v.shape[1]:], jnp.exp(inter - m_t), jnp.exp(-m_t),
                          jnp.exp(b_last + m_in[sb, c, d] - m_out[sb, c, d]), kw.astype(BF16),
                          jnp.sum(kw, axis=0, keepdims=True))

    for sb, c, d in units:
        sv, s_sum, w_inter, den_floor, w_c, kw, kw_sum = part[sb, c, d]
        rows = rows_of(sb, c)
        kv = lax.dot_general(kw, v_ref[rows, :], (((0,), (0,)), ((), ())), preferred_element_type=F32)
        if zero_init and c == scan_order[d][0]:
            num, den = sv, s_sum
            c_ref[sb, d] = kv
            n_ref[sb, d] = kw_sum
        else:
            q = q_ref[rows, :]
            c_old = c_ref[sb, d]
            n_old = n_ref[sb, d]
            num = sv + w_inter * jnp.dot(q, c_old.astype(BF16), preferred_element_type=F32)
            den = s_sum + w_inter * jnp.sum(q.astype(F32) * n_old, axis=1, keepdims=True)
            c_ref[sb, d] = w_c * c_old + kv
            n_ref[sb, d] = w_c * n_old + kw_sum
        inv = 1.0 / jnp.maximum(jnp.abs(den), den_floor)
        (hf_ref, hb_ref)[d][rows, :] = num * jnp.tile(inv, (1, num.shape[1] // LANES))

    def finish(c, _):
        rows = pl.ds(pl.multiple_of(c * L, L), L)
        hs = hf_ref[rows, :] + hb_ref[rows, :]
        y_ref[rows, :] = (_rms(hs) * ng_ref[...] * _sigmoid(o_ref[rows, :])).astype(y_ref.dtype)
        return 0

    lax.fori_loop(0, nsub * nc, finish, 0)

    if want_state:
        if all_slots:
            for ref in (co_ref, no_ref, mo_ref):
                ref[...] = jnp.zeros_like(ref)
            co_ref, no_ref, mo_ref = co_ref.at[:, layer], no_ref.at[:, layer], mo_ref.at[:, layer]
        co_ref[...] = c_ref[...]
        no_ref[...] = n_ref[...]
        for sb in range(nsub):
            for d in range(2):
                mo_ref[sb, d:d + 1, :] = jnp.broadcast_to(m_state[sb][d], (1, LANES))


def _mlstm_core(qkv, og, gate_b, norm_g, state, y_prev, state_prev, *, layer, n_layers, row0, batch, seq, nsub,
                heads, dk, dv, want_state):
    L = math.gcd(MLSTM_L, seq)
    rb = nsub * seq
    assert row0 % rb == 0 and batch % nsub == 0
    b0 = row0 // rb
    hk = heads * dk
    zero_init = state is None
    state_specs = [pl.BlockSpec((nsub, None, 2, None, dk, dv), lambda b, h: (b, layer, 0, h, 0, 0)),
                   pl.BlockSpec((nsub, None, 2, None, 1, dk), lambda b, h: (b, layer, 0, h, 0, 0)),
                   pl.BlockSpec((nsub, None, None, 2, LANES), lambda b, h: (b, layer, h, 0, 0))]
    in_specs = [pl.BlockSpec((rb, dk), lambda b, h: (b0 + b, h)),
                pl.BlockSpec((rb, dk), lambda b, h: (b0 + b, heads + h)),
                pl.BlockSpec((rb, dv), lambda b, h: (b0 + b, 2 * hk // dv + h)),
                pl.BlockSpec((rb, dv), lambda b, h: (b0 + b, h)),
                pl.BlockSpec((rb, LANES), lambda b, h: (b0 + b, heads * dv // LANES + h)),
                pl.BlockSpec((1, LANES), lambda b, h: (0, h)),
                pl.BlockSpec((1, dv), lambda b, h: (0, h))]
    args = [qkv, qkv, qkv, og, og, gate_b, norm_g.reshape(1, heads * dv)]
    if not zero_init:
        in_specs += state_specs
        args += list(state)
    in_specs.append(pl.BlockSpec(memory_space=pl.ANY))
    args.append(y_prev)
    aliases = {len(args) - 1: 0}
    out_specs = [pl.BlockSpec((rb, dv), lambda b, h: (b0 + b, h))]
    out_shape = [jax.ShapeDtypeStruct(y_prev.shape, y_prev.dtype)]
    all_slots = want_state and state_prev is None
    if want_state:
        if all_slots:
            out_specs += [pl.BlockSpec((nsub, n_layers, 2, None, dk, dv), lambda b, h: (b, 0, 0, h, 0, 0)),
                          pl.BlockSpec((nsub, n_layers, 2, None, 1, dk), lambda b, h: (b, 0, 0, h, 0, 0)),
                          pl.BlockSpec((nsub, n_layers, None, 2, LANES), lambda b, h: (b, 0, h, 0, 0))]
        else:
            out_specs += state_specs
        out_shape += [jax.ShapeDtypeStruct((batch, n_layers, 2, heads, dk, dv), F32),
                      jax.ShapeDtypeStruct((batch, n_layers, 2, heads, 1, dk), F32),
                      jax.ShapeDtypeStruct((batch, n_layers, heads, 2, LANES), F32)]
        if state_prev is not None:
            for k_out, arr in enumerate(state_prev):
                in_specs.append(pl.BlockSpec(memory_space=pl.ANY))
                args.append(arr)
                aliases[len(args) - 1] = 1 + k_out
    return pl.pallas_call(
        functools.partial(_mlstm_kernel, seq=seq, nsub=nsub, chunk=L, zero_init=zero_init,
                          want_state=want_state, n_alias=len(aliases), all_slots=all_slots, layer=layer),
        grid=(batch // nsub, heads),
        in_specs=in_specs,
        out_specs=out_specs,
        out_shape=out_shape,
        input_output_aliases=aliases,
        scratch_shapes=[pltpu.VMEM((rb, dv), F32), pltpu.VMEM((rb, dv), F32),
                        pltpu.VMEM((nsub, 2, dk, dv), F32), pltpu.VMEM((nsub, 2, 1, dk), F32)],
        compiler_params=_cparams("parallel", "parallel"),
        name="mlstm_core",
    )(*args)


def _dft_mats(seq, gc):
    def cs(n):
        idx = np.arange(n)
        ang = 2.0 * np.pi * ((idx[:, None] * idx[None, :]) % n) / n
        return np.cos(ang) / math.sqrt(n), np.sin(ang) / math.sqrt(n)
    ct, st = cs(seq)
    cc, sc = cs(gc)
    w_time = np.concatenate([ct, -st], axis=1)
    w_chan = np.concatenate([cc, sc], axis=1)
    return jnp.asarray(w_time, BF16), jnp.asarray(w_chan, BF16)


def _fnet_kernel(x_ref, g_ref, mod_ref, wt_ref, wc_ref, _, o_ref, *, seq, nsub, gc):
    h = _norm_mod(x_ref[...], g_ref[...], mod_ref[0:1, :], mod_ref[1:2, :]).astype(BF16)
    for grp in range(h.shape[1] // gc):
        cols = slice(grp * gc, (grp + 1) * gc)
        y = jnp.dot(h[:, cols], wc_ref[...], preferred_element_type=F32)
        for sb in range(nsub):
            ys = y[sb * seq:(sb + 1) * seq]
            ycat = jnp.concatenate([ys[:, :gc], ys[:, gc:]], axis=0).astype(BF16)
            o_ref[sb * seq:(sb + 1) * seq, cols] = jnp.dot(wt_ref[...], ycat,
                                                           preferred_element_type=F32).astype(o_ref.dtype)


def _fnet_mix(x, g, mods, y_prev, *, row0, batch, seq, nsub, rows_all, rows_ctx, rows_per_sample):
    d = x.shape[1]
    gc = d // FNET_GROUPS
    rb = nsub * seq
    assert row0 % rb == 0 and batch % nsub == 0 and x.shape[0] == rows_all
    b0 = row0 // rb
    cond = functools.partial(_cond_of_block, tm=rb, rows_ctx=rows_ctx, rows_per_sample=rows_per_sample)
    w_time, w_chan = _dft_mats(seq, gc)
    in_specs = [pl.BlockSpec((rb, d), lambda b: (b0 + b, 0)),
                pl.BlockSpec((1, d), lambda b: (0, 0)),
                pl.BlockSpec((None, 6, d), lambda b: (cond(b0 + b), 0, 0)),
                pl.BlockSpec((seq, 2 * seq), lambda b: (0, 0)),
                pl.BlockSpec((gc, 2 * gc), lambda b: (0, 0)),
                pl.BlockSpec(memory_space=pl.ANY)]
    args = [x, g.reshape(1, d), mods, w_time, w_chan, y_prev]
    return pl.pallas_call(
        functools.partial(_fnet_kernel, seq=seq, nsub=nsub, gc=gc),
        grid=(batch // nsub,),
        in_specs=in_specs,
        out_specs=pl.BlockSpec((rb, d), lambda b: (b0 + b, 0)),
        out_shape=jax.ShapeDtypeStruct(y_prev.shape, y_prev.dtype),
        input_output_aliases={len(args) - 1: 0},
        compiler_params=_cparams("parallel"),
        name="fnet_mix",
    )(*args)


def _lru_kernel(*refs, seq, nsub, zero_init, want_state):
    it = iter(refs)
    br_ref, xb_ref, cw_ref, cb_ref, gw_ref, gbias_ref, lam_ref = (next(it) for _ in range(7))
    if not zero_init:
        h0_ref = next(it)
    next(it)
    y_ref = next(it)
    if want_state:
        hl_ref = next(it)
    a_ref, u_ref, hs_ref = (next(it) for _ in range(3))
    T = seq
    rows_blk = nsub * T
    S = SUBLANES
    x = xb_ref[...]
    tt = lax.broadcasted_iota(jnp.int32, x.shape, 0) % T
    xm1 = jnp.where(tt >= 1, pltpu.roll(x, 1, 0), 0.0)
    xp1 = jnp.where(tt < T - 1, pltpu.roll(x, rows_blk - 1, 0), 0.0)
    xp2 = jnp.where(tt < T - 2, pltpu.roll(x, rows_blk - 2, 0), 0.0)
    xc = (cw_ref[0:1, :] * xm1 + cw_ref[1:2, :] * x + cw_ref[2:3, :] * xp1 + cw_ref[3:4, :] * xp2
          + cb_ref[...])
    xcb = xc.astype(BF16)
    r8 = lax.broadcasted_iota(jnp.int32, (rows_blk // S, S, x.shape[1]), 1)
    for d in range(2):
        pre_r = jnp.dot(xcb, gw_ref[d, 0], preferred_element_type=F32) + gbias_ref[2 * d:2 * d + 1, :]
        pre_i = jnp.dot(xcb, gw_ref[d, 1], preferred_element_type=F32) + gbias_ref[2 * d + 1:2 * d + 2, :]
        log_a = LRU_C * _sigmoid(pre_r) * _log_sigmoid(lam_ref[d:d + 1, :])
        a = jnp.exp(log_a)
        u = jnp.sqrt(-jnp.tanh(log_a) * (a * a + 1.0)) * (_sigmoid(pre_i) * xc)
        a = a.reshape(rows_blk // S, S, a.shape[1])
        u = u.reshape(a.shape)
        s = 1
        while s < S:
            shift, valid = (s, r8 >= s) if d == 0 else (S - s, r8 < S - s)
            u = jnp.where(valid, a * pltpu.roll(u, shift, 1) + u, u)
            a = jnp.where(valid, a * pltpu.roll(a, shift, 1), a)
            s *= 2
        a_ref[d] = a.reshape(x.shape)
        u_ref[d] = u.reshape(x.shape)

    if zero_init:
        h_init = tuple(jnp.zeros((1, x.shape[1]), F32) for _ in range(2 * nsub))
    else:
        h_init = tuple(h0_ref[sb, d:d + 1, :] for sb in range(nsub) for d in range(2))
    groups = T // S

    def group(i, carry):
        out = []
        for sb in range(nsub):
            rf = pl.ds(pl.multiple_of((sb * groups + i) * S, S), S)
            h_f = a_ref[0, rf, :] * carry[2 * sb] + u_ref[0, rf, :]
            hs_ref[0, rf, :] = h_f
            rb = pl.ds(pl.multiple_of((sb * groups + groups - 1 - i) * S, S), S)
            h_b = a_ref[1, rb, :] * carry[2 * sb + 1] + u_ref[1, rb, :]
            hs_ref[1, rb, :] = h_b
            out += [h_f[S - 1:S, :], h_b[0:1, :]]
        return tuple(out)

    h_last = lax.fori_loop(0, groups, group, h_init, unroll=4)
    y_ref[...] = ((hs_ref[0] + hs_ref[1]) * _gelu_tanh(br_ref[...])).astype(y_ref.dtype)
    if want_state:
        for sb in range(nsub):
            for d in range(2):
                hl_ref[sb, d:d + 1, :] = h_last[2 * sb + d]


def _lru_core(gx, conv_w, conv_b, gate_w, gate_b, lam, h0, y_prev, *, row0, batch, seq, nsub, want_state):
    width = conv_w.shape[1]
    nb = gate_w.shape[2]
    bs = width // nb
    rb = nsub * seq
    assert row0 % rb == 0 and batch % nsub == 0
    b0 = row0 // rb
    zero_init = h0 is None
    in_specs = [pl.BlockSpec((rb, bs), lambda b, n: (b0 + b, n)),
                pl.BlockSpec((rb, bs), lambda b, n: (b0 + b, nb + n)),
                pl.BlockSpec((conv_w.shape[0], bs), lambda b, n: (0, n)),
                pl.BlockSpec((1, bs), lambda b, n: (0, n)),
                pl.BlockSpec((2, 2, None, bs, bs), lambda b, n: (0, 0, n, 0, 0)),
                pl.BlockSpec((4, bs), lambda b, n: (0, n)),
                pl.BlockSpec((2, bs), lambda b, n: (0, n))]
    args = [gx, gx, conv_w, conv_b.reshape(1, width), gate_w, gate_b.reshape(4, width), lam]
    if not zero_init:
        in_specs.append(pl.BlockSpec((nsub, 2, bs), lambda b, n: (b, 0, n)))
        args.append(h0)
    in_specs.append(pl.BlockSpec(memory_space=pl.ANY))
    args.append(y_prev)
    aliases = {len(args) - 1: 0}
    out_specs = [pl.BlockSpec((rb, bs), lambda b, n: (b0 + b, n))]
    out_shape = [jax.ShapeDtypeStruct(y_prev.shape, y_prev.dtype)]
    if want_state:
        out_specs.append(pl.BlockSpec((nsub, 2, bs), lambda b, n: (b, 0, n)))
        out_shape.append(jax.ShapeDtypeStruct((batch, 2, width), F32))
    return pl.pallas_call(
        functools.partial(_lru_kernel, seq=seq, nsub=nsub, zero_init=zero_init, want_state=want_state),
        grid=(batch // nsub, nb),
        in_specs=in_specs,
        out_specs=out_specs,
        out_shape=out_shape,
        input_output_aliases=aliases,
        scratch_shapes=[pltpu.VMEM((2, rb, bs), F32)] * 3,
        compiler_params=_cparams("parallel", "parallel"),
        name="lru_core",
    )(*args)


def _mlstm_weights(w_in, b_gate, heads, dk, dv):
    nl, d, _ = w_in.shape
    hk, hv = heads * dk, heads * dv
    wq, wk, wv, wo, wg = jnp.split(w_in, [hk, 2 * hk, 2 * hk + hv, 2 * hk + 2 * hv], axis=2)
    w_qkv = jnp.concatenate([wq, wk * (dk ** -0.5), wv], axis=2).astype(BF16)
    wg = wg.reshape(nl, d, 2, 2, heads).transpose(0, 1, 4, 2, 3).reshape(nl, d, heads, 4)
    wg = jnp.pad(wg, ((0, 0), (0, 0), (0, 0), (0, LANES - 4))).reshape(nl, d, heads * LANES)
    w_og = jnp.concatenate([wo, wg], axis=2).astype(BF16)
    bg = b_gate.astype(F32).transpose(0, 3, 1, 2).reshape(nl, heads, 4)
    bg = jnp.pad(bg, ((0, 0), (0, 0), (0, LANES - 4))).reshape(nl, 1, heads * LANES)
    return w_qkv, w_og, bg


def kernel(x_prompt, x_sample, c, state_mlstm_C, state_mlstm_n, state_mlstm_m, state_lru_h, c_ctx,
           mod_w, mod_b, norm_g, ffn_w_up, ffn_w_down, mlstm_w_in, mlstm_b_gate, mlstm_norm_g, mlstm_w_out,
           fnet_w_out, fnet_b_out, lru_w_in, lru_conv_w, lru_conv_b, lru_gate_w, lru_gate_b, lru_lambda,
           lru_w_out):
    bp, tp, d = x_prompt.shape
    bs_, ts, _ = x_sample.shape
    depth = mod_w.shape[0]
    heads = mlstm_b_gate.shape[-1]
    dk, dv = state_mlstm_C.shape[-2:]
    rows_ctx = bp * tp
    rows_all = rows_ctx + bs_ * ts
    rows = dict(rows_all=rows_all, rows_ctx=rows_ctx, rows_per_sample=ts)
    assert 1 + bs_ <= N_COND
    tm = math.gcd(512, math.gcd(rows_ctx, ts))
    tm_ffn = math.gcd(1024, math.gcd(rows_ctx, ts))

    x_parts = [(x_prompt.reshape(rows_ctx, d), 0), (x_sample.reshape(bs_ * ts, d), rows_ctx)]
    cond = jnp.zeros((N_COND, d), F32).at[0].set(c_ctx).at[1:1 + bs_].set(c)
    mods_all = _modulation(cond, mod_w, mod_b).reshape(depth, N_COND, 6, d)
    zero_bias = jnp.zeros((d,), F32)
    trunks = ((0, bp, tp, True), (rows_ctx, bs_, ts, False))

    n_ml = mlstm_w_in.shape[0]
    w_qkv, w_og, bg = _mlstm_weights(mlstm_w_in, mlstm_b_gate, heads, dk, dv)
    mlstm_state_in = (state_mlstm_C, state_mlstm_n.reshape(bs_, n_ml, 2, heads, 1, dk),
                      jnp.broadcast_to(state_mlstm_m.transpose(0, 1, 3, 2)[..., None], (bs_, n_ml, heads, 2, LANES)))
    w_out_of = (mlstm_w_out.astype(BF16), fnet_w_out.astype(BF16), lru_w_out.astype(BF16))
    lru_w_in_b, lru_gate_w_b = lru_w_in.astype(BF16), lru_gate_w.astype(BF16)
    d_ff = ffn_w_up.shape[2]
    tf = math.gcd(d_ff, 512)
    ffn_w_up_b = ffn_w_up.reshape(depth, d, d_ff // tf, tf).transpose(0, 2, 1, 3).astype(BF16)

    assert heads * dv == d and lru_conv_w.shape[2] == d
    y = jnp.zeros((rows_all, d), BF16)
    mlstm_state_out, new_h = None, []
    for i in range(depth):
        mods = mods_all[i]
        kind, j = i % 3, i // 3
        b_out = zero_bias
        if kind == 0:
            pm = functools.partial(_norm_mod_matmul, x_parts, norm_g[i, 0], mods, sh=0, sc=1, tm=tm, **rows)
            qkv = pm(w_qkv, j, out_dtype=BF16)
            og = pm(w_og, j, out_dtype=F32)
            for row0, nb, seq, is_ctx in trunks:
                nsub = math.gcd(nb, max(1, ts // seq))
                res = _mlstm_core(qkv, og, bg[j], mlstm_norm_g[j], None if is_ctx else mlstm_state_in, y,
                                  mlstm_state_out if is_ctx else None, layer=j, n_layers=n_ml, row0=row0,
                                  batch=nb, seq=seq, nsub=nsub, heads=heads, dk=dk, dv=dv, want_state=is_ctx)
                y = res[0]
                if is_ctx:
                    mlstm_state_out = res[1:]
        elif kind == 1:
            for row0, nb, seq, _ in trunks:
                y = _fnet_mix(x_parts[0][0], norm_g[i, 0], mods, y, row0=row0, batch=nb, seq=seq,
                              nsub=math.gcd(nb, max(1, ts // seq)), **rows)
            b_out = fnet_b_out[j]
        else:
            gx = _norm_mod_matmul(x_parts, norm_g[i, 0], mods, lru_w_in_b, j, sh=0, sc=1, out_dtype=F32, tm=tm,
                                  **rows)
            for row0, nb, seq, is_ctx in trunks:
                res = _lru_core(gx, lru_conv_w[j], lru_conv_b[j], lru_gate_w_b[j], lru_gate_b[j],
                                lru_lambda[j], None if is_ctx else state_lru_h[:, j], y,
                                row0=row0, batch=nb, seq=seq, nsub=math.gcd(nb, max(1, ts // seq)),
                                want_state=is_ctx)
                y = res[0]
                if is_ctx:
                    new_h.append(res[1])
        x = _matmul_residual(y, w_out_of[kind], j, b_out, x_parts, norm_g[i, 1], mods, gate=2, tm=tm, **rows)
        ffn = functools.partial(_ffn, x, norm_g[i, 2], norm_g[i, 3], mods, ffn_w_up_b, ffn_w_down, i,
                                tm=tm_ffn, rows_ctx=rows_ctx, rows_per_sample=ts)
        if i < depth - 1:
            x_parts = [(ffn(row0=0, nrows=rows_all), 0)]
        else:
            y_prompt = ffn(row0=0, nrows=rows_ctx).reshape(bp, tp, d)
            y_sample = ffn(row0=rows_ctx, nrows=rows_all - rows_ctx).reshape(bs_, ts, d)

    new_c, new_n, new_m = mlstm_state_out
    return (y_prompt, y_sample, new_c, new_n.reshape(bp, n_ml, 2, heads, dk),
            new_m[..., 0].transpose(0, 1, 3, 2), jnp.stack(new_h, 1))
```

```python
import functools
import math

import numpy as np
import jax
import jax.numpy as jnp
from jax import lax
from jax.experimental import pallas as pl
from jax.experimental.pallas import tpu as pltpu

F32 = jnp.float32
BF16 = jnp.bfloat16
EPS = 1e-6
FNET_GROUPS = 8
LRU_C = 8.0
LANES = 128
MLSTM_L = 256
SUBLANES = 8
VMEM_LIMIT = 56 * 1024 * 1024
VMEM_LIMIT_FFN = 60 * 1024 * 1024
N_COND = 16
ROWS_PER_STEP = 512
ROWS_PER_STEP_FFN = 1024
FFN_TILE = 512
MOD_TILE_K = 256


def _cparams(*sem, vmem_limit=VMEM_LIMIT):
    return pltpu.CompilerParams(dimension_semantics=sem, vmem_limit_bytes=vmem_limit)


def _sigmoid(x):
    return 0.5 * jnp.tanh(0.5 * x) + 0.5


def _log_sigmoid(x):
    return jnp.minimum(x, 0.0) - jnp.log1p(jnp.exp(-jnp.abs(x)))


def _gelu_tanh(x):
    return 0.5 * x * (1.0 + jnp.tanh(math.sqrt(2.0 / math.pi) * (x + 0.044715 * (x * x * x))))


def _rms(x):
    return x * lax.rsqrt(jnp.mean(x * x, axis=-1, keepdims=True) + EPS)


def _norm_mod(x, g, shift, scale):
    return (_rms(x) * g) * (1.0 + scale) + shift


def _cond_of_block(i, tm, rows_ctx, rows_per_sample):
    r = i * tm
    return jnp.where(r < rows_ctx, 0, 1 + (r - rows_ctx) // rows_per_sample)


def _mod_kernel(c_ref, w_ref, b_ref, o_ref):
    @pl.when(pl.program_id(1) == 0)
    def _():
        o_ref[...] = jnp.broadcast_to(b_ref[...], o_ref.shape)

    c = c_ref[...]
    a = (c * _sigmoid(c)).astype(BF16)
    o_ref[...] += jnp.dot(a, w_ref[...].astype(BF16), preferred_element_type=F32)


def _modulation(cond, mod_w, mod_b):
    depth, d, n = mod_w.shape
    tk = math.gcd(d, MOD_TILE_K)
    return pl.pallas_call(
        _mod_kernel,
        grid=(depth, d // tk),
        in_specs=[pl.BlockSpec((N_COND, tk), lambda l, k: (0, k)),
                  pl.BlockSpec((None, tk, n), lambda l, k: (l, k, 0)),
                  pl.BlockSpec((None, 1, n), lambda l, k: (l, 0, 0))],
        out_specs=pl.BlockSpec((None, N_COND, n), lambda l, k: (l, 0, 0)),
        out_shape=jax.ShapeDtypeStruct((depth, N_COND, n), F32),
        compiler_params=_cparams("parallel", "arbitrary"),
        name="modulation",
    )(cond, mod_w, mod_b.reshape(depth, 1, n))


def _piece_specs(x_parts, tm, d):
    specs = []
    for x, row0 in x_parts:
        i0, nblk = row0 // tm, x.shape[0] // tm
        specs.append(pl.BlockSpec((tm, d), lambda i, i0=i0, nblk=nblk: (jnp.clip(i - i0, 0, nblk - 1), 0)))
    return specs


def _for_piece(x_refs, first_blocks, fn):
    if len(x_refs) == 1:
        fn(x_refs[0])
        return
    i = pl.program_id(0)
    bounds = list(first_blocks[1:]) + [None]
    for x_ref, lo, hi in zip(x_refs, first_blocks, bounds):
        pl.when((i >= lo) if hi is None else ((i >= lo) & (i < hi)))(functools.partial(fn, x_ref))


def _pm_kernel(*refs, sh, sc, first_blocks):
    n_x = len(first_blocks)
    g_ref, mod_ref, w_ref, o_ref = refs[n_x:]

    def project(x_ref):
        h = _norm_mod(x_ref[...], g_ref[...], mod_ref[sh:sh + 1, :], mod_ref[sc:sc + 1, :]).astype(BF16)
        o_ref[...] = jnp.dot(h, w_ref[...], preferred_element_type=F32).astype(o_ref.dtype)

    _for_piece(refs[:n_x], first_blocks, project)


def _norm_mod_matmul(x_parts, g, mods, w, layer, *, sh, sc, out_dtype, tm, rows_all, rows_ctx, rows_per_sample):
    d, n = w.shape[1:]
    cond = functools.partial(_cond_of_block, tm=tm, rows_ctx=rows_ctx, rows_per_sample=rows_per_sample)
    return pl.pallas_call(
        functools.partial(_pm_kernel, sh=sh, sc=sc, first_blocks=tuple(r // tm for _, r in x_parts)),
        grid=(rows_all // tm,),
        in_specs=_piece_specs(x_parts, tm, d) + [
            pl.BlockSpec((1, d), lambda i: (0, 0)),
            pl.BlockSpec((None, 6, d), lambda i: (cond(i), 0, 0)),
            pl.BlockSpec((None, d, n), lambda i: (layer, 0, 0), pipeline_mode=pl.Buffered(1))],
        out_specs=pl.BlockSpec((tm, n), lambda i: (i, 0)),
        out_shape=jax.ShapeDtypeStruct((rows_all, n), out_dtype),
        compiler_params=_cparams("parallel"),
        name="norm_mod_matmul",
    )(*[x for x, _ in x_parts], g.reshape(1, d), mods, w)


def _me_kernel(*refs, gate, first_blocks):
    n_x = len(first_blocks)
    y_ref, w_ref, b_ref, g_ref, mod_ref, o_ref = refs[n_x:]

    def residual(x_ref):
        out = jnp.dot(y_ref[...], w_ref[...], preferred_element_type=F32) + b_ref[...]
        o_ref[...] = x_ref[...] + mod_ref[gate:gate + 1, :] * (_rms(out) * g_ref[...])

    _for_piece(refs[:n_x], first_blocks, residual)


def _matmul_residual(y, w, layer, b, x_parts, g, mods, *, gate, tm, rows_all, rows_ctx, rows_per_sample):
    k, d = w.shape[1:]
    cond = functools.partial(_cond_of_block, tm=tm, rows_ctx=rows_ctx, rows_per_sample=rows_per_sample)
    return pl.pallas_call(
        functools.partial(_me_kernel, gate=gate, first_blocks=tuple(r // tm for _, r in x_parts)),
        grid=(rows_all // tm,),
        in_specs=_piece_specs(x_parts, tm, d) + [
            pl.BlockSpec((tm, k), lambda i: (i, 0)),
            pl.BlockSpec((None, k, d), lambda i: (layer, 0, 0)),
            pl.BlockSpec((1, d), lambda i: (0, 0)),
            pl.BlockSpec((1, d), lambda i: (0, 0)),
            pl.BlockSpec((None, 6, d), lambda i: (cond(i), 0, 0))],
        out_specs=pl.BlockSpec((tm, d), lambda i: (i, 0)),
        out_shape=jax.ShapeDtypeStruct((rows_all, d), F32),
        compiler_params=_cparams("parallel"),
        name="matmul_residual",
    )(*[x for x, _ in x_parts], y, w, b.reshape(1, d), g.reshape(1, d), mods)


def _ffn_kernel(x_ref, g1_ref, g2_ref, mod_ref, wu_ref, wd_ref, o_ref, h_ref):
    j = pl.program_id(1)

    @pl.when(j == 0)
    def _():
        h_ref[...] = _norm_mod(x_ref[...], g1_ref[...], mod_ref[3:4, :], mod_ref[4:5, :]).astype(BF16)
        o_ref[...] = jnp.zeros_like(o_ref)

    u = jnp.maximum(jnp.dot(h_ref[...], wu_ref[...], preferred_element_type=F32), 0.0)
    o_ref[...] += jnp.dot((u * u).astype(BF16), wd_ref[...].astype(BF16), preferred_element_type=F32)

    @pl.when(j == pl.num_programs(1) - 1)
    def _():
        o_ref[...] = x_ref[...] + mod_ref[5:6, :] * (_rms(o_ref[...]) * g2_ref[...])


def _ffn(x, g_pre, g_post, mods, w_up, w_down, layer, *, row0, nrows, tm, rows_ctx, rows_per_sample):
    d = x.shape[1]
    _, nf, _, tf = w_up.shape
    i0 = row0 // tm
    cond = functools.partial(_cond_of_block, tm=tm, rows_ctx=rows_ctx, rows_per_sample=rows_per_sample)
    return pl.pallas_call(
        _ffn_kernel,
        grid=(nrows // tm, nf),
        in_specs=[pl.BlockSpec((tm, d), lambda i, j: (i0 + i, 0)),
                  pl.BlockSpec((1, d), lambda i, j: (0, 0)),
                  pl.BlockSpec((1, d), lambda i, j: (0, 0)),
                  pl.BlockSpec((None, 6, d), lambda i, j: (cond(i0 + i), 0, 0)),
                  pl.BlockSpec((None, None, d, tf), lambda i, j: (layer, j, 0, 0)),
                  pl.BlockSpec((None, tf, d), lambda i, j: (layer, j, 0))],
        out_specs=pl.BlockSpec((tm, d), lambda i, j: (i, 0)),
        out_shape=jax.ShapeDtypeStruct((nrows, d), F32),
        scratch_shapes=[pltpu.VMEM((tm, d), BF16)],
        compiler_params=_cparams("parallel", "arbitrary", vmem_limit=VMEM_LIMIT_FFN),
        name="ffn",
    )(x, g_pre.reshape(1, d), g_post.reshape(1, d), mods, w_up, w_down)


def _mlstm_kernel(*refs, seq, nsub, chunk, zero_init, want_state, n_alias, all_slots, layer):
    it = iter(refs)
    q_ref, k_ref, v_ref, o_ref, gt_ref, gb_ref, ng_ref = (next(it) for _ in range(7))
    if not zero_init:
        c0_ref, n0_ref, m0_ref = next(it), next(it), next(it)
    for _ in range(n_alias):
        next(it)
    y_ref = next(it)
    if want_state:
        co_ref, no_ref, mo_ref = next(it), next(it), next(it)
    hf_ref, hb_ref, c_ref, n_ref = (next(it) for _ in range(4))
    L = chunk
    nc = seq // L

    if zero_init:
        c_ref[...] = jnp.zeros_like(c_ref)
        n_ref[...] = jnp.zeros_like(n_ref)
        m_state = [[jnp.zeros((1, 1), F32), jnp.zeros((1, 1), F32)] for _ in range(nsub)]
    else:
        c_ref[...] = c0_ref[...]
        n_ref[...] = n0_ref[...]
        m_state = [[m0_ref[sb, 0:1, 0:1], m0_ref[sb, 1:2, 0:1]] for sb in range(nsub)]

    row = lax.broadcasted_iota(jnp.int32, (L, L), 0)
    col = lax.broadcasted_iota(jnp.int32, (L, L), 1)
    masks = (col <= row, col >= row)
    lane = lax.broadcasted_iota(jnp.int32, (L, LANES), 1)
    is_forget = (lane % 2) == 1

    def rows_of(sb, c):
        return pl.ds(sb * seq + c * L, L)

    scan_order = (tuple(range(nc)), tuple(range(nc - 1, -1, -1)))
    units = [(sb, c, d) for step in range(nc) for sb in range(nsub) for d, c in ((0, step), (1, nc - 1 - step))]

    def log_gates(sb, c):
        gts = gt_ref[rows_of(sb, c), :] + gb_ref[...]
        x = jnp.where(is_forget, _log_sigmoid(gts), gts)
        def split3(z, axis):
            z_hi = z.astype(BF16)
            r_1 = z - z_hi.astype(F32)
            z_mid = r_1.astype(BF16)
            return jnp.concatenate([z_hi, z_mid, (r_1 - z_mid.astype(F32)).astype(BF16)], axis=axis)

        xt = x.T
        return x, xt, split3(x, 1), split3(xt, 0)

    chunk_gates = {(sb, c): log_gates(sb, c) for sb in range(nsub) for c in range(nc)}
    gate = {}
    for sb, c, d in units:
        x, xt, x3, xt3 = chunk_gates[sb, c]
        y3 = jnp.dot(masks[d].astype(BF16), x3, preferred_element_type=F32)
        y = (y3[:, :LANES] + y3[:, LANES:2 * LANES]) + y3[:, 2 * LANES:]
        yt3 = jnp.dot(xt3, masks[1 - d].astype(BF16), preferred_element_type=F32)
        yt = (yt3[:LANES] + yt3[LANES:2 * LANES]) + yt3[2 * LANES:]
        b_c = y[:, 2 * d + 1:2 * d + 2]
        a_c = x[:, 2 * d:2 * d + 1] - b_c
        a_r = xt[2 * d:2 * d + 1, :] - yt[2 * d + 1:2 * d + 2, :]
        am = jnp.where(masks[d], jnp.broadcast_to(a_r, (L, L)), -jnp.inf)
        b_last = b_c[L - 1:L, :] if d == 0 else b_c[0:1, :]
        g_c = b_last + a_c
        gate[sb, c, d] = (am, b_c + jnp.max(am, axis=1, keepdims=True), b_c, b_last, g_c,
                          jnp.max(g_c, axis=0, keepdims=True))

    m_in, m_out = {}, {}
    for sb in range(nsub):
        for d in range(2):
            m = m_state[sb][d]
            for c in scan_order[d]:
                _, _, _, b_last, _, g_max = gate[sb, c, d]
                m_in[sb, c, d] = m
                m = jnp.maximum(b_last + m, g_max)
                m_out[sb, c, d] = m
            m_state[sb][d] = m

    part = {}
    for sb, c, d in units:
        am, r_max, b_c, b_last, g_c, _ = gate[sb, c, d]
        rows = rows_of(sb, c)
        q = q_ref[rows, :]
        k = k_ref[rows, :]
        v = v_ref[rows, :]
        inter = b_c + m_in[sb, c, d]
        m_t = jnp.maximum(inter, r_max)
        s = (lax.dot_general(q, k, (((1,), (1,)), ((), ())), preferred_element_type=F32)
             * jnp.exp(am + (b_c - m_t)))
        kw = k.astype(F32) * jnp.exp(g_c - m_out[sb, c, d])
        sv = jnp.dot(s.astype(BF16), jnp.concatenate([v, jnp.ones((L, LANES), BF16)], axis=1),
                     preferred_element_type=F32)
        part[sb, c, d] = (sv[:, :v.shape[1]], sv[:, v.shape[1]:], jnp.exp(inter - m_t), jnp.exp(-m_t),
                          jnp.exp(b_last + m_in[sb, c, d] - m_out[sb, c, d]), kw.astype(BF16),
                          jnp.sum(kw, axis=0, keepdims=True))

    for sb, c, d in units:
        sv, s_sum, w_inter, den_floor, w_c, kw, kw_sum = part[sb, c, d]
        rows = rows_of(sb, c)
        kv = lax.dot_general(kw, v_ref[rows, :], (((0,), (0,)), ((), ())), preferred_element_type=F32)
        if zero_init and c == scan_order[d][0]:
            num, den = sv, s_sum
            c_ref[sb, d] = kv
            n_ref[sb, d] = kw_sum
        else:
            q = q_ref[rows, :]
            c_old = c_ref[sb, d]
            n_old = n_ref[sb, d]
            num = sv + w_inter * jnp.dot(q, c_old.astype(BF16), preferred_element_type=F32)
            den = s_sum + w_inter * jnp.sum(q.astype(F32) * n_old, axis=1, keepdims=True)
            c_ref[sb, d] = w_c * c_old + kv
            n_ref[sb, d] = w_c * n_old + kw_sum
        inv = 1.0 / jnp.maximum(jnp.abs(den), den_floor)
        (hf_ref, hb_ref)[d][rows, :] = num * jnp.tile(inv, (1, num.shape[1] // LANES))

    def finish(c, _):
        rows = pl.ds(pl.multiple_of(c * L, L), L)
        hs = hf_ref[rows, :] + hb_ref[rows, :]
        y_ref[rows, :] = (_rms(hs) * ng_ref[...] * _sigmoid(o_ref[rows, :])).astype(y_ref.dtype)
        return 0

    lax.fori_loop(0, nsub * nc, finish, 0)

    if want_state:
        if all_slots:
            for ref in (co_ref, no_ref, mo_ref):
                ref[...] = jnp.zeros_like(ref)
            co_ref, no_ref, mo_ref = co_ref.at[:, layer], no_ref.at[:, layer], mo_ref.at[:, layer]
        co_ref[...] = c_ref[...]
        no_ref[...] = n_ref[...]
        for sb in range(nsub):
            for d in range(2):
                mo_ref[sb, d:d + 1, :] = jnp.broadcast_to(m_state[sb][d], (1, LANES))


def _mlstm_core(qkv, og, gate_b, norm_g, state, y_prev, state_prev, *, layer, n_layers, row0, batch, seq, nsub,
                heads, dk, dv, want_state):
    L = math.gcd(MLSTM_L, seq)
    rb = nsub * seq
    assert row0 % rb == 0 and batch % nsub == 0
    b0 = row0 // rb
    hk = heads * dk
    zero_init = state is None
    state_specs = [pl.BlockSpec((nsub, None, 2, None, dk, dv), lambda b, h: (b, layer, 0, h, 0, 0)),
                   pl.BlockSpec((nsub, None, 2, None, 1, dk), lambda b, h: (b, layer, 0, h, 0, 0)),
                   pl.BlockSpec((nsub, None, None, 2, LANES), lambda b, h: (b, layer, h, 0, 0))]
    in_specs = [pl.BlockSpec((rb, dk), lambda b, h: (b0 + b, h)),
                pl.BlockSpec((rb, dk), lambda b, h: (b0 + b, heads + h)),
                pl.BlockSpec((rb, dv), lambda b, h: (b0 + b, 2 * hk // dv + h)),
                pl.BlockSpec((rb, dv), lambda b, h: (b0 + b, h)),
                pl.BlockSpec((rb, LANES), lambda b, h: (b0 + b, heads * dv // LANES + h)),
                pl.BlockSpec((1, LANES), lambda b, h: (0, h)),
                pl.BlockSpec((1, dv), lambda b, h: (0, h))]
    args = [qkv, qkv, qkv, og, og, gate_b, norm_g.reshape(1, heads * dv)]
    if not zero_init:
        in_specs += state_specs
        args += list(state)
    in_specs.append(pl.BlockSpec(memory_space=pl.ANY))
    args.append(y_prev)
    aliases = {len(args) - 1: 0}
    out_specs = [pl.BlockSpec((rb, dv), lambda b, h: (b0 + b, h))]
    out_shape = [jax.ShapeDtypeStruct(y_prev.shape, y_prev.dtype)]
    all_slots = want_state and state_prev is None
    if want_state:
        if all_slots:
            out_specs += [pl.BlockSpec((nsub, n_layers, 2, None, dk, dv), lambda b, h: (b, 0, 0, h, 0, 0)),
                          pl.BlockSpec((nsub, n_layers, 2, None, 1, dk), lambda b, h: (b, 0, 0, h, 0, 0)),
                          pl.BlockSpec((nsub, n_layers, None, 2, LANES), lambda b, h: (b, 0, h, 0, 0))]
        else:
            out_specs += state_specs
        out_shape += [jax.ShapeDtypeStruct((batch, n_layers, 2, heads, dk, dv), F32),
                      jax.ShapeDtypeStruct((batch, n_layers, 2, heads, 1, dk), F32),
                      jax.ShapeDtypeStruct((batch, n_layers, heads, 2, LANES), F32)]
        if state_prev is not None:
            for k_out, arr in enumerate(state_prev):
                in_specs.append(pl.BlockSpec(memory_space=pl.ANY))
                args.append(arr)
                aliases[len(args) - 1] = 1 + k_out
    return pl.pallas_call(
        functools.partial(_mlstm_kernel, seq=seq, nsub=nsub, chunk=L, zero_init=zero_init,
                          want_state=want_state, n_alias=len(aliases), all_slots=all_slots, layer=layer),
        grid=(batch // nsub, heads),
        in_specs=in_specs,
        out_specs=out_specs,
        out_shape=out_shape,
        input_output_aliases=aliases,
        scratch_shapes=[pltpu.VMEM((rb, dv), F32), pltpu.VMEM((rb, dv), F32),
                        pltpu.VMEM((nsub, 2, dk, dv), F32), pltpu.VMEM((nsub, 2, 1, dk), F32)],
        compiler_params=_cparams("parallel", "parallel"),
        name="mlstm_core",
    )(*args)


def _dft_mats(seq, gc):
    def cs(n):
        idx = np.arange(n)
        ang = 2.0 * np.pi * ((idx[:, None] * idx[None, :]) % n) / n
        return np.cos(ang) / math.sqrt(n), np.sin(ang) / math.sqrt(n)
    ct, st = cs(seq)
    cc, sc = cs(gc)
    w_time = np.concatenate([ct, -st], axis=1)
    w_chan = np.concatenate([cc, sc], axis=1)
    return jnp.asarray(w_time, BF16), jnp.asarray(w_chan, BF16)


def _fnet_kernel(x_ref, g_ref, mod_ref, wt_ref, wc_ref, _, o_ref, *, seq, nsub, gc):
    h = _norm_mod(x_ref[...], g_ref[...], mod_ref[0:1, :], mod_ref[1:2, :]).astype(BF16)
    for grp in range(h.shape[1] // gc):
        cols = slice(grp * gc, (grp + 1) * gc)
        y = jnp.dot(h[:, cols], wc_ref[...], preferred_element_type=F32)
        for sb in range(nsub):
            ys = y[sb * seq:(sb + 1) * seq]
            ycat = jnp.concatenate([ys[:, :gc], ys[:, gc:]], axis=0).astype(BF16)
            o_ref[sb * seq:(sb + 1) * seq, cols] = jnp.dot(wt_ref[...], ycat,
                                                           preferred_element_type=F32).astype(o_ref.dtype)


def _fnet_mix(x, g, mods, y_prev, *, row0, batch, seq, nsub, rows_all, rows_ctx, rows_per_sample):
    d = x.shape[1]
    gc = d // FNET_GROUPS
    rb = nsub * seq
    assert row0 % rb == 0 and batch % nsub == 0 and x.shape[0] == rows_all
    b0 = row0 // rb
    cond = functools.partial(_cond_of_block, tm=rb, rows_ctx=rows_ctx, rows_per_sample=rows_per_sample)
    w_time, w_chan = _dft_mats(seq, gc)
    in_specs = [pl.BlockSpec((rb, d), lambda b: (b0 + b, 0)),
                pl.BlockSpec((1, d), lambda b: (0, 0)),
                pl.BlockSpec((None, 6, d), lambda b: (cond(b0 + b), 0, 0)),
                pl.BlockSpec((seq, 2 * seq), lambda b: (0, 0)),
                pl.BlockSpec((gc, 2 * gc), lambda b: (0, 0)),
                pl.BlockSpec(memory_space=pl.ANY)]
    args = [x, g.reshape(1, d), mods, w_time, w_chan, y_prev]
    return pl.pallas_call(
        functools.partial(_fnet_kernel, seq=seq, nsub=nsub, gc=gc),
        grid=(batch // nsub,),
        in_specs=in_specs,
        out_specs=pl.BlockSpec((rb, d), lambda b: (b0 + b, 0)),
        out_shape=jax.ShapeDtypeStruct(y_prev.shape, y_prev.dtype),
        input_output_aliases={len(args) - 1: 0},
        compiler_params=_cparams("parallel"),
        name="fnet_mix",
    )(*args)


def _lru_kernel(*refs, seq, nsub, zero_init, want_state):
    it = iter(refs)
    br_ref, xb_ref, cw_ref, cb_ref, gw_ref, gbias_ref, lam_ref = (next(it) for _ in range(7))
    if not zero_init:
        h0_ref = next(it)
    next(it)
    y_ref = next(it)
    if want_state:
        hl_ref = next(it)
    a_ref, u_ref, hs_ref = (next(it) for _ in range(3))
    T = seq
    rows_blk = nsub * T
    S = SUBLANES
    x = xb_ref[...]
    tt = lax.broadcasted_iota(jnp.int32, x.shape, 0) % T
    xm1 = jnp.where(tt >= 1, pltpu.roll(x, 1, 0), 0.0)
    xp1 = jnp.where(tt < T - 1, pltpu.roll(x, rows_blk - 1, 0), 0.0)
    xp2 = jnp.where(tt < T - 2, pltpu.roll(x, rows_blk - 2, 0), 0.0)
    xc = (cw_ref[0:1, :] * xm1 + cw_ref[1:2, :] * x + cw_ref[2:3, :] * xp1 + cw_ref[3:4, :] * xp2
          + cb_ref[...])
    xcb = xc.astype(BF16)
    r8 = lax.broadcasted_iota(jnp.int32, (rows_blk // S, S, x.shape[1]), 1)
    for d in range(2):
        pre_r = jnp.dot(xcb, gw_ref[d, 0], preferred_element_type=F32) + gbias_ref[2 * d:2 * d + 1, :]
        pre_i = jnp.dot(xcb, gw_ref[d, 1], preferred_element_type=F32) + gbias_ref[2 * d + 1:2 * d + 2, :]
        log_a = LRU_C * _sigmoid(pre_r) * _log_sigmoid(lam_ref[d:d + 1, :])
        a = jnp.exp(log_a)
        u = jnp.sqrt(-jnp.tanh(log_a) * (a * a + 1.0)) * (_sigmoid(pre_i) * xc)
        a = a.reshape(rows_blk // S, S, a.shape[1])
        u = u.reshape(a.shape)
        s = 1
        while s < S:
            shift, valid = (s, r8 >= s) if d == 0 else (S - s, r8 < S - s)
            u = jnp.where(valid, a * pltpu.roll(u, shift, 1) + u, u)
            a = jnp.where(valid, a * pltpu.roll(a, shift, 1), a)
            s *= 2
        a_ref[d] = a.reshape(x.shape)
        u_ref[d] = u.reshape(x.shape)

    if zero_init:
        h_init = tuple(jnp.zeros((1, x.shape[1]), F32) for _ in range(2 * nsub))
    else:
        h_init = tuple(h0_ref[sb, d:d + 1, :] for sb in range(nsub) for d in range(2))
    groups = T // S

    def group(i, carry):
        out = []
        for sb in range(nsub):
            rf = pl.ds(pl.multiple_of((sb * groups + i) * S, S), S)
            h_f = a_ref[0, rf, :] * carry[2 * sb] + u_ref[0, rf, :]
            hs_ref[0, rf, :] = h_f
            rb = pl.ds(pl.multiple_of((sb * groups + groups - 1 - i) * S, S), S)
            h_b = a_ref[1, rb, :] * carry[2 * sb + 1] + u_ref[1, rb, :]
            hs_ref[1, rb, :] = h_b
            out += [h_f[S - 1:S, :], h_b[0:1, :]]
        return tuple(out)

    h_last = lax.fori_loop(0, groups, group, h_init, unroll=4)
    y_ref[...] = ((hs_ref[0] + hs_ref[1]) * _gelu_tanh(br_ref[...])).astype(y_ref.dtype)
    if want_state:
        for sb in range(nsub):
            for d in range(2):
                hl_ref[sb, d:d + 1, :] = h_last[2 * sb + d]


def _lru_core(gx, conv_w, conv_b, gate_w, gate_b, lam, h0, y_prev, *, row0, batch, seq, nsub, want_state):
    width = conv_w.shape[1]
    nb = gate_w.shape[2]
    bs = width // nb
    rb = nsub * seq
    assert row0 % rb == 0 and batch % nsub == 0
    b0 = row0 // rb
    zero_init = h0 is None
    in_specs = [pl.BlockSpec((rb, bs), lambda b, n: (b0 + b, n)),
                pl.BlockSpec((rb, bs), lambda b, n: (b0 + b, nb + n)),
                pl.BlockSpec((conv_w.shape[0], bs), lambda b, n: (0, n)),
                pl.BlockSpec((1, bs), lambda b, n: (0, n)),
                pl.BlockSpec((2, 2, None, bs, bs), lambda b, n: (0, 0, n, 0, 0)),
                pl.BlockSpec((4, bs), lambda b, n: (0, n)),
                pl.BlockSpec((2, bs), lambda b, n: (0, n))]
    args = [gx, gx, conv_w, conv_b.reshape(1, width), gate_w, gate_b.reshape(4, width), lam]
    if not zero_init:
        in_specs.append(pl.BlockSpec((nsub, 2, bs), lambda b, n: (b, 0, n)))
        args.append(h0)
    in_specs.append(pl.BlockSpec(memory_space=pl.ANY))
    args.append(y_prev)
    aliases = {len(args) - 1: 0}
    out_specs = [pl.BlockSpec((rb, bs), lambda b, n: (b0 + b, n))]
    out_shape = [jax.ShapeDtypeStruct(y_prev.shape, y_prev.dtype)]
    if want_state:
        out_specs.append(pl.BlockSpec((nsub, 2, bs), lambda b, n: (b, 0, n)))
        out_shape.append(jax.ShapeDtypeStruct((batch, 2, width), F32))
    return pl.pallas_call(
        functools.partial(_lru_kernel, seq=seq, nsub=nsub, zero_init=zero_init, want_state=want_state),
        grid=(batch // nsub, nb),
        in_specs=in_specs,
        out_specs=out_specs,
        out_shape=out_shape,
        input_output_aliases=aliases,
        scratch_shapes=[pltpu.VMEM((2, rb, bs), F32)] * 3,
        compiler_params=_cparams("parallel", "parallel"),
        name="lru_core",
    )(*args)


def _mlstm_weights(w_in, b_gate, heads, dk, dv):
    nl, d, _ = w_in.shape
    hk, hv = heads * dk, heads * dv
    wq, wk, wv, wo, wg = jnp.split(w_in, [hk, 2 * hk, 2 * hk + hv, 2 * hk + 2 * hv], axis=2)
    w_qkv = jnp.concatenate([wq.astype(BF16), (wk * (dk ** -0.5)).astype(BF16), wv.astype(BF16)], axis=2)
    wg = wg.reshape(nl, d, 2, 2, heads).transpose(0, 1, 4, 2, 3).reshape(nl, d, heads, 4)
    wg = jnp.pad(wg, ((0, 0), (0, 0), (0, 0), (0, LANES - 4))).reshape(nl, d, heads * LANES)
    w_og = jnp.concatenate([wo.astype(BF16), wg.astype(BF16)], axis=2)
    bg = b_gate.astype(F32).transpose(0, 3, 1, 2).reshape(nl, heads, 4)
    bg = jnp.pad(bg, ((0, 0), (0, 0), (0, LANES - 4))).reshape(nl, 1, heads * LANES)
    return w_qkv, w_og, bg


def kernel(x_prompt, x_sample, c, state_mlstm_C, state_mlstm_n, state_mlstm_m, state_lru_h, c_ctx,
           mod_w, mod_b, norm_g, ffn_w_up, ffn_w_down, mlstm_w_in, mlstm_b_gate, mlstm_norm_g, mlstm_w_out,
           fnet_w_out, fnet_b_out, lru_w_in, lru_conv_w, lru_conv_b, lru_gate_w, lru_gate_b, lru_lambda,
           lru_w_out):
    bp, tp, d = x_prompt.shape
    bs_, ts, _ = x_sample.shape
    depth = mod_w.shape[0]
    heads = mlstm_b_gate.shape[-1]
    dk, dv = state_mlstm_C.shape[-2:]
    rows_ctx = bp * tp
    rows_all = rows_ctx + bs_ * ts
    rows = dict(rows_all=rows_all, rows_ctx=rows_ctx, rows_per_sample=ts)
    assert 1 + bs_ <= N_COND
    tm = math.gcd(ROWS_PER_STEP, math.gcd(rows_ctx, ts))
    tm_ffn = math.gcd(ROWS_PER_STEP_FFN, math.gcd(rows_ctx, ts))

    x_parts = [(x_prompt.reshape(rows_ctx, d), 0), (x_sample.reshape(bs_ * ts, d), rows_ctx)]
    cond = jnp.zeros((N_COND, d), F32).at[0].set(c_ctx).at[1:1 + bs_].set(c)
    mods_all = _modulation(cond, mod_w, mod_b).reshape(depth, N_COND, 6, d)
    zero_bias = jnp.zeros((d,), F32)
    trunks = ((0, bp, tp, True), (rows_ctx, bs_, ts, False))

    n_ml = mlstm_w_in.shape[0]
    w_qkv, w_og, bg = _mlstm_weights(mlstm_w_in, mlstm_b_gate, heads, dk, dv)
    mlstm_state_in = (state_mlstm_C, state_mlstm_n.reshape(bs_, n_ml, 2, heads, 1, dk),
                      jnp.broadcast_to(state_mlstm_m.transpose(0, 1, 3, 2)[..., None], (bs_, n_ml, heads, 2, LANES)))
    w_out_of = (mlstm_w_out.astype(BF16), fnet_w_out.astype(BF16), lru_w_out.astype(BF16))
    lru_w_in_b, lru_gate_w_b = lru_w_in.astype(BF16), lru_gate_w.astype(BF16)
    d_ff = ffn_w_up.shape[2]
    tf = math.gcd(d_ff, FFN_TILE)
    ffn_w_up_b = ffn_w_up.reshape(depth, d, d_ff // tf, tf).transpose(0, 2, 1, 3).astype(BF16)

    assert heads * dv == d and lru_conv_w.shape[2] == d
    y = jnp.zeros((rows_all, d), BF16)
    mlstm_state_out, new_h = None, []
    for i in range(depth):
        mods = mods_all[i]
        kind, j = i % 3, i // 3
        b_out = zero_bias
        if kind == 0:
            pm = functools.partial(_norm_mod_matmul, x_parts, norm_g[i, 0], mods, sh=0, sc=1, tm=tm, **rows)
            qkv = pm(w_qkv, j, out_dtype=BF16)
            og = pm(w_og, j, out_dtype=F32)
            for row0, nb, seq, is_ctx in trunks:
                nsub = math.gcd(nb, max(1, ts // seq))
                res = _mlstm_core(qkv, og, bg[j], mlstm_norm_g[j], None if is_ctx else mlstm_state_in, y,
                                  mlstm_state_out if is_ctx else None, layer=j, n_layers=n_ml, row0=row0,
                                  batch=nb, seq=seq, nsub=nsub, heads=heads, dk=dk, dv=dv, want_state=is_ctx)
                y = res[0]
                if is_ctx:
                    mlstm_state_out = res[1:]
        elif kind == 1:
            for row0, nb, seq, _ in trunks:
                y = _fnet_mix(x_parts[0][0], norm_g[i, 0], mods, y, row0=row0, batch=nb, seq=seq,
                              nsub=math.gcd(nb, max(1, ts // seq)), **rows)
            b_out = fnet_b_out[j]
        else:
            gx = _norm_mod_matmul(x_parts, norm_g[i, 0], mods, lru_w_in_b, j, sh=0, sc=1, out_dtype=F32, tm=tm,
                                  **rows)
            for row0, nb, seq, is_ctx in trunks:
                res = _lru_core(gx, lru_conv_w[j], lru_conv_b[j], lru_gate_w_b[j], lru_gate_b[j],
                                lru_lambda[j], None if is_ctx else state_lru_h[:, j], y,
                                row0=row0, batch=nb, seq=seq, nsub=math.gcd(nb, max(1, ts // seq)),
                                want_state=is_ctx)
                y = res[0]
                if is_ctx:
                    new_h.append(res[1])
        x = _matmul_residual(y, w_out_of[kind], j, b_out, x_parts, norm_g[i, 1], mods, gate=2, tm=tm, **rows)
        ffn = functools.partial(_ffn, x, norm_g[i, 2], norm_g[i, 3], mods, ffn_w_up_b, ffn_w_down, i,
                                tm=tm_ffn, rows_ctx=rows_ctx, rows_per_sample=ts)
        if i < depth - 1:
            x_parts = [(ffn(row0=0, nrows=rows_all), 0)]
        else:
            y_prompt = ffn(row0=0, nrows=rows_ctx).reshape(bp, tp, d)
            y_sample = ffn(row0=rows_ctx, nrows=rows_all - rows_ctx).reshape(bs_, ts, d)

    new_c, new_n, new_m = mlstm_state_out
    return (y_prompt, y_sample, new_c, new_n.reshape(bp, n_ml, 2, heads, dk),
            new_m[..., 0].transpose(0, 1, 3, 2), jnp.stack(new_h, 1))
```

```python
import functools
import math

import numpy as np
import jax
import jax.numpy as jnp
from jax import lax
from jax.experimental import pallas as pl
from jax.experimental.pallas import tpu as pltpu

F32 = jnp.float32
BF16 = jnp.bfloat16
EPS = 1e-6
FNET_GROUPS = 8
LRU_C = 8.0
LANES = 128
MLSTM_L = 256
SUBLANES = 8
VMEM_LIMIT = 56 * 1024 * 1024
VMEM_LIMIT_FFN = 60 * 1024 * 1024
N_COND = 16
ROWS_PER_STEP = 512
ROWS_PER_STEP_FFN = 1024
FFN_TILE = 512
MOD_TILE_K = 256
MOD_STREAMS = 4


def _cparams(*sem, vmem_limit=VMEM_LIMIT):
    return pltpu.CompilerParams(dimension_semantics=sem, vmem_limit_bytes=vmem_limit)


def _sigmoid(x):
    return 0.5 * jnp.tanh(0.5 * x) + 0.5


def _log_sigmoid(x):
    return jnp.minimum(x, 0.0) - jnp.log1p(jnp.exp(-jnp.abs(x)))


def _gelu_tanh(x):
    return 0.5 * x * (1.0 + jnp.tanh(math.sqrt(2.0 / math.pi) * (x + 0.044715 * (x * x * x))))


def _rms(x):
    return x * lax.rsqrt(jnp.mean(x * x, axis=-1, keepdims=True) + EPS)


def _norm_mod(x, g, shift, scale):
    return (_rms(x) * g) * (1.0 + scale) + shift


def _cond_of_block(i, tm, rows_ctx, rows_per_sample):
    r = i * tm
    return jnp.where(r < rows_ctx, 0, 1 + (r - rows_ctx) // rows_per_sample)


def _mod_kernel(c_ref, *refs):
    w_refs, b_ref, o_ref = refs[:MOD_STREAMS], refs[MOD_STREAMS], refs[MOD_STREAMS + 1]

    @pl.when(pl.program_id(1) == 0)
    def _():
        o_ref[...] = jnp.broadcast_to(b_ref[...], o_ref.shape)

    c = c_ref[...]
    a = (c * _sigmoid(c)).astype(BF16)
    slab = o_ref.shape[1] // MOD_STREAMS
    for s, w_ref in enumerate(w_refs):
        o_ref[:, s * slab:(s + 1) * slab] += jnp.dot(a, w_ref[...].astype(BF16), preferred_element_type=F32)


def _modulation(cond, mod_w, mod_b):
    depth, d, n = mod_w.shape
    tk = math.gcd(d, MOD_TILE_K)
    slab = n // MOD_STREAMS
    assert slab % LANES == 0
    w_specs = [pl.BlockSpec((None, tk, slab), lambda l, k, s=s: (l, k, s)) for s in range(MOD_STREAMS)]
    return pl.pallas_call(
        _mod_kernel,
        grid=(depth, d // tk),
        in_specs=[pl.BlockSpec((N_COND, tk), lambda l, k: (0, k))] + w_specs + [
            pl.BlockSpec((None, 1, n), lambda l, k: (l, 0, 0))],
        out_specs=pl.BlockSpec((None, N_COND, n), lambda l, k: (l, 0, 0)),
        out_shape=jax.ShapeDtypeStruct((depth, N_COND, n), F32),
        compiler_params=_cparams("parallel", "arbitrary"),
        name="modulation",
    )(cond, *([mod_w] * MOD_STREAMS), mod_b.reshape(depth, 1, n))


def _piece_specs(x_parts, tm, d):
    specs = []
    for x, row0 in x_parts:
        i0, nblk = row0 // tm, x.shape[0] // tm
        specs.append(pl.BlockSpec((tm, d), lambda i, i0=i0, nblk=nblk: (jnp.clip(i - i0, 0, nblk - 1), 0)))
    return specs


def _for_piece(x_refs, first_blocks, fn):
    if len(x_refs) == 1:
        fn(x_refs[0])
        return
    i = pl.program_id(0)
    bounds = list(first_blocks[1:]) + [None]
    for x_ref, lo, hi in zip(x_refs, first_blocks, bounds):
        pl.when((i >= lo) if hi is None else ((i >= lo) & (i < hi)))(functools.partial(fn, x_ref))


def _pm_kernel(*refs, sh, sc, first_blocks):
    n_x = len(first_blocks)
    g_ref, mod_ref, w_ref, o_ref = refs[n_x:]

    def project(x_ref):
        h = _norm_mod(x_ref[...], g_ref[...], mod_ref[sh:sh + 1, :], mod_ref[sc:sc + 1, :]).astype(BF16)
        o_ref[...] = jnp.dot(h, w_ref[...], preferred_element_type=F32).astype(o_ref.dtype)

    _for_piece(refs[:n_x], first_blocks, project)


def _norm_mod_matmul(x_parts, g, mods, w, layer, *, sh, sc, out_dtype, tm, rows_all, rows_ctx, rows_per_sample):
    d, n = w.shape[1:]
    cond = functools.partial(_cond_of_block, tm=tm, rows_ctx=rows_ctx, rows_per_sample=rows_per_sample)
    return pl.pallas_call(
        functools.partial(_pm_kernel, sh=sh, sc=sc, first_blocks=tuple(r // tm for _, r in x_parts)),
        grid=(rows_all // tm,),
        in_specs=_piece_specs(x_parts, tm, d) + [
            pl.BlockSpec((1, d), lambda i: (0, 0)),
            pl.BlockSpec((None, 6, d), lambda i: (cond(i), 0, 0)),
            pl.BlockSpec((None, d, n), lambda i: (layer, 0, 0), pipeline_mode=pl.Buffered(1))],
        out_specs=pl.BlockSpec((tm, n), lambda i: (i, 0)),
        out_shape=jax.ShapeDtypeStruct((rows_all, n), out_dtype),
        compiler_params=_cparams("parallel"),
        name="norm_mod_matmul",
    )(*[x for x, _ in x_parts], g.reshape(1, d), mods, w)


def _me_kernel(*refs, gate, first_blocks):
    n_x = len(first_blocks)
    y_ref, w_ref, b_ref, g_ref, mod_ref, o_ref = refs[n_x:]

    def residual(x_ref):
        out = jnp.dot(y_ref[...], w_ref[...], preferred_element_type=F32) + b_ref[...]
        o_ref[...] = x_ref[...] + mod_ref[gate:gate + 1, :] * (_rms(out) * g_ref[...])

    _for_piece(refs[:n_x], first_blocks, residual)


def _matmul_residual(y, w, layer, b, x_parts, g, mods, *, gate, tm, rows_all, rows_ctx, rows_per_sample):
    k, d = w.shape[1:]
    cond = functools.partial(_cond_of_block, tm=tm, rows_ctx=rows_ctx, rows_per_sample=rows_per_sample)
    return pl.pallas_call(
        functools.partial(_me_kernel, gate=gate, first_blocks=tuple(r // tm for _, r in x_parts)),
        grid=(rows_all // tm,),
        in_specs=_piece_specs(x_parts, tm, d) + [
            pl.BlockSpec((tm, k), lambda i: (i, 0)),
            pl.BlockSpec((None, k, d), lambda i: (layer, 0, 0)),
            pl.BlockSpec((1, d), lambda i: (0, 0)),
            pl.BlockSpec((1, d), lambda i: (0, 0)),
            pl.BlockSpec((None, 6, d), lambda i: (cond(i), 0, 0))],
        out_specs=pl.BlockSpec((tm, d), lambda i: (i, 0)),
        out_shape=jax.ShapeDtypeStruct((rows_all, d), F32),
        compiler_params=_cparams("parallel"),
        name="matmul_residual",
    )(*[x for x, _ in x_parts], y, w, b.reshape(1, d), g.reshape(1, d), mods)


def _ffn_kernel(x_ref, g1_ref, g2_ref, mod_ref, wu_ref, wd_ref, o_ref, h_ref):
    j = pl.program_id(1)

    @pl.when(j == 0)
    def _():
        h_ref[...] = _norm_mod(x_ref[...], g1_ref[...], mod_ref[3:4, :], mod_ref[4:5, :]).astype(BF16)
        o_ref[...] = jnp.zeros_like(o_ref)

    u = jnp.maximum(jnp.dot(h_ref[...], wu_ref[...], preferred_element_type=F32), 0.0)
    o_ref[...] += jnp.dot((u * u).astype(BF16), wd_ref[...].astype(BF16), preferred_element_type=F32)

    @pl.when(j == pl.num_programs(1) - 1)
    def _():
        o_ref[...] = x_ref[...] + mod_ref[5:6, :] * (_rms(o_ref[...]) * g2_ref[...])


def _ffn(x, g_pre, g_post, mods, w_up, w_down, layer, *, row0, nrows, tm, rows_ctx, rows_per_sample):
    d = x.shape[1]
    _, nf, _, tf = w_up.shape
    i0 = row0 // tm
    cond = functools.partial(_cond_of_block, tm=tm, rows_ctx=rows_ctx, rows_per_sample=rows_per_sample)
    return pl.pallas_call(
        _ffn_kernel,
        grid=(nrows // tm, nf),
        in_specs=[pl.BlockSpec((tm, d), lambda i, j: (i0 + i, 0)),
                  pl.BlockSpec((1, d), lambda i, j: (0, 0)),
                  pl.BlockSpec((1, d), lambda i, j: (0, 0)),
                  pl.BlockSpec((None, 6, d), lambda i, j: (cond(i0 + i), 0, 0)),
                  pl.BlockSpec((None, None, d, tf), lambda i, j: (layer, j, 0, 0)),
                  pl.BlockSpec((None, tf, d), lambda i, j: (layer, j, 0))],
        out_specs=pl.BlockSpec((tm, d), lambda i, j: (i, 0)),
        out_shape=jax.ShapeDtypeStruct((nrows, d), F32),
        scratch_shapes=[pltpu.VMEM((tm, d), BF16)],
        compiler_params=_cparams("parallel", "arbitrary", vmem_limit=VMEM_LIMIT_FFN),
        name="ffn",
    )(x, g_pre.reshape(1, d), g_post.reshape(1, d), mods, w_up, w_down)


def _mlstm_kernel(*refs, seq, nsub, chunk, zero_init, want_state, n_alias, all_slots, layer):
    it = iter(refs)
    q_ref, k_ref, v_ref, o_ref, gt_ref, gb_ref, ng_ref = (next(it) for _ in range(7))
    if not zero_init:
        c0_ref, n0_ref, m0_ref = next(it), next(it), next(it)
    for _ in range(n_alias):
        next(it)
    y_ref = next(it)
    if want_state:
        co_ref, no_ref, mo_ref = next(it), next(it), next(it)
    hf_ref, hb_ref, c_ref, n_ref = (next(it) for _ in range(4))
    L = chunk
    nc = seq // L

    if zero_init:
        c_ref[...] = jnp.zeros_like(c_ref)
        n_ref[...] = jnp.zeros_like(n_ref)
        m_state = [[jnp.zeros((1, 1), F32), jnp.zeros((1, 1), F32)] for _ in range(nsub)]
    else:
        c_ref[...] = c0_ref[...]
        n_ref[...] = n0_ref[...]
        m_state = [[m0_ref[sb, 0:1, 0:1], m0_ref[sb, 1:2, 0:1]] for sb in range(nsub)]

    row = lax.broadcasted_iota(jnp.int32, (L, L), 0)
    col = lax.broadcasted_iota(jnp.int32, (L, L), 1)
    masks = (col <= row, col >= row)
    lane = lax.broadcasted_iota(jnp.int32, (L, LANES), 1)
    is_forget = (lane % 2) == 1

    def rows_of(sb, c):
        return pl.ds(sb * seq + c * L, L)

    scan_order = (tuple(range(nc)), tuple(range(nc - 1, -1, -1)))
    units = [(sb, c, d) for step in range(nc) for sb in range(nsub) for d, c in ((0, step), (1, nc - 1 - step))]

    def log_gates(sb, c):
        gts = gt_ref[rows_of(sb, c), :] + gb_ref[...]
        x = jnp.where(is_forget, _log_sigmoid(gts), gts)
        def split3(z, axis):
            z_hi = z.astype(BF16)
            r_1 = z - z_hi.astype(F32)
            z_mid = r_1.astype(BF16)
            return jnp.concatenate([z_hi, z_mid, (r_1 - z_mid.astype(F32)).astype(BF16)], axis=axis)

        xt = x.T
        return x, xt, split3(x, 1), split3(xt, 0)

    chunk_gates = {(sb, c): log_gates(sb, c) for sb in range(nsub) for c in range(nc)}
    gate = {}
    for sb, c, d in units:
        x, xt, x3, xt3 = chunk_gates[sb, c]
        y3 = jnp.dot(masks[d].astype(BF16), x3, preferred_element_type=F32)
        y = (y3[:, :LANES] + y3[:, LANES:2 * LANES]) + y3[:, 2 * LANES:]
        yt3 = jnp.dot(xt3, masks[1 - d].astype(BF16), preferred_element_type=F32)
        yt = (yt3[:LANES] + yt3[LANES:2 * LANES]) + yt3[2 * LANES:]
        b_c = y[:, 2 * d + 1:2 * d + 2]
        a_c = x[:, 2 * d:2 * d + 1] - b_c
        a_r = xt[2 * d:2 * d + 1, :] - yt[2 * d + 1:2 * d + 2, :]
        am = jnp.where(masks[d], jnp.broadcast_to(a_r, (L, L)), -jnp.inf)
        b_last = b_c[L - 1:L, :] if d == 0 else b_c[0:1, :]
        g_c = b_last + a_c
        gate[sb, c, d] = (am, b_c + jnp.max(am, axis=1, keepdims=True), b_c, b_last, g_c,
                          jnp.max(g_c, axis=0, keepdims=True))

    m_in, m_out = {}, {}
    for sb in range(nsub):
        for d in range(2):
            m = m_state[sb][d]
            for c in scan_order[d]:
                _, _, _, b_last, _, g_max = gate[sb, c, d]
                m_in[sb, c, d] = m
                m = jnp.maximum(b_last + m, g_max)
                m_out[sb, c, d] = m
            m_state[sb][d] = m

    part = {}
    for sb, c, d in units:
        am, r_max, b_c, b_last, g_c, _ = gate[sb, c, d]
        rows = rows_of(sb, c)
        q = q_ref[rows, :]
        k = k_ref[rows, :]
        v = v_ref[rows, :]
        inter = b_c + m_in[sb, c, d]
        m_t = jnp.maximum(inter, r_max)
        s = (lax.dot_general(q, k, (((1,), (1,)), ((), ())), preferred_element_type=F32)
             * jnp.exp(am + (b_c - m_t)))
        kw = k.astype(F32) * jnp.exp(g_c - m_out[sb, c, d])
        sv = jnp.dot(s.astype(BF16), jnp.concatenate([v, jnp.ones((L, LANES), BF16)], axis=1),
                     preferred_element_type=F32)
        part[sb, c, d] = (sv[:, :v.shape[1]], sv[:, v.shape[1]:], jnp.exp(inter - m_t), jnp.exp(-m_t),
                          jnp.exp(b_last + m_in[sb, c, d] - m_out[sb, c, d]), kw.astype(BF16),
                          jnp.sum(kw, axis=0, keepdims=True))

    for sb, c, d in units:
        sv, s_sum, w_inter, den_floor, w_c, kw, kw_sum = part[sb, c, d]
        rows = rows_of(sb, c)
        kv = lax.dot_general(kw, v_ref[rows, :], (((0,), (0,)), ((), ())), preferred_element_type=F32)
        if zero_init and c == scan_order[d][0]:
            num, den = sv, s_sum
            c_ref[sb, d] = kv
            n_ref[sb, d] = kw_sum
        else:
            q = q_ref[rows, :]
            c_old = c_ref[sb, d]
            n_old = n_ref[sb, d]
            num = sv + w_inter * jnp.dot(q, c_old.astype(BF16), preferred_element_type=F32)
            den = s_sum + w_inter * jnp.sum(q.astype(F32) * n_old, axis=1, keepdims=True)
            c_ref[sb, d] = w_c * c_old + kv
            n_ref[sb, d] = w_c * n_old + kw_sum
        inv = 1.0 / jnp.maximum(jnp.abs(den), den_floor)
        (hf_ref, hb_ref)[d][rows, :] = num * jnp.tile(inv, (1, num.shape[1] // LANES))

    def finish(c, _):
        rows = pl.ds(pl.multiple_of(c * L, L), L)
        hs = hf_ref[rows, :] + hb_ref[rows, :]
        y_ref[rows, :] = (_rms(hs) * ng_ref[...] * _sigmoid(o_ref[rows, :])).astype(y_ref.dtype)
        return 0

    lax.fori_loop(0, nsub * nc, finish, 0)

    if want_state:
        if all_slots:
            for ref in (co_ref, no_ref, mo_ref):
                ref[...] = jnp.zeros_like(ref)
            co_ref, no_ref, mo_ref = co_ref.at[:, layer], no_ref.at[:, layer], mo_ref.at[:, layer]
        co_ref[...] = c_ref[...]
        no_ref[...] = n_ref[...]
        for sb in range(nsub):
            for d in range(2):
                mo_ref[sb, d:d + 1, :] = jnp.broadcast_to(m_state[sb][d], (1, LANES))


def _mlstm_core(qkv, og, gate_b, norm_g, state, y_prev, state_prev, *, layer, n_layers, row0, batch, seq, nsub,
                heads, dk, dv, want_state):
    L = math.gcd(MLSTM_L, seq)
    rb = nsub * seq
    assert row0 % rb == 0 and batch % nsub == 0
    b0 = row0 // rb
    hk = heads * dk
    zero_init = state is None
    state_specs = [pl.BlockSpec((nsub, None, 2, None, dk, dv), lambda b, h: (b, layer, 0, h, 0, 0)),
                   pl.BlockSpec((nsub, None, 2, None, 1, dk), lambda b, h: (b, layer, 0, h, 0, 0)),
                   pl.BlockSpec((nsub, None, None, 2, LANES), lambda b, h: (b, layer, h, 0, 0))]
    in_specs = [pl.BlockSpec((rb, dk), lambda b, h: (b0 + b, h)),
                pl.BlockSpec((rb, dk), lambda b, h: (b0 + b, heads + h)),
                pl.BlockSpec((rb, dv), lambda b, h: (b0 + b, 2 * hk // dv + h)),
                pl.BlockSpec((rb, dv), lambda b, h: (b0 + b, h)),
                pl.BlockSpec((rb, LANES), lambda b, h: (b0 + b, heads * dv // LANES + h)),
                pl.BlockSpec((1, LANES), lambda b, h: (0, h)),
                pl.BlockSpec((1, dv), lambda b, h: (0, h))]
    args = [qkv, qkv, qkv, og, og, gate_b, norm_g.reshape(1, heads * dv)]
    if not zero_init:
        in_specs += state_specs
        args += list(state)
    in_specs.append(pl.BlockSpec(memory_space=pl.ANY))
    args.append(y_prev)
    aliases = {len(args) - 1: 0}
    out_specs = [pl.BlockSpec((rb, dv), lambda b, h: (b0 + b, h))]
    out_shape = [jax.ShapeDtypeStruct(y_prev.shape, y_prev.dtype)]
    all_slots = want_state and state_prev is None
    if want_state:
        if all_slots:
            out_specs += [pl.BlockSpec((nsub, n_layers, 2, None, dk, dv), lambda b, h: (b, 0, 0, h, 0, 0)),
                          pl.BlockSpec((nsub, n_layers, 2, None, 1, dk), lambda b, h: (b, 0, 0, h, 0, 0)),
                          pl.BlockSpec((nsub, n_layers, None, 2, LANES), lambda b, h: (b, 0, h, 0, 0))]
        else:
            out_specs += state_specs
        out_shape += [jax.ShapeDtypeStruct((batch, n_layers, 2, heads, dk, dv), F32),
                      jax.ShapeDtypeStruct((batch, n_layers, 2, heads, 1, dk), F32),
                      jax.ShapeDtypeStruct((batch, n_layers, heads, 2, LANES), F32)]
        if state_prev is not None:
            for k_out, arr in enumerate(state_prev):
                in_specs.append(pl.BlockSpec(memory_space=pl.ANY))
                args.append(arr)
                aliases[len(args) - 1] = 1 + k_out
    return pl.pallas_call(
        functools.partial(_mlstm_kernel, seq=seq, nsub=nsub, chunk=L, zero_init=zero_init,
                          want_state=want_state, n_alias=len(aliases), all_slots=all_slots, layer=layer),
        grid=(batch // nsub, heads),
        in_specs=in_specs,
        out_specs=out_specs,
        out_shape=out_shape,
        input_output_aliases=aliases,
        scratch_shapes=[pltpu.VMEM((rb, dv), F32), pltpu.VMEM((rb, dv), F32),
                        pltpu.VMEM((nsub, 2, dk, dv), F32), pltpu.VMEM((nsub, 2, 1, dk), F32)],
        compiler_params=_cparams("parallel", "parallel"),
        name="mlstm_core",
    )(*args)


def _dft_mats(seq, gc):
    def cs(n):
        idx = np.arange(n)
        ang = 2.0 * np.pi * ((idx[:, None] * idx[None, :]) % n) / n
        return np.cos(ang) / math.sqrt(n), np.sin(ang) / math.sqrt(n)
    ct, st = cs(seq)
    cc, sc = cs(gc)
    w_time = np.concatenate([ct, -st], axis=1)
    w_chan = np.concatenate([cc, sc], axis=1)
    return jnp.asarray(w_time, BF16), jnp.asarray(w_chan, BF16)


def _fnet_kernel(x_ref, g_ref, mod_ref, wt_ref, wc_ref, _, o_ref, *, seq, nsub, gc):
    h = _norm_mod(x_ref[...], g_ref[...], mod_ref[0:1, :], mod_ref[1:2, :]).astype(BF16)
    for grp in range(h.shape[1] // gc):
        cols = slice(grp * gc, (grp + 1) * gc)
        y = jnp.dot(h[:, cols], wc_ref[...], preferred_element_type=F32)
        for sb in range(nsub):
            ys = y[sb * seq:(sb + 1) * seq]
            ycat = jnp.concatenate([ys[:, :gc], ys[:, gc:]], axis=0).astype(BF16)
            o_ref[sb * seq:(sb + 1) * seq, cols] = jnp.dot(wt_ref[...], ycat,
                                                           preferred_element_type=F32).astype(o_ref.dtype)


def _fnet_mix(x, g, mods, y_prev, *, row0, batch, seq, nsub, rows_all, rows_ctx, rows_per_sample):
    d = x.shape[1]
    gc = d // FNET_GROUPS
    rb = nsub * seq
    assert row0 % rb == 0 and batch % nsub == 0 and x.shape[0] == rows_all
    b0 = row0 // rb
    cond = functools.partial(_cond_of_block, tm=rb, rows_ctx=rows_ctx, rows_per_sample=rows_per_sample)
    w_time, w_chan = _dft_mats(seq, gc)
    in_specs = [pl.BlockSpec((rb, d), lambda b: (b0 + b, 0)),
                pl.BlockSpec((1, d), lambda b: (0, 0)),
                pl.BlockSpec((None, 6, d), lambda b: (cond(b0 + b), 0, 0)),
                pl.BlockSpec((seq, 2 * seq), lambda b: (0, 0)),
                pl.BlockSpec((gc, 2 * gc), lambda b: (0, 0)),
                pl.BlockSpec(memory_space=pl.ANY)]
    args = [x, g.reshape(1, d), mods, w_time, w_chan, y_prev]
    return pl.pallas_call(
        functools.partial(_fnet_kernel, seq=seq, nsub=nsub, gc=gc),
        grid=(batch // nsub,),
        in_specs=in_specs,
        out_specs=pl.BlockSpec((rb, d), lambda b: (b0 + b, 0)),
        out_shape=jax.ShapeDtypeStruct(y_prev.shape, y_prev.dtype),
        input_output_aliases={len(args) - 1: 0},
        compiler_params=_cparams("parallel"),
        name="fnet_mix",
    )(*args)


def _lru_kernel(*refs, seq, nsub, zero_init, want_state):
    it = iter(refs)
    br_ref, xb_ref, cw_ref, cb_ref, gw_ref, gbias_ref, lam_ref = (next(it) for _ in range(7))
    if not zero_init:
        h0_ref = next(it)
    next(it)
    y_ref = next(it)
    if want_state:
        hl_ref = next(it)
    a_ref, u_ref, hs_ref = (next(it) for _ in range(3))
    T = seq
    rows_blk = nsub * T
    S = SUBLANES
    x = xb_ref[...]
    tt = lax.broadcasted_iota(jnp.int32, x.shape, 0) % T
    xm1 = jnp.where(tt >= 1, pltpu.roll(x, 1, 0), 0.0)
    xp1 = jnp.where(tt < T - 1, pltpu.roll(x, rows_blk - 1, 0), 0.0)
    xp2 = jnp.where(tt < T - 2, pltpu.roll(x, rows_blk - 2, 0), 0.0)
    xc = (cw_ref[0:1, :] * xm1 + cw_ref[1:2, :] * x + cw_ref[2:3, :] * xp1 + cw_ref[3:4, :] * xp2
          + cb_ref[...])
    xcb = xc.astype(BF16)
    r8 = lax.broadcasted_iota(jnp.int32, (rows_blk // S, S, x.shape[1]), 1)
    for d in range(2):
        pre_r = jnp.dot(xcb, gw_ref[d, 0], preferred_element_type=F32) + gbias_ref[2 * d:2 * d + 1, :]
        pre_i = jnp.dot(xcb, gw_ref[d, 1], preferred_element_type=F32) + gbias_ref[2 * d + 1:2 * d + 2, :]
        log_a = LRU_C * _sigmoid(pre_r) * _log_sigmoid(lam_ref[d:d + 1, :])
        a = jnp.exp(log_a)
        u = jnp.sqrt(-jnp.tanh(log_a) * (a * a + 1.0)) * (_sigmoid(pre_i) * xc)
        a = a.reshape(rows_blk // S, S, a.shape[1])
        u = u.reshape(a.shape)
        s = 1
        while s < S:
            shift, valid = (s, r8 >= s) if d == 0 else (S - s, r8 < S - s)
            u = jnp.where(valid, a * pltpu.roll(u, shift, 1) + u, u)
            a = jnp.where(valid, a * pltpu.roll(a, shift, 1), a)
            s *= 2
        a_ref[d] = a.reshape(x.shape)
        u_ref[d] = u.reshape(x.shape)

    if zero_init:
        h_init = tuple(jnp.zeros((1, x.shape[1]), F32) for _ in range(2 * nsub))
    else:
        h_init = tuple(h0_ref[sb, d:d + 1, :] for sb in range(nsub) for d in range(2))
    groups = T // S

    def group(i, carry):
        out = []
        for sb in range(nsub):
            rf = pl.ds(pl.multiple_of((sb * groups + i) * S, S), S)
            h_f = a_ref[0, rf, :] * carry[2 * sb] + u_ref[0, rf, :]
            hs_ref[0, rf, :] = h_f
            rb = pl.ds(pl.multiple_of((sb * groups + groups - 1 - i) * S, S), S)
            h_b = a_ref[1, rb, :] * carry[2 * sb + 1] + u_ref[1, rb, :]
            hs_ref[1, rb, :] = h_b
            out += [h_f[S - 1:S, :], h_b[0:1, :]]
        return tuple(out)

    h_last = lax.fori_loop(0, groups, group, h_init, unroll=4)
    y_ref[...] = ((hs_ref[0] + hs_ref[1]) * _gelu_tanh(br_ref[...])).astype(y_ref.dtype)
    if want_state:
        for sb in range(nsub):
            for d in range(2):
                hl_ref[sb, d:d + 1, :] = h_last[2 * sb + d]


def _lru_core(gx, conv_w, conv_b, gate_w, gate_b, lam, h0, y_prev, *, row0, batch, seq, nsub, want_state):
    width = conv_w.shape[1]
    nb = gate_w.shape[2]
    bs = width // nb
    rb = nsub * seq
    assert row0 % rb == 0 and batch % nsub == 0
    b0 = row0 // rb
    zero_init = h0 is None
    in_specs = [pl.BlockSpec((rb, bs), lambda b, n: (b0 + b, n)),
                pl.BlockSpec((rb, bs), lambda b, n: (b0 + b, nb + n)),
                pl.BlockSpec((conv_w.shape[0], bs), lambda b, n: (0, n)),
                pl.BlockSpec((1, bs), lambda b, n: (0, n)),
                pl.BlockSpec((2, 2, None, bs, bs), lambda b, n: (0, 0, n, 0, 0)),
                pl.BlockSpec((4, bs), lambda b, n: (0, n)),
                pl.BlockSpec((2, bs), lambda b, n: (0, n))]
    args = [gx, gx, conv_w, conv_b.reshape(1, width), gate_w, gate_b.reshape(4, width), lam]
    if not zero_init:
        in_specs.append(pl.BlockSpec((nsub, 2, bs), lambda b, n: (b, 0, n)))
        args.append(h0)
    in_specs.append(pl.BlockSpec(memory_space=pl.ANY))
    args.append(y_prev)
    aliases = {len(args) - 1: 0}
    out_specs = [pl.BlockSpec((rb, bs), lambda b, n: (b0 + b, n))]
    out_shape = [jax.ShapeDtypeStruct(y_prev.shape, y_prev.dtype)]
    if want_state:
        out_specs.append(pl.BlockSpec((nsub, 2, bs), lambda b, n: (b, 0, n)))
        out_shape.append(jax.ShapeDtypeStruct((batch, 2, width), F32))
    return pl.pallas_call(
        functools.partial(_lru_kernel, seq=seq, nsub=nsub, zero_init=zero_init, want_state=want_state),
        grid=(batch // nsub, nb),
        in_specs=in_specs,
        out_specs=out_specs,
        out_shape=out_shape,
        input_output_aliases=aliases,
        scratch_shapes=[pltpu.VMEM((2, rb, bs), F32)] * 3,
        compiler_params=_cparams("parallel", "parallel"),
        name="lru_core",
    )(*args)


def _mlstm_weights(w_in, b_gate, heads, dk, dv):
    nl, d, _ = w_in.shape
    hk, hv = heads * dk, heads * dv
    wq, wk, wv, wo, wg = jnp.split(w_in, [hk, 2 * hk, 2 * hk + hv, 2 * hk + 2 * hv], axis=2)
    w_qkv = jnp.concatenate([wq.astype(BF16), (wk * (dk ** -0.5)).astype(BF16), wv.astype(BF16)], axis=2)
    wg = wg.reshape(nl, d, 2, 2, heads).transpose(0, 1, 4, 2, 3).reshape(nl, d, heads, 4)
    wg = jnp.pad(wg, ((0, 0), (0, 0), (0, 0), (0, LANES - 4))).reshape(nl, d, heads * LANES)
    w_og = jnp.concatenate([wo.astype(BF16), wg.astype(BF16)], axis=2)
    bg = b_gate.astype(F32).transpose(0, 3, 1, 2).reshape(nl, heads, 4)
    bg = jnp.pad(bg, ((0, 0), (0, 0), (0, LANES - 4))).reshape(nl, 1, heads * LANES)
    return w_qkv, w_og, bg


def kernel(x_prompt, x_sample, c, state_mlstm_C, state_mlstm_n, state_mlstm_m, state_lru_h, c_ctx,
           mod_w, mod_b, norm_g, ffn_w_up, ffn_w_down, mlstm_w_in, mlstm_b_gate, mlstm_norm_g, mlstm_w_out,
           fnet_w_out, fnet_b_out, lru_w_in, lru_conv_w, lru_conv_b, lru_gate_w, lru_gate_b, lru_lambda,
           lru_w_out):
    bp, tp, d = x_prompt.shape
    bs_, ts, _ = x_sample.shape
    depth = mod_w.shape[0]
    heads = mlstm_b_gate.shape[-1]
    dk, dv = state_mlstm_C.shape[-2:]
    rows_ctx = bp * tp
    rows_all = rows_ctx + bs_ * ts
    rows = dict(rows_all=rows_all, rows_ctx=rows_ctx, rows_per_sample=ts)
    assert 1 + bs_ <= N_COND
    tm = math.gcd(ROWS_PER_STEP, math.gcd(rows_ctx, ts))
    tm_ffn = math.gcd(ROWS_PER_STEP_FFN, math.gcd(rows_ctx, ts))

    x_parts = [(x_prompt.reshape(rows_ctx, d), 0), (x_sample.reshape(bs_ * ts, d), rows_ctx)]
    cond = jnp.zeros((N_COND, d), F32).at[0].set(c_ctx).at[1:1 + bs_].set(c)
    mods_all = _modulation(cond, mod_w, mod_b).reshape(depth, N_COND, 6, d)
    zero_bias = jnp.zeros((d,), F32)
    trunks = ((0, bp, tp, True), (rows_ctx, bs_, ts, False))

    n_ml = mlstm_w_in.shape[0]
    w_qkv, w_og, bg = _mlstm_weights(mlstm_w_in, mlstm_b_gate, heads, dk, dv)
    mlstm_state_in = (state_mlstm_C, state_mlstm_n.reshape(bs_, n_ml, 2, heads, 1, dk),
                      jnp.broadcast_to(state_mlstm_m.transpose(0, 1, 3, 2)[..., None], (bs_, n_ml, heads, 2, LANES)))
    w_out_of = (mlstm_w_out.astype(BF16), fnet_w_out.astype(BF16), lru_w_out.astype(BF16))
    lru_w_in_b, lru_gate_w_b = lru_w_in.astype(BF16), lru_gate_w.astype(BF16)
    d_ff = ffn_w_up.shape[2]
    tf = math.gcd(d_ff, FFN_TILE)
    ffn_w_up_b = ffn_w_up.reshape(depth, d, d_ff // tf, tf).transpose(0, 2, 1, 3).astype(BF16)

    assert heads * dv == d and lru_conv_w.shape[2] == d
    y = jnp.zeros((rows_all, d), BF16)
    mlstm_state_out, new_h = None, []
    for i in range(depth):
        mods = mods_all[i]
        kind, j = i % 3, i // 3
        b_out = zero_bias
        if kind == 0:
            pm = functools.partial(_norm_mod_matmul, x_parts, norm_g[i, 0], mods, sh=0, sc=1, tm=tm, **rows)
            qkv = pm(w_qkv, j, out_dtype=BF16)
            og = pm(w_og, j, out_dtype=F32)
            for row0, nb, seq, is_ctx in trunks:
                nsub = math.gcd(nb, max(1, ts // seq))
                res = _mlstm_core(qkv, og, bg[j], mlstm_norm_g[j], None if is_ctx else mlstm_state_in, y,
                                  mlstm_state_out if is_ctx else None, layer=j, n_layers=n_ml, row0=row0,
                                  batch=nb, seq=seq, nsub=nsub, heads=heads, dk=dk, dv=dv, want_state=is_ctx)
                y = res[0]
                if is_ctx:
                    mlstm_state_out = res[1:]
        elif kind == 1:
            for row0, nb, seq, _ in trunks:
                y = _fnet_mix(x_parts[0][0], norm_g[i, 0], mods, y, row0=row0, batch=nb, seq=seq,
                              nsub=math.gcd(nb, max(1, ts // seq)), **rows)
            b_out = fnet_b_out[j]
        else:
            gx = _norm_mod_matmul(x_parts, norm_g[i, 0], mods, lru_w_in_b, j, sh=0, sc=1, out_dtype=F32, tm=tm,
                                  **rows)
            for row0, nb, seq, is_ctx in trunks:
                res = _lru_core(gx, lru_conv_w[j], lru_conv_b[j], lru_gate_w_b[j], lru_gate_b[j],
                                lru_lambda[j], None if is_ctx else state_lru_h[:, j], y,
                                row0=row0, batch=nb, seq=seq, nsub=math.gcd(nb, max(1, ts // seq)),
                                want_state=is_ctx)
                y = res[0]
                if is_ctx:
                    new_h.append(res[1])
        x = _matmul_residual(y, w_out_of[kind], j, b_out, x_parts, norm_g[i, 1], mods, gate=2, tm=tm, **rows)
        ffn = functools.partial(_ffn, x, norm_g[i, 2], norm_g[i, 3], mods, ffn_w_up_b, ffn_w_down, i,
                                tm=tm_ffn, rows_ctx=rows_ctx, rows_per_sample=ts)
        if i < depth - 1:
            x_parts = [(ffn(row0=0, nrows=rows_all), 0)]
        else:
            y_prompt = ffn(row0=0, nrows=rows_ctx).reshape(bp, tp, d)
            y_sample = ffn(row0=rows_ctx, nrows=rows_all - rows_ctx).reshape(bs_, ts, d)

    new_c, new_n, new_m = mlstm_state_out
    return (y_prompt, y_sample, new_c, new_n.reshape(bp, n_ml, 2, heads, dk),
            new_m[..., 0].transpose(0, 1, 3, 2), jnp.stack(new_h, 1))
```

```python
import functools
import math

import numpy as np
import jax
import jax.numpy as jnp
from jax import lax
from jax.experimental import pallas as pl
from jax.experimental.pallas import tpu as pltpu

F32 = jnp.float32
BF16 = jnp.bfloat16
EPS = 1e-6
FNET_GROUPS = 8
LRU_C = 8.0
LANES = 128
MLSTM_L = 256
SUBLANES = 8
VMEM_LIMIT = 56 * 1024 * 1024
VMEM_LIMIT_FFN = 60 * 1024 * 1024
N_COND = 16
ROWS_PER_STEP = 512
ROWS_PER_STEP_FFN = 1024
FFN_TILE = 512
MOD_TILE_K = 256


def _cparams(*sem, vmem_limit=VMEM_LIMIT):
    return pltpu.CompilerParams(dimension_semantics=sem, vmem_limit_bytes=vmem_limit)


def _sigmoid(x):
    return 0.5 * jnp.tanh(0.5 * x) + 0.5


def _log_sigmoid(x):
    return jnp.minimum(x, 0.0) - jnp.log1p(jnp.exp(-jnp.abs(x)))


def _gelu_tanh(x):
    return 0.5 * x * (1.0 + jnp.tanh(math.sqrt(2.0 / math.pi) * (x + 0.044715 * (x * x * x))))


def _rms(x):
    return x * lax.rsqrt(jnp.mean(x * x, axis=-1, keepdims=True) + EPS)


def _norm_mod(x, g, shift, scale):
    return (_rms(x) * g) * (1.0 + scale) + shift


def _cond_of_block(i, tm, rows_ctx, rows_per_sample):
    r = i * tm
    return jnp.where(r < rows_ctx, 0, 1 + (r - rows_ctx) // rows_per_sample)


def _mod_kernel(c_ref, w_ref, b_ref, o_ref):
    @pl.when(pl.program_id(1) == 0)
    def _():
        o_ref[...] = jnp.broadcast_to(b_ref[...], o_ref.shape)

    c = c_ref[...]
    a = (c * _sigmoid(c)).astype(BF16)
    o_ref[...] += jnp.dot(a, w_ref[...].astype(BF16), preferred_element_type=F32)


def _modulation(cond, mod_w, mod_b):
    depth, d, n = mod_w.shape
    tk = math.gcd(d, MOD_TILE_K)
    return pl.pallas_call(
        _mod_kernel,
        grid=(depth, d // tk),
        in_specs=[pl.BlockSpec((N_COND, tk), lambda l, k: (0, k)),
                  pl.BlockSpec((None, tk, n), lambda l, k: (l, k, 0)),
                  pl.BlockSpec((None, 1, n), lambda l, k: (l, 0, 0))],
        out_specs=pl.BlockSpec((None, N_COND, n), lambda l, k: (l, 0, 0)),
        out_shape=jax.ShapeDtypeStruct((depth, N_COND, n), F32),
        compiler_params=_cparams("parallel", "arbitrary"),
        name="modulation",
    )(cond, mod_w, mod_b.reshape(depth, 1, n))


def _piece_specs(x_parts, tm, d):
    specs = []
    for x, row0 in x_parts:
        i0, nblk = row0 // tm, x.shape[0] // tm
        specs.append(pl.BlockSpec((tm, d), lambda i, i0=i0, nblk=nblk: (jnp.clip(i - i0, 0, nblk - 1), 0)))
    return specs


def _for_piece(x_refs, first_blocks, fn):
    if len(x_refs) == 1:
        fn(x_refs[0])
        return
    i = pl.program_id(0)
    bounds = list(first_blocks[1:]) + [None]
    for x_ref, lo, hi in zip(x_refs, first_blocks, bounds):
        pl.when((i >= lo) if hi is None else ((i >= lo) & (i < hi)))(functools.partial(fn, x_ref))


def _pm_kernel(*refs, sh, sc, first_blocks):
    n_x = len(first_blocks)
    g_ref, mod_ref, w_ref, o_ref = refs[n_x:]

    def project(x_ref):
        h = _norm_mod(x_ref[...], g_ref[...], mod_ref[sh:sh + 1, :], mod_ref[sc:sc + 1, :]).astype(BF16)
        o_ref[...] = jnp.dot(h, w_ref[...], preferred_element_type=F32).astype(o_ref.dtype)

    _for_piece(refs[:n_x], first_blocks, project)


def _norm_mod_matmul(x_parts, g, mods, w, layer, *, sh, sc, out_dtype, tm, rows_all, rows_ctx, rows_per_sample):
    d, n = w.shape[1:]
    cond = functools.partial(_cond_of_block, tm=tm, rows_ctx=rows_ctx, rows_per_sample=rows_per_sample)
    return pl.pallas_call(
        functools.partial(_pm_kernel, sh=sh, sc=sc, first_blocks=tuple(r // tm for _, r in x_parts)),
        grid=(rows_all // tm,),
        in_specs=_piece_specs(x_parts, tm, d) + [
            pl.BlockSpec((1, d), lambda i: (0, 0)),
            pl.BlockSpec((None, 6, d), lambda i: (cond(i), 0, 0)),
            pl.BlockSpec((None, d, n), lambda i: (layer, 0, 0), pipeline_mode=pl.Buffered(1))],
        out_specs=pl.BlockSpec((tm, n), lambda i: (i, 0)),
        out_shape=jax.ShapeDtypeStruct((rows_all, n), out_dtype),
        compiler_params=_cparams("parallel"),
        name="norm_mod_matmul",
    )(*[x for x, _ in x_parts], g.reshape(1, d), mods, w)


def _me_kernel(*refs, gate, first_blocks):
    n_x = len(first_blocks)
    y_ref, w_ref, b_ref, g_ref, mod_ref, o_ref = refs[n_x:]

    def residual(x_ref):
        out = jnp.dot(y_ref[...], w_ref[...], preferred_element_type=F32) + b_ref[...]
        o_ref[...] = x_ref[...] + mod_ref[gate:gate + 1, :] * (_rms(out) * g_ref[...])

    _for_piece(refs[:n_x], first_blocks, residual)


def _matmul_residual(y, w, layer, b, x_parts, g, mods, *, gate, tm, rows_all, rows_ctx, rows_per_sample):
    k, d = w.shape[1:]
    cond = functools.partial(_cond_of_block, tm=tm, rows_ctx=rows_ctx, rows_per_sample=rows_per_sample)
    return pl.pallas_call(
        functools.partial(_me_kernel, gate=gate, first_blocks=tuple(r // tm for _, r in x_parts)),
        grid=(rows_all // tm,),
        in_specs=_piece_specs(x_parts, tm, d) + [
            pl.BlockSpec((tm, k), lambda i: (i, 0)),
            pl.BlockSpec((None, k, d), lambda i: (layer, 0, 0)),
            pl.BlockSpec((1, d), lambda i: (0, 0)),
            pl.BlockSpec((1, d), lambda i: (0, 0)),
            pl.BlockSpec((None, 6, d), lambda i: (cond(i), 0, 0))],
        out_specs=pl.BlockSpec((tm, d), lambda i: (i, 0)),
        out_shape=jax.ShapeDtypeStruct((rows_all, d), F32),
        compiler_params=_cparams("parallel"),
        name="matmul_residual",
    )(*[x for x, _ in x_parts], y, w, b.reshape(1, d), g.reshape(1, d), mods)


def _ffn_kernel(x_ref, g1_ref, g2_ref, mod_ref, wu_ref, wd_ref, o_ref, h_ref):
    j = pl.program_id(1)

    @pl.when(j == 0)
    def _():
        h_ref[...] = _norm_mod(x_ref[...], g1_ref[...], mod_ref[3:4, :], mod_ref[4:5, :]).astype(BF16)
        o_ref[...] = jnp.zeros_like(o_ref)

    u = jnp.maximum(jnp.dot(h_ref[...], wu_ref[...], preferred_element_type=F32), 0.0)
    o_ref[...] += jnp.dot((u * u).astype(BF16), wd_ref[...].astype(BF16), preferred_element_type=F32)

    @pl.when(j == pl.num_programs(1) - 1)
    def _():
        o_ref[...] = x_ref[...] + mod_ref[5:6, :] * (_rms(o_ref[...]) * g2_ref[...])


def _ffn(x, g_pre, g_post, mods, w_up, w_down, layer, *, row0, nrows, tm, rows_ctx, rows_per_sample):
    d = x.shape[1]
    _, nf, _, tf = w_up.shape
    i0 = row0 // tm
    cond = functools.partial(_cond_of_block, tm=tm, rows_ctx=rows_ctx, rows_per_sample=rows_per_sample)
    return pl.pallas_call(
        _ffn_kernel,
        grid=(nrows // tm, nf),
        in_specs=[pl.BlockSpec((tm, d), lambda i, j: (i0 + i, 0)),
                  pl.BlockSpec((1, d), lambda i, j: (0, 0)),
                  pl.BlockSpec((1, d), lambda i, j: (0, 0)),
                  pl.BlockSpec((None, 6, d), lambda i, j: (cond(i0 + i), 0, 0)),
                  pl.BlockSpec((None, None, d, tf), lambda i, j: (layer, j, 0, 0)),
                  pl.BlockSpec((None, tf, d), lambda i, j: (layer, j, 0))],
        out_specs=pl.BlockSpec((tm, d), lambda i, j: (i, 0)),
        out_shape=jax.ShapeDtypeStruct((nrows, d), F32),
        scratch_shapes=[pltpu.VMEM((tm, d), BF16)],
        compiler_params=_cparams("parallel", "arbitrary", vmem_limit=VMEM_LIMIT_FFN),
        name="ffn",
    )(x, g_pre.reshape(1, d), g_post.reshape(1, d), mods, w_up, w_down)


def _mlstm_kernel(*refs, seq, nsub, chunk, zero_init, want_state, n_alias, all_slots, layer):
    it = iter(refs)
    q_ref, k_ref, v_ref, o_ref, gt_ref, gb_ref, ng_ref = (next(it) for _ in range(7))
    if not zero_init:
        c0_ref, n0_ref, m0_ref = next(it), next(it), next(it)
    for _ in range(n_alias):
        next(it)
    y_ref = next(it)
    if want_state:
        co_ref, no_ref, mo_ref = next(it), next(it), next(it)
    hf_ref, hb_ref, c_ref, n_ref = (next(it) for _ in range(4))
    L = chunk
    nc = seq // L

    if zero_init:
        m_state = [[jnp.zeros((1, 1), F32), jnp.zeros((1, 1), F32)] for _ in range(nsub)]
    else:
        c_ref[...] = c0_ref[...]
        n_ref[...] = n0_ref[...]
        m_state = [[m0_ref[sb, 0:1, 0:1], m0_ref[sb, 1:2, 0:1]] for sb in range(nsub)]

    row = lax.broadcasted_iota(jnp.int32, (L, L), 0)
    col = lax.broadcasted_iota(jnp.int32, (L, L), 1)
    masks = (col <= row, col >= row)
    lane = lax.broadcasted_iota(jnp.int32, (L, LANES), 1)
    is_forget = (lane % 2) == 1

    def rows_of(sb, c):
        return pl.ds(sb * seq + c * L, L)

    scan_order = (tuple(range(nc)), tuple(range(nc - 1, -1, -1)))
    units = [(sb, c, d) for step in range(nc) for sb in range(nsub) for d, c in ((0, step), (1, nc - 1 - step))]

    def log_gates(sb, c):
        gts = gt_ref[rows_of(sb, c), :] + gb_ref[...]
        x = jnp.where(is_forget, _log_sigmoid(gts), gts)
        def split3(z, axis):
            z_hi = z.astype(BF16)
            r_1 = z - z_hi.astype(F32)
            z_mid = r_1.astype(BF16)
            return jnp.concatenate([z_hi, z_mid, (r_1 - z_mid.astype(F32)).astype(BF16)], axis=axis)

        xt = x.T
        return x, xt, split3(x, 1), split3(xt, 0)

    chunk_gates = {(sb, c): log_gates(sb, c) for sb in range(nsub) for c in range(nc)}
    gate = {}
    for sb, c, d in units:
        x, xt, x3, xt3 = chunk_gates[sb, c]
        y3 = jnp.dot(masks[d].astype(BF16), x3, preferred_element_type=F32)
        y = (y3[:, :LANES] + y3[:, LANES:2 * LANES]) + y3[:, 2 * LANES:]
        yt3 = jnp.dot(xt3, masks[1 - d].astype(BF16), preferred_element_type=F32)
        yt = (yt3[:LANES] + yt3[LANES:2 * LANES]) + yt3[2 * LANES:]
        b_c = y[:, 2 * d + 1:2 * d + 2]
        a_c = x[:, 2 * d:2 * d + 1] - b_c
        a_r = xt[2 * d:2 * d + 1, :] - yt[2 * d + 1:2 * d + 2, :]
        am = jnp.where(masks[d], jnp.broadcast_to(a_r, (L, L)), -jnp.inf)
        b_last = b_c[L - 1:L, :] if d == 0 else b_c[0:1, :]
        g_c = b_last + a_c
        gate[sb, c, d] = (am, b_c + jnp.max(am, axis=1, keepdims=True), b_c, b_last, g_c,
                          jnp.max(g_c, axis=0, keepdims=True))

    m_in, m_out = {}, {}
    for sb in range(nsub):
        for d in range(2):
            m = m_state[sb][d]
            for c in scan_order[d]:
                _, _, _, b_last, _, g_max = gate[sb, c, d]
                m_in[sb, c, d] = m
                m = jnp.maximum(b_last + m, g_max)
                m_out[sb, c, d] = m
            m_state[sb][d] = m

    part = {}
    for sb, c, d in units:
        am, r_max, b_c, b_last, g_c, _ = gate[sb, c, d]
        rows = rows_of(sb, c)
        q = q_ref[rows, :]
        k = k_ref[rows, :]
        v = v_ref[rows, :]
        inter = b_c + m_in[sb, c, d]
        m_t = jnp.maximum(inter, r_max)
        s = (lax.dot_general(q, k, (((1,), (1,)), ((), ())), preferred_element_type=F32)
             * jnp.exp(am + (b_c - m_t)))
        kw = k.astype(F32) * jnp.exp(g_c - m_out[sb, c, d])
        sv = jnp.dot(s.astype(BF16), jnp.concatenate([v, jnp.ones((L, LANES), BF16)], axis=1),
                     preferred_element_type=F32)
        part[sb, c, d] = (sv[:, :v.shape[1]], sv[:, v.shape[1]:], jnp.exp(inter - m_t), jnp.exp(-m_t),
                          jnp.exp(b_last + m_in[sb, c, d] - m_out[sb, c, d]), kw.astype(BF16),
                          jnp.sum(kw, axis=0, keepdims=True))

    for sb, c, d in units:
        sv, s_sum, w_inter, den_floor, w_c, kw, kw_sum = part[sb, c, d]
        rows = rows_of(sb, c)
        kv = lax.dot_general(kw, v_ref[rows, :], (((0,), (0,)), ((), ())), preferred_element_type=F32)
        if zero_init and c == scan_order[d][0]:
            num, den = sv, s_sum
            c_ref[sb, d] = kv
            n_ref[sb, d] = kw_sum
        else:
            q = q_ref[rows, :]
            c_old = c_ref[sb, d]
            n_old = n_ref[sb, d]
            num = sv + w_inter * jnp.dot(q, c_old.astype(BF16), preferred_element_type=F32)
            den = s_sum + w_inter * jnp.sum(q.astype(F32) * n_old, axis=1, keepdims=True)
            c_ref[sb, d] = w_c * c_old + kv
            n_ref[sb, d] = w_c * n_old + kw_sum
        inv = 1.0 / jnp.maximum(jnp.abs(den), den_floor)
        (hf_ref, hb_ref)[d][rows, :] = num * jnp.tile(inv, (1, num.shape[1] // LANES))

    def finish(c, _):
        rows = pl.ds(pl.multiple_of(c * L, L), L)
        hs = hf_ref[rows, :] + hb_ref[rows, :]
        y_ref[rows, :] = (_rms(hs) * ng_ref[...] * _sigmoid(o_ref[rows, :])).astype(y_ref.dtype)
        return 0

    lax.fori_loop(0, nsub * nc, finish, 0)

    if want_state:
        if all_slots:
            for ref in (co_ref, no_ref, mo_ref):
                ref[...] = jnp.zeros_like(ref)
            co_ref, no_ref, mo_ref = co_ref.at[:, layer], no_ref.at[:, layer], mo_ref.at[:, layer]
        co_ref[...] = c_ref[...]
        no_ref[...] = n_ref[...]
        for sb in range(nsub):
            for d in range(2):
                mo_ref[sb, d:d + 1, :] = jnp.broadcast_to(m_state[sb][d], (1, LANES))


def _mlstm_core(qkv, og, gate_b, norm_g, state, y_prev, state_prev, *, layer, n_layers, row0, batch, seq, nsub,
                heads, dk, dv, want_state):
    L = math.gcd(MLSTM_L, seq)
    rb = nsub * seq
    assert row0 % rb == 0 and batch % nsub == 0
    b0 = row0 // rb
    hk = heads * dk
    zero_init = state is None
    state_specs = [pl.BlockSpec((nsub, None, 2, None, dk, dv), lambda b, h: (b, layer, 0, h, 0, 0)),
                   pl.BlockSpec((nsub, None, 2, None, 1, dk), lambda b, h: (b, layer, 0, h, 0, 0)),
                   pl.BlockSpec((nsub, None, None, 2, LANES), lambda b, h: (b, layer, h, 0, 0))]
    in_specs = [pl.BlockSpec((rb, dk), lambda b, h: (b0 + b, h)),
                pl.BlockSpec((rb, dk), lambda b, h: (b0 + b, heads + h)),
                pl.BlockSpec((rb, dv), lambda b, h: (b0 + b, 2 * hk // dv + h)),
                pl.BlockSpec((rb, dv), lambda b, h: (b0 + b, h)),
                pl.BlockSpec((rb, LANES), lambda b, h: (b0 + b, heads * dv // LANES + h)),
                pl.BlockSpec((1, LANES), lambda b, h: (0, h)),
                pl.BlockSpec((1, dv), lambda b, h: (0, h))]
    args = [qkv, qkv, qkv, og, og, gate_b, norm_g.reshape(1, heads * dv)]
    if not zero_init:
        in_specs += state_specs
        args += list(state)
    in_specs.append(pl.BlockSpec(memory_space=pl.ANY))
    args.append(y_prev)
    aliases = {len(args) - 1: 0}
    out_specs = [pl.BlockSpec((rb, dv), lambda b, h: (b0 + b, h))]
    out_shape = [jax.ShapeDtypeStruct(y_prev.shape, y_prev.dtype)]
    all_slots = want_state and state_prev is None
    if want_state:
        if all_slots:
            out_specs += [pl.BlockSpec((nsub, n_layers, 2, None, dk, dv), lambda b, h: (b, 0, 0, h, 0, 0)),
                          pl.BlockSpec((nsub, n_layers, 2, None, 1, dk), lambda b, h: (b, 0, 0, h, 0, 0)),
                          pl.BlockSpec((nsub, n_layers, None, 2, LANES), lambda b, h: (b, 0, h, 0, 0))]
        else:
            out_specs += state_specs
        out_shape += [jax.ShapeDtypeStruct((batch, n_layers, 2, heads, dk, dv), F32),
                      jax.ShapeDtypeStruct((batch, n_layers, 2, heads, 1, dk), F32),
                      jax.ShapeDtypeStruct((batch, n_layers, heads, 2, LANES), F32)]
        if state_prev is not None:
            for k_out, arr in enumerate(state_prev):
                in_specs.append(pl.BlockSpec(memory_space=pl.ANY))
                args.append(arr)
                aliases[len(args) - 1] = 1 + k_out
    return pl.pallas_call(
        functools.partial(_mlstm_kernel, seq=seq, nsub=nsub, chunk=L, zero_init=zero_init,
                          want_state=want_state, n_alias=len(aliases), all_slots=all_slots, layer=layer),
        grid=(batch // nsub, heads),
        in_specs=in_specs,
        out_specs=out_specs,
        out_shape=out_shape,
        input_output_aliases=aliases,
        scratch_shapes=[pltpu.VMEM((rb, dv), F32), pltpu.VMEM((rb, dv), F32),
                        pltpu.VMEM((nsub, 2, dk, dv), F32), pltpu.VMEM((nsub, 2, 1, dk), F32)],
        compiler_params=_cparams("parallel", "parallel"),
        name="mlstm_core",
    )(*args)


def _dft_mats(seq, gc):
    def cs(n):
        idx = np.arange(n)
        ang = 2.0 * np.pi * ((idx[:, None] * idx[None, :]) % n) / n
        return np.cos(ang) / math.sqrt(n), np.sin(ang) / math.sqrt(n)
    ct, st = cs(seq)
    cc, sc = cs(gc)
    w_time = np.concatenate([ct, -st], axis=1)
    w_chan = np.concatenate([cc, sc], axis=1)
    return jnp.asarray(w_time, BF16), jnp.asarray(w_chan, BF16)


def _fnet_kernel(x_ref, g_ref, mod_ref, wt_ref, wc_ref, _, o_ref, *, seq, nsub, gc):
    h = _norm_mod(x_ref[...], g_ref[...], mod_ref[0:1, :], mod_ref[1:2, :]).astype(BF16)
    for grp in range(h.shape[1] // gc):
        cols = slice(grp * gc, (grp + 1) * gc)
        y = jnp.dot(h[:, cols], wc_ref[...], preferred_element_type=F32)
        for sb in range(nsub):
            ys = y[sb * seq:(sb + 1) * seq]
            ycat = jnp.concatenate([ys[:, :gc], ys[:, gc:]], axis=0).astype(BF16)
            o_ref[sb * seq:(sb + 1) * seq, cols] = jnp.dot(wt_ref[...], ycat,
                                                           preferred_element_type=F32).astype(o_ref.dtype)


def _fnet_mix(x, g, mods, y_prev, *, row0, batch, seq, nsub, rows_all, rows_ctx, rows_per_sample):
    d = x.shape[1]
    gc = d // FNET_GROUPS
    rb = nsub * seq
    assert row0 % rb == 0 and batch % nsub == 0 and x.shape[0] == rows_all
    b0 = row0 // rb
    cond = functools.partial(_cond_of_block, tm=rb, rows_ctx=rows_ctx, rows_per_sample=rows_per_sample)
    w_time, w_chan = _dft_mats(seq, gc)
    in_specs = [pl.BlockSpec((rb, d), lambda b: (b0 + b, 0)),
                pl.BlockSpec((1, d), lambda b: (0, 0)),
                pl.BlockSpec((None, 6, d), lambda b: (cond(b0 + b), 0, 0)),
                pl.BlockSpec((seq, 2 * seq), lambda b: (0, 0)),
                pl.BlockSpec((gc, 2 * gc), lambda b: (0, 0)),
                pl.BlockSpec(memory_space=pl.ANY)]
    args = [x, g.reshape(1, d), mods, w_time, w_chan, y_prev]
    return pl.pallas_call(
        functools.partial(_fnet_kernel, seq=seq, nsub=nsub, gc=gc),
        grid=(batch // nsub,),
        in_specs=in_specs,
        out_specs=pl.BlockSpec((rb, d), lambda b: (b0 + b, 0)),
        out_shape=jax.ShapeDtypeStruct(y_prev.shape, y_prev.dtype),
        input_output_aliases={len(args) - 1: 0},
        compiler_params=_cparams("parallel"),
        name="fnet_mix",
    )(*args)


def _lru_kernel(*refs, seq, nsub, zero_init, want_state):
    it = iter(refs)
    br_ref, xb_ref, cw_ref, cb_ref, gw_ref, gbias_ref, lam_ref = (next(it) for _ in range(7))
    if not zero_init:
        h0_ref = next(it)
    next(it)
    y_ref = next(it)
    if want_state:
        hl_ref = next(it)
    a_ref, u_ref, hs_ref = (next(it) for _ in range(3))
    T = seq
    rows_blk = nsub * T
    S = SUBLANES
    x = xb_ref[...]
    tt = lax.broadcasted_iota(jnp.int32, x.shape, 0) % T
    xm1 = jnp.where(tt >= 1, pltpu.roll(x, 1, 0), 0.0)
    xp1 = jnp.where(tt < T - 1, pltpu.roll(x, rows_blk - 1, 0), 0.0)
    xp2 = jnp.where(tt < T - 2, pltpu.roll(x, rows_blk - 2, 0), 0.0)
    xc = (cw_ref[0:1, :] * xm1 + cw_ref[1:2, :] * x + cw_ref[2:3, :] * xp1 + cw_ref[3:4, :] * xp2
          + cb_ref[...])
    xcb = xc.astype(BF16)
    r8 = lax.broadcasted_iota(jnp.int32, (rows_blk // S, S, x.shape[1]), 1)
    for d in range(2):
        pre_r = jnp.dot(xcb, gw_ref[d, 0], preferred_element_type=F32) + gbias_ref[2 * d:2 * d + 1, :]
        pre_i = jnp.dot(xcb, gw_ref[d, 1], preferred_element_type=F32) + gbias_ref[2 * d + 1:2 * d + 2, :]
        log_a = _sigmoid(pre_r) * (LRU_C * _log_sigmoid(lam_ref[d:d + 1, :]))
        a = jnp.exp(log_a)
        u = jnp.sqrt(-jnp.tanh(log_a) * (a * a + 1.0)) * (_sigmoid(pre_i) * xc)
        a = a.reshape(rows_blk // S, S, a.shape[1])
        u = u.reshape(a.shape)
        s = 1
        while s < S:
            shift, valid = (s, r8 >= s) if d == 0 else (S - s, r8 < S - s)
            u = jnp.where(valid, a * pltpu.roll(u, shift, 1) + u, u)
            a = jnp.where(valid, a * pltpu.roll(a, shift, 1), a)
            s *= 2
        a_ref[d] = a.reshape(x.shape)
        u_ref[d] = u.reshape(x.shape)

    if zero_init:
        h_init = tuple(jnp.zeros((1, x.shape[1]), F32) for _ in range(2 * nsub))
    else:
        h_init = tuple(h0_ref[sb, d:d + 1, :] for sb in range(nsub) for d in range(2))
    groups = T // S

    def group(i, carry):
        out = []
        for sb in range(nsub):
            rf = pl.ds(pl.multiple_of((sb * groups + i) * S, S), S)
            h_f = a_ref[0, rf, :] * carry[2 * sb] + u_ref[0, rf, :]
            hs_ref[0, rf, :] = h_f
            rb = pl.ds(pl.multiple_of((sb * groups + groups - 1 - i) * S, S), S)
            h_b = a_ref[1, rb, :] * carry[2 * sb + 1] + u_ref[1, rb, :]
            hs_ref[1, rb, :] = h_b
            out += [h_f[S - 1:S, :], h_b[0:1, :]]
        return tuple(out)

    h_last = lax.fori_loop(0, groups, group, h_init, unroll=4)
    y_ref[...] = ((hs_ref[0] + hs_ref[1]) * _gelu_tanh(br_ref[...])).astype(y_ref.dtype)
    if want_state:
        for sb in range(nsub):
            for d in range(2):
                hl_ref[sb, d:d + 1, :] = h_last[2 * sb + d]


def _lru_core(gx, conv_w, conv_b, gate_w, gate_b, lam, h0, y_prev, *, row0, batch, seq, nsub, want_state):
    width = conv_w.shape[1]
    nb = gate_w.shape[2]
    bs = width // nb
    rb = nsub * seq
    assert row0 % rb == 0 and batch % nsub == 0
    b0 = row0 // rb
    zero_init = h0 is None
    in_specs = [pl.BlockSpec((rb, bs), lambda b, n: (b0 + b, n)),
                pl.BlockSpec((rb, bs), lambda b, n: (b0 + b, nb + n)),
                pl.BlockSpec((conv_w.shape[0], bs), lambda b, n: (0, n)),
                pl.BlockSpec((1, bs), lambda b, n: (0, n)),
                pl.BlockSpec((2, 2, None, bs, bs), lambda b, n: (0, 0, n, 0, 0)),
                pl.BlockSpec((4, bs), lambda b, n: (0, n)),
                pl.BlockSpec((2, bs), lambda b, n: (0, n))]
    args = [gx, gx, conv_w, conv_b.reshape(1, width), gate_w, gate_b.reshape(4, width), lam]
    if not zero_init:
        in_specs.append(pl.BlockSpec((nsub, 2, bs), lambda b, n: (b, 0, n)))
        args.append(h0)
    in_specs.append(pl.BlockSpec(memory_space=pl.ANY))
    args.append(y_prev)
    aliases = {len(args) - 1: 0}
    out_specs = [pl.BlockSpec((rb, bs), lambda b, n: (b0 + b, n))]
    out_shape = [jax.ShapeDtypeStruct(y_prev.shape, y_prev.dtype)]
    if want_state:
        out_specs.append(pl.BlockSpec((nsub, 2, bs), lambda b, n: (b, 0, n)))
        out_shape.append(jax.ShapeDtypeStruct((batch, 2, width), F32))
    return pl.pallas_call(
        functools.partial(_lru_kernel, seq=seq, nsub=nsub, zero_init=zero_init, want_state=want_state),
        grid=(batch // nsub, nb),
        in_specs=in_specs,
        out_specs=out_specs,
        out_shape=out_shape,
        input_output_aliases=aliases,
        scratch_shapes=[pltpu.VMEM((2, rb, bs), F32)] * 3,
        compiler_params=_cparams("parallel", "parallel"),
        name="lru_core",
    )(*args)


def _mlstm_weights(w_in, b_gate, heads, dk, dv):
    nl, d, _ = w_in.shape
    hk, hv = heads * dk, heads * dv
    wq, wk, wv, wo, wg = jnp.split(w_in, [hk, 2 * hk, 2 * hk + hv, 2 * hk + 2 * hv], axis=2)
    w_qkv = jnp.concatenate([wq.astype(BF16), (wk * (dk ** -0.5)).astype(BF16), wv.astype(BF16)], axis=2)
    wg = wg.reshape(nl, d, 2, 2, heads).transpose(0, 1, 4, 2, 3).reshape(nl, d, heads, 4)
    wg = jnp.pad(wg, ((0, 0), (0, 0), (0, 0), (0, LANES - 4))).reshape(nl, d, heads * LANES)
    w_og = jnp.concatenate([wo.astype(BF16), wg.astype(BF16)], axis=2)
    bg = b_gate.astype(F32).transpose(0, 3, 1, 2).reshape(nl, heads, 4)
    bg = jnp.pad(bg, ((0, 0), (0, 0), (0, LANES - 4))).reshape(nl, 1, heads * LANES)
    return w_qkv, w_og, bg


def kernel(x_prompt, x_sample, c, state_mlstm_C, state_mlstm_n, state_mlstm_m, state_lru_h, c_ctx,
           mod_w, mod_b, norm_g, ffn_w_up, ffn_w_down, mlstm_w_in, mlstm_b_gate, mlstm_norm_g, mlstm_w_out,
           fnet_w_out, fnet_b_out, lru_w_in, lru_conv_w, lru_conv_b, lru_gate_w, lru_gate_b, lru_lambda,
           lru_w_out):
    bp, tp, d = x_prompt.shape
    bs_, ts, _ = x_sample.shape
    depth = mod_w.shape[0]
    heads = mlstm_b_gate.shape[-1]
    dk, dv = state_mlstm_C.shape[-2:]
    rows_ctx = bp * tp
    rows_all = rows_ctx + bs_ * ts
    rows = dict(rows_all=rows_all, rows_ctx=rows_ctx, rows_per_sample=ts)
    assert 1 + bs_ <= N_COND
    tm = math.gcd(ROWS_PER_STEP, math.gcd(rows_ctx, ts))
    tm_ffn = math.gcd(ROWS_PER_STEP_FFN, math.gcd(rows_ctx, ts))

    x_parts = [(x_prompt.reshape(rows_ctx, d), 0), (x_sample.reshape(bs_ * ts, d), rows_ctx)]
    cond = jnp.zeros((N_COND, d), F32).at[0].set(c_ctx).at[1:1 + bs_].set(c)
    mods_all = _modulation(cond, mod_w, mod_b).reshape(depth, N_COND, 6, d)
    zero_bias = jnp.zeros((d,), F32)
    trunks = ((0, bp, tp, True), (rows_ctx, bs_, ts, False))

    n_ml = mlstm_w_in.shape[0]
    w_qkv, w_og, bg = _mlstm_weights(mlstm_w_in, mlstm_b_gate, heads, dk, dv)
    mlstm_state_in = (state_mlstm_C, state_mlstm_n.reshape(bs_, n_ml, 2, heads, 1, dk),
                      jnp.broadcast_to(state_mlstm_m.transpose(0, 1, 3, 2)[..., None], (bs_, n_ml, heads, 2, LANES)))
    w_out_of = (mlstm_w_out.astype(BF16), fnet_w_out.astype(BF16), lru_w_out.astype(BF16))
    lru_w_in_b, lru_gate_w_b = lru_w_in.astype(BF16), lru_gate_w.astype(BF16)
    d_ff = ffn_w_up.shape[2]
    tf = math.gcd(d_ff, FFN_TILE)
    ffn_w_up_b = ffn_w_up.reshape(depth, d, d_ff // tf, tf).transpose(0, 2, 1, 3).astype(BF16)

    assert heads * dv == d and lru_conv_w.shape[2] == d
    y = jnp.zeros((rows_all, d), BF16)
    mlstm_state_out, new_h = None, []
    for i in range(depth):
        mods = mods_all[i]
        kind, j = i % 3, i // 3
        b_out = zero_bias
        if kind == 0:
            pm = functools.partial(_norm_mod_matmul, x_parts, norm_g[i, 0], mods, sh=0, sc=1, tm=tm, **rows)
            qkv = pm(w_qkv, j, out_dtype=BF16)
            og = pm(w_og, j, out_dtype=F32)
            for row0, nb, seq, is_ctx in trunks:
                nsub = math.gcd(nb, max(1, ts // seq))
                res = _mlstm_core(qkv, og, bg[j], mlstm_norm_g[j], None if is_ctx else mlstm_state_in, y,
                                  mlstm_state_out if is_ctx else None, layer=j, n_layers=n_ml, row0=row0,
                                  batch=nb, seq=seq, nsub=nsub, heads=heads, dk=dk, dv=dv, want_state=is_ctx)
                y = res[0]
                if is_ctx:
                    mlstm_state_out = res[1:]
        elif kind == 1:
            for row0, nb, seq, _ in trunks:
                y = _fnet_mix(x_parts[0][0], norm_g[i, 0], mods, y, row0=row0, batch=nb, seq=seq,
                              nsub=math.gcd(nb, max(1, ts // seq)), **rows)
            b_out = fnet_b_out[j]
        else:
            gx = _norm_mod_matmul(x_parts, norm_g[i, 0], mods, lru_w_in_b, j, sh=0, sc=1, out_dtype=F32, tm=tm,
                                  **rows)
            for row0, nb, seq, is_ctx in trunks:
                res = _lru_core(gx, lru_conv_w[j], lru_conv_b[j], lru_gate_w_b[j], lru_gate_b[j],
                                lru_lambda[j], None if is_ctx else state_lru_h[:, j], y,
                                row0=row0, batch=nb, seq=seq, nsub=math.gcd(nb, max(1, ts // seq)),
                                want_state=is_ctx)
                y = res[0]
                if is_ctx:
                    new_h.append(res[1])
        x = _matmul_residual(y, w_out_of[kind], j, b_out, x_parts, norm_g[i, 1], mods, gate=2, tm=tm, **rows)
        ffn = functools.partial(_ffn, x, norm_g[i, 2], norm_g[i, 3], mods, ffn_w_up_b, ffn_w_down, i,
                                tm=tm_ffn, rows_ctx=rows_ctx, rows_per_sample=ts)
        if i < depth - 1:
            x_parts = [(ffn(row0=0, nrows=rows_all), 0)]
        else:
            y_prompt = ffn(row0=0, nrows=rows_ctx).reshape(bp, tp, d)
            y_sample = ffn(row0=rows_ctx, nrows=rows_all - rows_ctx).reshape(bs_, ts, d)

    new_c, new_n, new_m = mlstm_state_out
    return (y_prompt, y_sample, new_c, new_n.reshape(bp, n_ml, 2, heads, dk),
            new_m[..., 0].transpose(0, 1, 3, 2), jnp.stack(new_h, 1))
```

```python
import functools
import math

import numpy as np
import jax
import jax.numpy as jnp
from jax import lax
from jax.experimental import pallas as pl
from jax.experimental.pallas import tpu as pltpu

F32 = jnp.float32
BF16 = jnp.bfloat16
EPS = 1e-6
FNET_GROUPS = 8
LRU_C = 8.0
LANES = 128
MLSTM_L = 256
SUBLANES = 8
VMEM_LIMIT = 56 * 1024 * 1024
VMEM_LIMIT_FFN = 60 * 1024 * 1024
N_COND = 16
ROWS_PER_STEP = 512
ROWS_PER_STEP_FFN = 1024
FFN_TILE = 512
MOD_TILE_K = 256


def _cparams(*sem, vmem_limit=VMEM_LIMIT):
    return pltpu.CompilerParams(dimension_semantics=sem, vmem_limit_bytes=vmem_limit)


def _sigmoid(x):
    return 0.5 * jnp.tanh(0.5 * x) + 0.5


def _log_sigmoid(x):
    return jnp.minimum(x, 0.0) - jnp.log1p(jnp.exp(-jnp.abs(x)))


def _gelu_tanh(x):
    return 0.5 * x * (1.0 + jnp.tanh(math.sqrt(2.0 / math.pi) * (x + 0.044715 * (x * x * x))))


def _rms(x):
    return x * lax.rsqrt(jnp.mean(x * x, axis=-1, keepdims=True) + EPS)


def _norm_mod(x, g, shift, scale):
    return (_rms(x) * g) * (1.0 + scale) + shift


def _cond_of_block(i, tm, rows_ctx, rows_per_sample):
    r = i * tm
    return jnp.where(r < rows_ctx, 0, 1 + (r - rows_ctx) // rows_per_sample)


def _mod_kernel(c_ref, w_ref, b_ref, o_ref):
    @pl.when(pl.program_id(1) == 0)
    def _():
        o_ref[...] = jnp.broadcast_to(b_ref[...], o_ref.shape)

    c = c_ref[...]
    a = (c * _sigmoid(c)).astype(BF16)
    o_ref[...] += jnp.dot(a, w_ref[...].astype(BF16), preferred_element_type=F32)


def _modulation(cond, mod_w, mod_b):
    depth, d, n = mod_w.shape
    tk = math.gcd(d, MOD_TILE_K)
    return pl.pallas_call(
        _mod_kernel,
        grid=(depth, d // tk),
        in_specs=[pl.BlockSpec((N_COND, tk), lambda l, k: (0, k)),
                  pl.BlockSpec((None, tk, n), lambda l, k: (l, k, 0)),
                  pl.BlockSpec((None, 1, n), lambda l, k: (l, 0, 0))],
        out_specs=pl.BlockSpec((None, N_COND, n), lambda l, k: (l, 0, 0)),
        out_shape=jax.ShapeDtypeStruct((depth, N_COND, n), F32),
        compiler_params=_cparams("parallel", "arbitrary"),
        name="modulation",
    )(cond, mod_w, mod_b.reshape(depth, 1, n))


def _piece_specs(x_parts, tm, d):
    specs = []
    for x, row0 in x_parts:
        i0, nblk = row0 // tm, x.shape[0] // tm
        specs.append(pl.BlockSpec((tm, d), lambda i, i0=i0, nblk=nblk: (jnp.clip(i - i0, 0, nblk - 1), 0)))
    return specs


def _for_piece(x_refs, first_blocks, fn):
    if len(x_refs) == 1:
        fn(x_refs[0])
        return
    i = pl.program_id(0)
    bounds = list(first_blocks[1:]) + [None]
    for x_ref, lo, hi in zip(x_refs, first_blocks, bounds):
        pl.when((i >= lo) if hi is None else ((i >= lo) & (i < hi)))(functools.partial(fn, x_ref))


def _pm_kernel(*refs, sh, sc, first_blocks):
    n_x = len(first_blocks)
    g_ref, mod_ref = refs[n_x:n_x + 2]
    n_w = (len(refs) - n_x - 2) // 2
    w_refs, o_refs = refs[n_x + 2:n_x + 2 + n_w], refs[n_x + 2 + n_w:]

    def project(x_ref):
        h = _norm_mod(x_ref[...], g_ref[...], mod_ref[sh:sh + 1, :], mod_ref[sc:sc + 1, :]).astype(BF16)
        for w_ref, o_ref in zip(w_refs, o_refs):
            o_ref[...] = jnp.dot(h, w_ref[...], preferred_element_type=F32).astype(o_ref.dtype)

    _for_piece(refs[:n_x], first_blocks, project)


def _norm_mod_matmul(x_parts, g, mods, ws, layer, *, sh, sc, out_dtypes, tm, rows_all, rows_ctx, rows_per_sample):
    d = ws[0].shape[1]
    cond = functools.partial(_cond_of_block, tm=tm, rows_ctx=rows_ctx, rows_per_sample=rows_per_sample)
    return pl.pallas_call(
        functools.partial(_pm_kernel, sh=sh, sc=sc, first_blocks=tuple(r // tm for _, r in x_parts)),
        grid=(rows_all // tm,),
        in_specs=_piece_specs(x_parts, tm, d) + [
            pl.BlockSpec((1, d), lambda i: (0, 0)),
            pl.BlockSpec((None, 6, d), lambda i: (cond(i), 0, 0))] + [
            pl.BlockSpec((None, d, w.shape[2]), lambda i: (layer, 0, 0), pipeline_mode=pl.Buffered(1)) for w in ws],
        out_specs=[pl.BlockSpec((tm, w.shape[2]), lambda i: (i, 0)) for w in ws],
        out_shape=[jax.ShapeDtypeStruct((rows_all, w.shape[2]), dt) for w, dt in zip(ws, out_dtypes)],
        compiler_params=_cparams("parallel"),
        name="norm_mod_matmul",
    )(*[x for x, _ in x_parts], g.reshape(1, d), mods, *ws)


def _me_kernel(*refs, gate, first_blocks):
    n_x = len(first_blocks)
    y_ref, w_ref, b_ref, g_ref, mod_ref, o_ref = refs[n_x:]

    def residual(x_ref):
        out = jnp.dot(y_ref[...], w_ref[...], preferred_element_type=F32) + b_ref[...]
        o_ref[...] = x_ref[...] + mod_ref[gate:gate + 1, :] * (_rms(out) * g_ref[...])

    _for_piece(refs[:n_x], first_blocks, residual)


def _matmul_residual(y, w, layer, b, x_parts, g, mods, *, gate, tm, rows_all, rows_ctx, rows_per_sample):
    k, d = w.shape[1:]
    cond = functools.partial(_cond_of_block, tm=tm, rows_ctx=rows_ctx, rows_per_sample=rows_per_sample)
    return pl.pallas_call(
        functools.partial(_me_kernel, gate=gate, first_blocks=tuple(r // tm for _, r in x_parts)),
        grid=(rows_all // tm,),
        in_specs=_piece_specs(x_parts, tm, d) + [
            pl.BlockSpec((tm, k), lambda i: (i, 0)),
            pl.BlockSpec((None, k, d), lambda i: (layer, 0, 0)),
            pl.BlockSpec((1, d), lambda i: (0, 0)),
            pl.BlockSpec((1, d), lambda i: (0, 0)),
            pl.BlockSpec((None, 6, d), lambda i: (cond(i), 0, 0))],
        out_specs=pl.BlockSpec((tm, d), lambda i: (i, 0)),
        out_shape=jax.ShapeDtypeStruct((rows_all, d), F32),
        compiler_params=_cparams("parallel"),
        name="matmul_residual",
    )(*[x for x, _ in x_parts], y, w, b.reshape(1, d), g.reshape(1, d), mods)


def _ffn_kernel(x_ref, g1_ref, g2_ref, mod_ref, wu_ref, wd_ref, o_ref, h_ref):
    j = pl.program_id(1)

    @pl.when(j == 0)
    def _():
        h_ref[...] = _norm_mod(x_ref[...], g1_ref[...], mod_ref[3:4, :], mod_ref[4:5, :]).astype(BF16)
        o_ref[...] = jnp.zeros_like(o_ref)

    u = jnp.maximum(jnp.dot(h_ref[...], wu_ref[...], preferred_element_type=F32), 0.0)
    o_ref[...] += jnp.dot((u * u).astype(BF16), wd_ref[...].astype(BF16), preferred_element_type=F32)

    @pl.when(j == pl.num_programs(1) - 1)
    def _():
        o_ref[...] = x_ref[...] + mod_ref[5:6, :] * (_rms(o_ref[...]) * g2_ref[...])


def _ffn(x, g_pre, g_post, mods, w_up, w_down, layer, *, row0, nrows, tm, rows_ctx, rows_per_sample):
    d = x.shape[1]
    _, nf, _, tf = w_up.shape
    i0 = row0 // tm
    cond = functools.partial(_cond_of_block, tm=tm, rows_ctx=rows_ctx, rows_per_sample=rows_per_sample)
    return pl.pallas_call(
        _ffn_kernel,
        grid=(nrows // tm, nf),
        in_specs=[pl.BlockSpec((tm, d), lambda i, j: (i0 + i, 0)),
                  pl.BlockSpec((1, d), lambda i, j: (0, 0)),
                  pl.BlockSpec((1, d), lambda i, j: (0, 0)),
                  pl.BlockSpec((None, 6, d), lambda i, j: (cond(i0 + i), 0, 0)),
                  pl.BlockSpec((None, None, d, tf), lambda i, j: (layer, j, 0, 0)),
                  pl.BlockSpec((None, tf, d), lambda i, j: (layer, j, 0))],
        out_specs=pl.BlockSpec((tm, d), lambda i, j: (i, 0)),
        out_shape=jax.ShapeDtypeStruct((nrows, d), F32),
        scratch_shapes=[pltpu.VMEM((tm, d), BF16)],
        compiler_params=_cparams("parallel", "arbitrary", vmem_limit=VMEM_LIMIT_FFN),
        name="ffn",
    )(x, g_pre.reshape(1, d), g_post.reshape(1, d), mods, w_up, w_down)


def _mlstm_kernel(*refs, seq, nsub, chunk, zero_init, want_state, n_alias, all_slots, layer):
    it = iter(refs)
    q_ref, k_ref, v_ref, o_ref, gt_ref, gb_ref, ng_ref = (next(it) for _ in range(7))
    if not zero_init:
        c0_ref, n0_ref, m0_ref = next(it), next(it), next(it)
    for _ in range(n_alias):
        next(it)
    y_ref = next(it)
    if want_state:
        co_ref, no_ref, mo_ref = next(it), next(it), next(it)
    hf_ref, hb_ref, c_ref, n_ref = (next(it) for _ in range(4))
    L = chunk
    nc = seq // L

    if zero_init:
        m_state = [[jnp.zeros((1, 1), F32), jnp.zeros((1, 1), F32)] for _ in range(nsub)]
    else:
        c_ref[...] = c0_ref[...]
        n_ref[...] = n0_ref[...]
        m_state = [[m0_ref[sb, 0:1, 0:1], m0_ref[sb, 1:2, 0:1]] for sb in range(nsub)]

    row = lax.broadcasted_iota(jnp.int32, (L, L), 0)
    col = lax.broadcasted_iota(jnp.int32, (L, L), 1)
    masks = (col <= row, col >= row)
    lane = lax.broadcasted_iota(jnp.int32, (L, LANES), 1)
    is_forget = (lane % 2) == 1

    def rows_of(sb, c):
        return pl.ds(sb * seq + c * L, L)

    scan_order = (tuple(range(nc)), tuple(range(nc - 1, -1, -1)))
    units = [(sb, c, d) for step in range(nc) for sb in range(nsub) for d, c in ((0, step), (1, nc - 1 - step))]

    def log_gates(sb, c):
        gts = gt_ref[rows_of(sb, c), :] + gb_ref[...]
        x = jnp.where(is_forget, _log_sigmoid(gts), gts)
        def split3(z, axis):
            z_hi = z.astype(BF16)
            r_1 = z - z_hi.astype(F32)
            z_mid = r_1.astype(BF16)
            return jnp.concatenate([z_hi, z_mid, (r_1 - z_mid.astype(F32)).astype(BF16)], axis=axis)

        xt = x.T
        return x, xt, split3(x, 1), split3(xt, 0)

    chunk_gates = {(sb, c): log_gates(sb, c) for sb in range(nsub) for c in range(nc)}
    gate = {}
    for sb, c, d in units:
        x, xt, x3, xt3 = chunk_gates[sb, c]
        y3 = jnp.dot(masks[d].astype(BF16), x3, preferred_element_type=F32)
        y = (y3[:, :LANES] + y3[:, LANES:2 * LANES]) + y3[:, 2 * LANES:]
        yt3 = jnp.dot(xt3, masks[1 - d].astype(BF16), preferred_element_type=F32)
        yt = (yt3[:LANES] + yt3[LANES:2 * LANES]) + yt3[2 * LANES:]
        b_c = y[:, 2 * d + 1:2 * d + 2]
        a_c = x[:, 2 * d:2 * d + 1] - b_c
        a_r = xt[2 * d:2 * d + 1, :] - yt[2 * d + 1:2 * d + 2, :]
        am = jnp.where(masks[d], jnp.broadcast_to(a_r, (L, L)), -jnp.inf)
        b_last = b_c[L - 1:L, :] if d == 0 else b_c[0:1, :]
        g_c = b_last + a_c
        gate[sb, c, d] = (am, b_c + jnp.max(am, axis=1, keepdims=True), b_c, b_last, g_c,
                          jnp.max(g_c, axis=0, keepdims=True))

    m_in, m_out = {}, {}
    for sb in range(nsub):
        for d in range(2):
            m = m_state[sb][d]
            for c in scan_order[d]:
                _, _, _, b_last, _, g_max = gate[sb, c, d]
                m_in[sb, c, d] = m
                m = jnp.maximum(b_last + m, g_max)
                m_out[sb, c, d] = m
            m_state[sb][d] = m

    part = {}
    for sb, c, d in units:
        am, r_max, b_c, b_last, g_c, _ = gate[sb, c, d]
        rows = rows_of(sb, c)
        q = q_ref[rows, :]
        k = k_ref[rows, :]
        v = v_ref[rows, :]
        inter = b_c + m_in[sb, c, d]
        m_t = jnp.maximum(inter, r_max)
        s = (lax.dot_general(q, k, (((1,), (1,)), ((), ())), preferred_element_type=F32)
             * jnp.exp(am + (b_c - m_t)))
        kw = k.astype(F32) * jnp.exp(g_c - m_out[sb, c, d])
        sv = jnp.dot(s.astype(BF16), jnp.concatenate([v, jnp.ones((L, LANES), BF16)], axis=1),
                     preferred_element_type=F32)
        part[sb, c, d] = (sv[:, :v.shape[1]], sv[:, v.shape[1]:], jnp.exp(inter - m_t), jnp.exp(-m_t),
                          jnp.exp(b_last + m_in[sb, c, d] - m_out[sb, c, d]), kw.astype(BF16),
                          jnp.sum(kw, axis=0, keepdims=True))

    for sb, c, d in units:
        sv, s_sum, w_inter, den_floor, w_c, kw, kw_sum = part[sb, c, d]
        rows = rows_of(sb, c)
        kv = lax.dot_general(kw, v_ref[rows, :], (((0,), (0,)), ((), ())), preferred_element_type=F32)
        if zero_init and c == scan_order[d][0]:
            num, den = sv, s_sum
            c_ref[sb, d] = kv
            n_ref[sb, d] = kw_sum
        else:
            q = q_ref[rows, :]
            c_old = c_ref[sb, d]
            n_old = n_ref[sb, d]
            num = sv + w_inter * jnp.dot(q, c_old.astype(BF16), preferred_element_type=F32)
            den = s_sum + w_inter * jnp.sum(q.astype(F32) * n_old, axis=1, keepdims=True)
            c_ref[sb, d] = w_c * c_old + kv
            n_ref[sb, d] = w_c * n_old + kw_sum
        inv = 1.0 / jnp.maximum(jnp.abs(den), den_floor)
        (hf_ref, hb_ref)[d][rows, :] = num * jnp.tile(inv, (1, num.shape[1] // LANES))

    def finish(c, _):
        rows = pl.ds(pl.multiple_of(c * L, L), L)
        hs = hf_ref[rows, :] + hb_ref[rows, :]
        y_ref[rows, :] = (_rms(hs) * ng_ref[...] * _sigmoid(o_ref[rows, :])).astype(y_ref.dtype)
        return 0

    lax.fori_loop(0, nsub * nc, finish, 0)

    if want_state:
        if all_slots:
            for ref in (co_ref, no_ref, mo_ref):
                ref[...] = jnp.zeros_like(ref)
            co_ref, no_ref, mo_ref = co_ref.at[:, layer], no_ref.at[:, layer], mo_ref.at[:, layer]
        co_ref[...] = c_ref[...]
        no_ref[...] = n_ref[...]
        for sb in range(nsub):
            for d in range(2):
                mo_ref[sb, d:d + 1, :] = jnp.broadcast_to(m_state[sb][d], (1, LANES))


def _mlstm_core(qkv, og, gate_b, norm_g, state, y_prev, state_prev, *, layer, n_layers, row0, batch, seq, nsub,
                heads, dk, dv, want_state):
    L = math.gcd(MLSTM_L, seq)
    rb = nsub * seq
    assert row0 % rb == 0 and batch % nsub == 0
    b0 = row0 // rb
    hk = heads * dk
    zero_init = state is None
    state_specs = [pl.BlockSpec((nsub, None, 2, None, dk, dv), lambda b, h: (b, layer, 0, h, 0, 0)),
                   pl.BlockSpec((nsub, None, 2, None, 1, dk), lambda b, h: (b, layer, 0, h, 0, 0)),
                   pl.BlockSpec((nsub, None, None, 2, LANES), lambda b, h: (b, layer, h, 0, 0))]
    in_specs = [pl.BlockSpec((rb, dk), lambda b, h: (b0 + b, h)),
                pl.BlockSpec((rb, dk), lambda b, h: (b0 + b, heads + h)),
                pl.BlockSpec((rb, dv), lambda b, h: (b0 + b, 2 * hk // dv + h)),
                pl.BlockSpec((rb, dv), lambda b, h: (b0 + b, h)),
                pl.BlockSpec((rb, LANES), lambda b, h: (b0 + b, heads * dv // LANES + h)),
                pl.BlockSpec((1, LANES), lambda b, h: (0, h)),
                pl.BlockSpec((1, dv), lambda b, h: (0, h))]
    args = [qkv, qkv, qkv, og, og, gate_b, norm_g.reshape(1, heads * dv)]
    if not zero_init:
        in_specs += state_specs
        args += list(state)
    in_specs.append(pl.BlockSpec(memory_space=pl.ANY))
    args.append(y_prev)
    aliases = {len(args) - 1: 0}
    out_specs = [pl.BlockSpec((rb, dv), lambda b, h: (b0 + b, h))]
    out_shape = [jax.ShapeDtypeStruct(y_prev.shape, y_prev.dtype)]
    all_slots = want_state and state_prev is None
    if want_state:
        if all_slots:
            out_specs += [pl.BlockSpec((nsub, n_layers, 2, None, dk, dv), lambda b, h: (b, 0, 0, h, 0, 0)),
                          pl.BlockSpec((nsub, n_layers, 2, None, 1, dk), lambda b, h: (b, 0, 0, h, 0, 0)),
                          pl.BlockSpec((nsub, n_layers, None, 2, LANES), lambda b, h: (b, 0, h, 0, 0))]
        else:
            out_specs += state_specs
        out_shape += [jax.ShapeDtypeStruct((batch, n_layers, 2, heads, dk, dv), F32),
                      jax.ShapeDtypeStruct((batch, n_layers, 2, heads, 1, dk), F32),
                      jax.ShapeDtypeStruct((batch, n_layers, heads, 2, LANES), F32)]
        if state_prev is not None:
            for k_out, arr in enumerate(state_prev):
                in_specs.append(pl.BlockSpec(memory_space=pl.ANY))
                args.append(arr)
                aliases[len(args) - 1] = 1 + k_out
    return pl.pallas_call(
        functools.partial(_mlstm_kernel, seq=seq, nsub=nsub, chunk=L, zero_init=zero_init,
                          want_state=want_state, n_alias=len(aliases), all_slots=all_slots, layer=layer),
        grid=(batch // nsub, heads),
        in_specs=in_specs,
        out_specs=out_specs,
        out_shape=out_shape,
        input_output_aliases=aliases,
        scratch_shapes=[pltpu.VMEM((rb, dv), F32), pltpu.VMEM((rb, dv), F32),
                        pltpu.VMEM((nsub, 2, dk, dv), F32), pltpu.VMEM((nsub, 2, 1, dk), F32)],
        compiler_params=_cparams("parallel", "parallel"),
        name="mlstm_core",
    )(*args)


def _dft_mats(seq, gc):
    def cs(n):
        idx = np.arange(n)
        ang = 2.0 * np.pi * ((idx[:, None] * idx[None, :]) % n) / n
        return np.cos(ang) / math.sqrt(n), np.sin(ang) / math.sqrt(n)
    ct, st = cs(seq)
    cc, sc = cs(gc)
    w_time = np.concatenate([ct, -st], axis=1)
    w_chan = np.concatenate([cc, sc], axis=1)
    return jnp.asarray(w_time, BF16), jnp.asarray(w_chan, BF16)


def _fnet_kernel(x_ref, g_ref, mod_ref, wt_ref, wc_ref, _, o_ref, *, seq, nsub, gc):
    h = _norm_mod(x_ref[...], g_ref[...], mod_ref[0:1, :], mod_ref[1:2, :]).astype(BF16)
    for grp in range(h.shape[1] // gc):
        cols = slice(grp * gc, (grp + 1) * gc)
        y = jnp.dot(h[:, cols], wc_ref[...], preferred_element_type=F32)
        for sb in range(nsub):
            ys = y[sb * seq:(sb + 1) * seq]
            ycat = jnp.concatenate([ys[:, :gc], ys[:, gc:]], axis=0).astype(BF16)
            o_ref[sb * seq:(sb + 1) * seq, cols] = jnp.dot(wt_ref[...], ycat,
                                                           preferred_element_type=F32).astype(o_ref.dtype)


def _fnet_mix(x, g, mods, y_prev, *, row0, batch, seq, nsub, rows_all, rows_ctx, rows_per_sample):
    d = x.shape[1]
    gc = d // FNET_GROUPS
    rb = nsub * seq
    assert row0 % rb == 0 and batch % nsub == 0 and x.shape[0] == rows_all
    b0 = row0 // rb
    cond = functools.partial(_cond_of_block, tm=rb, rows_ctx=rows_ctx, rows_per_sample=rows_per_sample)
    w_time, w_chan = _dft_mats(seq, gc)
    in_specs = [pl.BlockSpec((rb, d), lambda b: (b0 + b, 0)),
                pl.BlockSpec((1, d), lambda b: (0, 0)),
                pl.BlockSpec((None, 6, d), lambda b: (cond(b0 + b), 0, 0)),
                pl.BlockSpec((seq, 2 * seq), lambda b: (0, 0)),
                pl.BlockSpec((gc, 2 * gc), lambda b: (0, 0)),
                pl.BlockSpec(memory_space=pl.ANY)]
    args = [x, g.reshape(1, d), mods, w_time, w_chan, y_prev]
    return pl.pallas_call(
        functools.partial(_fnet_kernel, seq=seq, nsub=nsub, gc=gc),
        grid=(batch // nsub,),
        in_specs=in_specs,
        out_specs=pl.BlockSpec((rb, d), lambda b: (b0 + b, 0)),
        out_shape=jax.ShapeDtypeStruct(y_prev.shape, y_prev.dtype),
        input_output_aliases={len(args) - 1: 0},
        compiler_params=_cparams("parallel"),
        name="fnet_mix",
    )(*args)


def _lru_kernel(*refs, seq, nsub, zero_init, want_state):
    it = iter(refs)
    br_ref, xb_ref, cw_ref, cb_ref, gw_ref, gbias_ref, lam_ref = (next(it) for _ in range(7))
    if not zero_init:
        h0_ref = next(it)
    next(it)
    y_ref = next(it)
    if want_state:
        hl_ref = next(it)
    a_ref, u_ref, hs_ref = (next(it) for _ in range(3))
    T = seq
    rows_blk = nsub * T
    S = SUBLANES
    x = xb_ref[...]
    tt = lax.broadcasted_iota(jnp.int32, x.shape, 0) % T
    xm1 = jnp.where(tt >= 1, pltpu.roll(x, 1, 0), 0.0)
    xp1 = jnp.where(tt < T - 1, pltpu.roll(x, rows_blk - 1, 0), 0.0)
    xp2 = jnp.where(tt < T - 2, pltpu.roll(x, rows_blk - 2, 0), 0.0)
    xc = (cw_ref[0:1, :] * xm1 + cw_ref[1:2, :] * x + cw_ref[2:3, :] * xp1 + cw_ref[3:4, :] * xp2
          + cb_ref[...])
    xcb = xc.astype(BF16)
    r8 = lax.broadcasted_iota(jnp.int32, (rows_blk // S, S, x.shape[1]), 1)
    for d in range(2):
        pre_r = jnp.dot(xcb, gw_ref[d, 0], preferred_element_type=F32) + gbias_ref[2 * d:2 * d + 1, :]
        pre_i = jnp.dot(xcb, gw_ref[d, 1], preferred_element_type=F32) + gbias_ref[2 * d + 1:2 * d + 2, :]
        log_a = _sigmoid(pre_r) * (LRU_C * _log_sigmoid(lam_ref[d:d + 1, :]))
        a = jnp.exp(log_a)
        u = jnp.sqrt(-jnp.tanh(log_a) * (a * a + 1.0)) * (_sigmoid(pre_i) * xc)
        a = a.reshape(rows_blk // S, S, a.shape[1])
        u = u.reshape(a.shape)
        s = 1
        while s < S:
            shift, valid = (s, r8 >= s) if d == 0 else (S - s, r8 < S - s)
            u = jnp.where(valid, a * pltpu.roll(u, shift, 1) + u, u)
            a = jnp.where(valid, a * pltpu.roll(a, shift, 1), a)
            s *= 2
        a_ref[d] = a.reshape(x.shape)
        u_ref[d] = u.reshape(x.shape)

    if zero_init:
        h_init = tuple(jnp.zeros((1, x.shape[1]), F32) for _ in range(2 * nsub))
    else:
        h_init = tuple(h0_ref[sb, d:d + 1, :] for sb in range(nsub) for d in range(2))
    groups = T // S

    def group(i, carry):
        out = []
        for sb in range(nsub):
            rf = pl.ds(pl.multiple_of((sb * groups + i) * S, S), S)
            h_f = a_ref[0, rf, :] * carry[2 * sb] + u_ref[0, rf, :]
            hs_ref[0, rf, :] = h_f
            rb = pl.ds(pl.multiple_of((sb * groups + groups - 1 - i) * S, S), S)
            h_b = a_ref[1, rb, :] * carry[2 * sb + 1] + u_ref[1, rb, :]
            hs_ref[1, rb, :] = h_b
            out += [h_f[S - 1:S, :], h_b[0:1, :]]
        return tuple(out)

    h_last = lax.fori_loop(0, groups, group, h_init, unroll=4)
    y_ref[...] = ((hs_ref[0] + hs_ref[1]) * _gelu_tanh(br_ref[...])).astype(y_ref.dtype)
    if want_state:
        for sb in range(nsub):
            for d in range(2):
                hl_ref[sb, d:d + 1, :] = h_last[2 * sb + d]


def _lru_core(gx, conv_w, conv_b, gate_w, gate_b, lam, h0, y_prev, *, row0, batch, seq, nsub, want_state):
    width = conv_w.shape[1]
    nb = gate_w.shape[2]
    bs = width // nb
    rb = nsub * seq
    assert row0 % rb == 0 and batch % nsub == 0
    b0 = row0 // rb
    zero_init = h0 is None
    in_specs = [pl.BlockSpec((rb, bs), lambda b, n: (b0 + b, n)),
                pl.BlockSpec((rb, bs), lambda b, n: (b0 + b, nb + n)),
                pl.BlockSpec((conv_w.shape[0], bs), lambda b, n: (0, n)),
                pl.BlockSpec((1, bs), lambda b, n: (0, n)),
                pl.BlockSpec((2, 2, None, bs, bs), lambda b, n: (0, 0, n, 0, 0)),
                pl.BlockSpec((4, bs), lambda b, n: (0, n)),
                pl.BlockSpec((2, bs), lambda b, n: (0, n))]
    args = [gx, gx, conv_w, conv_b.reshape(1, width), gate_w, gate_b.reshape(4, width), lam]
    if not zero_init:
        in_specs.append(pl.BlockSpec((nsub, 2, bs), lambda b, n: (b, 0, n)))
        args.append(h0)
    in_specs.append(pl.BlockSpec(memory_space=pl.ANY))
    args.append(y_prev)
    aliases = {len(args) - 1: 0}
    out_specs = [pl.BlockSpec((rb, bs), lambda b, n: (b0 + b, n))]
    out_shape = [jax.ShapeDtypeStruct(y_prev.shape, y_prev.dtype)]
    if want_state:
        out_specs.append(pl.BlockSpec((nsub, 2, bs), lambda b, n: (b, 0, n)))
        out_shape.append(jax.ShapeDtypeStruct((batch, 2, width), F32))
    return pl.pallas_call(
        functools.partial(_lru_kernel, seq=seq, nsub=nsub, zero_init=zero_init, want_state=want_state),
        grid=(batch // nsub, nb),
        in_specs=in_specs,
        out_specs=out_specs,
        out_shape=out_shape,
        input_output_aliases=aliases,
        scratch_shapes=[pltpu.VMEM((2, rb, bs), F32)] * 3,
        compiler_params=_cparams("parallel", "parallel"),
        name="lru_core",
    )(*args)


def _mlstm_weights(w_in, b_gate, heads, dk, dv):
    nl, d, _ = w_in.shape
    hk, hv = heads * dk, heads * dv
    wq, wk, wv, wo, wg = jnp.split(w_in, [hk, 2 * hk, 2 * hk + hv, 2 * hk + 2 * hv], axis=2)
    w_qkv = jnp.concatenate([wq.astype(BF16), (wk * (dk ** -0.5)).astype(BF16), wv.astype(BF16)], axis=2)
    wg = wg.reshape(nl, d, 2, 2, heads).transpose(0, 1, 4, 2, 3).reshape(nl, d, heads, 4)
    wg = jnp.pad(wg, ((0, 0), (0, 0), (0, 0), (0, LANES - 4))).reshape(nl, d, heads * LANES)
    w_og = jnp.concatenate([wo.astype(BF16), wg.astype(BF16)], axis=2)
    bg = b_gate.astype(F32).transpose(0, 3, 1, 2).reshape(nl, heads, 4)
    bg = jnp.pad(bg, ((0, 0), (0, 0), (0, LANES - 4))).reshape(nl, 1, heads * LANES)
    return w_qkv, w_og, bg


def kernel(x_prompt, x_sample, c, state_mlstm_C, state_mlstm_n, state_mlstm_m, state_lru_h, c_ctx,
           mod_w, mod_b, norm_g, ffn_w_up, ffn_w_down, mlstm_w_in, mlstm_b_gate, mlstm_norm_g, mlstm_w_out,
           fnet_w_out, fnet_b_out, lru_w_in, lru_conv_w, lru_conv_b, lru_gate_w, lru_gate_b, lru_lambda,
           lru_w_out):
    bp, tp, d = x_prompt.shape
    bs_, ts, _ = x_sample.shape
    depth = mod_w.shape[0]
    heads = mlstm_b_gate.shape[-1]
    dk, dv = state_mlstm_C.shape[-2:]
    rows_ctx = bp * tp
    rows_all = rows_ctx + bs_ * ts
    rows = dict(rows_all=rows_all, rows_ctx=rows_ctx, rows_per_sample=ts)
    assert 1 + bs_ <= N_COND
    tm = math.gcd(ROWS_PER_STEP, math.gcd(rows_ctx, ts))
    tm_ffn = math.gcd(ROWS_PER_STEP_FFN, math.gcd(rows_ctx, ts))

    x_parts = [(x_prompt.reshape(rows_ctx, d), 0), (x_sample.reshape(bs_ * ts, d), rows_ctx)]
    cond = jnp.zeros((N_COND, d), F32).at[0].set(c_ctx).at[1:1 + bs_].set(c)
    mods_all = _modulation(cond, mod_w, mod_b).reshape(depth, N_COND, 6, d)
    zero_bias = jnp.zeros((d,), F32)
    trunks = ((0, bp, tp, True), (rows_ctx, bs_, ts, False))

    n_ml = mlstm_w_in.shape[0]
    w_qkv, w_og, bg = _mlstm_weights(mlstm_w_in, mlstm_b_gate, heads, dk, dv)
    mlstm_state_in = (state_mlstm_C, state_mlstm_n.reshape(bs_, n_ml, 2, heads, 1, dk),
                      jnp.broadcast_to(state_mlstm_m.transpose(0, 1, 3, 2)[..., None], (bs_, n_ml, heads, 2, LANES)))
    w_out_of = (mlstm_w_out.astype(BF16), fnet_w_out.astype(BF16), lru_w_out.astype(BF16))
    lru_w_in_b, lru_gate_w_b = lru_w_in.astype(BF16), lru_gate_w.astype(BF16)
    d_ff = ffn_w_up.shape[2]
    tf = math.gcd(d_ff, FFN_TILE)
    ffn_w_up_b = ffn_w_up.reshape(depth, d, d_ff // tf, tf).transpose(0, 2, 1, 3).astype(BF16)

    assert heads * dv == d and lru_conv_w.shape[2] == d
    y = jnp.zeros((rows_all, d), BF16)
    mlstm_state_out, new_h = None, []
    for i in range(depth):
        mods = mods_all[i]
        kind, j = i % 3, i // 3
        b_out = zero_bias
        if kind == 0:
            qkv, og = _norm_mod_matmul(x_parts, norm_g[i, 0], mods, (w_qkv, w_og), j, sh=0, sc=1,
                                       out_dtypes=(BF16, F32), tm=tm // 2, **rows)
            for row0, nb, seq, is_ctx in trunks:
                nsub = math.gcd(nb, max(1, ts // seq))
                res = _mlstm_core(qkv, og, bg[j], mlstm_norm_g[j], None if is_ctx else mlstm_state_in, y,
                                  mlstm_state_out if is_ctx else None, layer=j, n_layers=n_ml, row0=row0,
                                  batch=nb, seq=seq, nsub=nsub, heads=heads, dk=dk, dv=dv, want_state=is_ctx)
                y = res[0]
                if is_ctx:
                    mlstm_state_out = res[1:]
        elif kind == 1:
            for row0, nb, seq, _ in trunks:
                y = _fnet_mix(x_parts[0][0], norm_g[i, 0], mods, y, row0=row0, batch=nb, seq=seq,
                              nsub=math.gcd(nb, max(1, ts // seq)), **rows)
            b_out = fnet_b_out[j]
        else:
            gx, = _norm_mod_matmul(x_parts, norm_g[i, 0], mods, (lru_w_in_b,), j, sh=0, sc=1, out_dtypes=(F32,),
                                   tm=tm, **rows)
            for row0, nb, seq, is_ctx in trunks:
                res = _lru_core(gx, lru_conv_w[j], lru_conv_b[j], lru_gate_w_b[j], lru_gate_b[j],
                                lru_lambda[j], None if is_ctx else state_lru_h[:, j], y,
                                row0=row0, batch=nb, seq=seq, nsub=math.gcd(nb, max(1, ts // seq)),
                                want_state=is_ctx)
                y = res[0]
                if is_ctx:
                    new_h.append(res[1])
        x = _matmul_residual(y, w_out_of[kind], j, b_out, x_parts, norm_g[i, 1], mods, gate=2, tm=tm, **rows)
        ffn = functools.partial(_ffn, x, norm_g[i, 2], norm_g[i, 3], mods, ffn_w_up_b, ffn_w_down, i,
                                tm=tm_ffn, rows_ctx=rows_ctx, rows_per_sample=ts)
        if i < depth - 1:
            x_parts = [(ffn(row0=0, nrows=rows_all), 0)]
        else:
            y_prompt = ffn(row0=0, nrows=rows_ctx).reshape(bp, tp, d)
            y_sample = ffn(row0=rows_ctx, nrows=rows_all - rows_ctx).reshape(bs_, ts, d)

    new_c, new_n, new_m = mlstm_state_out
    return (y_prompt, y_sample, new_c, new_n.reshape(bp, n_ml, 2, heads, dk),
            new_m[..., 0].transpose(0, 1, 3, 2), jnp.stack(new_h, 1))
```
